```python
import math
import jax
import jax.numpy as jnp
from jax import lax
import numpy as np

D_MODEL = 1024
BATCH = 32
SEQ = 256
DEPTH = 2
DEC_BATCH = 2
DEC_SEQ = 4096
PAST_LEN = 256

GRID_W = 64
CHUNK = 64
Q_BLOCK = 128
CONV_W = 3
EPS = 1e-6
GDN_HEADS = 4
GDN_DK = 128
GDN_DV = 128
GDN_WIDTH = GDN_HEADS * GDN_DV
N_HEADS = 8
KV_HEADS = 2
HEAD_DIM = 64
GROUP = N_HEADS // KV_HEADS
ATTN_WIDTH = N_HEADS * HEAD_DIM
KV_WIDTH = KV_HEADS * HEAD_DIM
ATTN_SCALE = HEAD_DIM ** -0.5
ROT_AXIS = HEAD_DIM // 2
ROPE_BASE = 10000.0
SSM_HEADS = 16
SSM_HEAD_DIM = 64
D_INNER = SSM_HEADS * SSM_HEAD_DIM
SSM_GROUPS = 2
SSM_STATE = 128
SSM_XBC = D_INNER + 2 * SSM_GROUPS * SSM_STATE
IN_SPLITS = (3 * GDN_WIDTH, GDN_WIDTH, 2 * GDN_HEADS, 2 * GDN_HEADS,
             ATTN_WIDTH, KV_WIDTH, KV_WIDTH, ATTN_WIDTH,
             SSM_XBC, D_INNER, 2 * SSM_HEADS,
             3 * D_MODEL)
IN_WIDTH = sum(IN_SPLITS)

kernel_name = 'hybrid_gdn_gqa_ssd_prefix_dit_step'


def split_cols(a, widths):
    out, start = [], 0
    for w in widths:
        out.append(a[..., start:start + w])
        start += w
    return out


def rev(a):
    return jnp.flip(a, axis=1)


def rms_norm(x, w):
    xf = x.astype(jnp.float32)
    y = xf * lax.rsqrt(jnp.mean(xf * xf, axis=-1, keepdims=True) + EPS)
    return (y * w.astype(jnp.float32)).astype(x.dtype)


def l2_normalize(x):
    xf = x.astype(jnp.float32)
    return (xf * lax.rsqrt(jnp.sum(xf * xf, axis=-1, keepdims=True) + EPS)).astype(x.dtype)


def centred_dwconv(x, w):
    ch = x.shape[-1]
    return lax.conv_general_dilated(
        x, w[:, None, :].astype(x.dtype), window_strides=(1,),
        padding=[(CONV_W // 2, CONV_W // 2)],
        dimension_numbers=('NWC', 'WIO', 'NWC'), feature_group_count=ch)


def axial_rope(x):
    t = x.shape[1]
    rows = t // GRID_W
    row = jnp.repeat(jnp.arange(rows), GRID_W)
    col = jnp.tile(jnp.arange(GRID_W), rows)
    freqs = ROPE_BASE ** (-jnp.arange(ROT_AXIS // 2, dtype=jnp.float32) / (ROT_AXIS // 2))

    def rotate(xa, pos):
        ang = pos.astype(jnp.float32)[:, None] * freqs
        cos = jnp.cos(ang)[None, :, None, :]
        sin = jnp.sin(ang)[None, :, None, :]
        x1, x2 = jnp.split(xa.astype(jnp.float32), 2, axis=-1)
        return jnp.concatenate([x1 * cos - x2 * sin, x2 * cos + x1 * sin], axis=-1)

    out = jnp.concatenate([rotate(x[..., :ROT_AXIS], row), rotate(x[..., ROT_AXIS:], col)], axis=-1)
    return out.astype(x.dtype)


def blocked_attention(q, k, v):
    b, t = q.shape[:2]
    nb = t // Q_BLOCK
    qb = jnp.moveaxis(q.reshape(b, nb, Q_BLOCK, KV_HEADS, GROUP, HEAD_DIM), 1, 0)

    def attend(q_blk):
        s = jnp.einsum('bqkgd,bskd->bkgqs', q_blk, k).astype(jnp.float32) * ATTN_SCALE
        p = jax.nn.softmax(s, axis=-1).astype(v.dtype)
        return jnp.einsum('bkgqs,bskd->bqkgd', p, v)

    o = lax.map(attend, qb)
    return jnp.moveaxis(o, 0, 1).reshape(b, t, ATTN_WIDTH)


def gdn_scan(q, k, v, g, beta, s0):
    out_dtype = v.dtype
    q, k, v, g, beta = (a.astype(jnp.float32) for a in (q, k, v, g, beta))
    b, t, h, dk = k.shape
    dv = v.shape[-1]
    n = t // CHUNK

    def blocks(a):
        a = a.reshape(b, n, CHUNK, h, *a.shape[3:])
        return jnp.moveaxis(a, (1, 2), (0, 3))

    qc, kc, vc, gc, bc = (blocks(a) for a in (q, k, v, g, beta))
    gcum = jnp.cumsum(gc, axis=-1)
    lower = jnp.tril(jnp.ones((CHUNK, CHUNK), dtype=bool))
    strict = jnp.tril(jnp.ones((CHUNK, CHUNK), dtype=bool), -1)
    decay = jnp.exp(jnp.where(lower, gcum[..., :, None] - gcum[..., None, :], -jnp.inf))
    kb = kc * bc[..., None]
    a_mat = jnp.where(strict, jnp.einsum('nbhid,nbhjd->nbhij', kb, kc) * decay, 0.0) \
        + jnp.eye(CHUNK, dtype=jnp.float32)
    rhs = jnp.concatenate([vc * bc[..., None], kb * jnp.exp(gcum)[..., None]], axis=-1)
    sol = lax.linalg.triangular_solve(a_mat, rhs, left_side=True, lower=True)
    u, w = sol[..., :dv], sol[..., dv:]
    intra = jnp.einsum('nbhid,nbhjd->nbhij', qc, kc) * decay
    q_dec = qc * jnp.exp(gcum)[..., None]
    k_dec = kc * jnp.exp(gcum[..., -1:] - gcum)[..., None]
    g_last = jnp.exp(gcum[..., -1])

    def step(S, xs):
        q_i, k_i, u_i, w_i, a_i, gl = xs
        v_new = u_i - jnp.einsum('bhcd,bhde->bhce', w_i, S)
        o = jnp.einsum('bhcd,bhde->bhce', q_i, S) + jnp.einsum('bhij,bhje->bhie', a_i, v_new)
        S = S * gl[..., None, None] + jnp.einsum('bhcd,bhce->bhde', k_i, v_new)
        return S, o

    s_fin, o = lax.scan(step, s0.astype(jnp.float32), (q_dec, k_dec, u, w, intra, g_last))
    o = jnp.moveaxis(o, (0, 3), (1, 2)).reshape(b, t, h, dv)
    return o.astype(out_dtype), s_fin


def ssd_scan(x, a, bm, cm, s0):
    out_dtype = x.dtype
    x, a, bm, cm = (z.astype(jnp.float32) for z in (x, a, bm, cm))
    b, t, h, p = x.shape
    g = bm.shape[2]
    r = h // g
    nc = t // CHUNK
    xc = x.reshape(b, nc, CHUNK, g, r, p)
    bc = bm.reshape(b, nc, CHUNK, g, SSM_STATE)
    cc = cm.reshape(b, nc, CHUNK, g, SSM_STATE)
    ac = jnp.moveaxis(a.reshape(b, nc, CHUNK, g, r), 2, -1)
    acum = jnp.cumsum(ac, axis=-1)
    lower = jnp.tril(jnp.ones((CHUNK, CHUNK), dtype=bool))
    lmat = jnp.exp(jnp.where(lower, acum[..., :, None] - acum[..., None, :], -jnp.inf))
    cb = jnp.einsum('bclgn,bcsgn->bcgls', cc, bc)
    y_diag = jnp.einsum('bcgls,bcgrls,bcsgrp->bclgrp', cb, lmat, xc)
    dstate = jnp.exp(acum[..., -1:] - acum)
    states = jnp.einsum('bclgn,bcgrl,bclgrp->bcgrpn', bc, dstate, xc)
    decay_in = jnp.exp(acum)
    chunk_decay = jnp.exp(acum[..., -1])

    def step(S, xs):
        c_i, st_i, din_i, cd_i = xs
        y_off = jnp.einsum('blgn,bgrpn,bgrl->blgrp', c_i, S, din_i)
        S = S * cd_i[..., None, None] + st_i
        return S, y_off

    xs = tuple(jnp.moveaxis(z, 1, 0) for z in (cc, states, decay_in, chunk_decay))
    s_fin, y_off = lax.scan(step, s0.astype(jnp.float32).reshape(b, g, r, p, SSM_STATE), xs)
    y = y_diag + jnp.moveaxis(y_off, 0, 1)
    return y.reshape(b, t, h, p).astype(out_dtype), s_fin.reshape(b, h, p, SSM_STATE)


def modulation(cond, w_mod, b_mod):
    m = jnp.einsum('...d,de->...e', jax.nn.silu(cond), w_mod) + b_mod
    return split_cols(m, (D_MODEL, D_MODEL, D_MODEL))


def mixer(h, lp, ctx):
    b, t, _ = h.shape
    proj = jnp.einsum('btd,de->bte', h, lp['w_in'])
    (a_qkv, a_z, a_beta, a_alpha, b_q, b_k, b_v, b_z,
     c_xbc, c_z, c_dt, gates) = split_cols(proj, IN_SPLITS)

    a_qkv = jax.nn.silu(centred_dwconv(a_qkv, lp['gdn_conv_w']))
    a_q, a_k, a_v = split_cols(a_qkv, (GDN_WIDTH, GDN_WIDTH, GDN_WIDTH))
    a_q = l2_normalize(a_q.reshape(b, t, GDN_HEADS, GDN_DK)) * (GDN_DK ** -0.5)
    a_k = l2_normalize(a_k.reshape(b, t, GDN_HEADS, GDN_DK))
    a_v = a_v.reshape(b, t, GDN_HEADS, GDN_DV)
    beta = jax.nn.sigmoid(a_beta.reshape(b, t, 2, GDN_HEADS))
    log_decay = -jnp.exp(lp['gdn_a_log']) * jax.nn.softplus(
        a_alpha.reshape(b, t, 2, GDN_HEADS) + lp['gdn_dt_bias'])
    s_gdn0 = jnp.zeros((b, 2, GDN_HEADS, GDN_DK, GDN_DV), jnp.float32) if ctx is None else ctx[2]
    o_f, sg_f = gdn_scan(a_q, a_k, a_v, log_decay[:, :, 0], beta[:, :, 0], s_gdn0[:, 0])
    o_b, sg_b = gdn_scan(rev(a_q), rev(a_k), rev(a_v), rev(log_decay[:, :, 1]),
                         rev(beta[:, :, 1]), s_gdn0[:, 1])
    o_a = rms_norm(o_f + rev(o_b), lp['gdn_norm_w']) * jax.nn.silu(a_z.reshape(b, t, GDN_HEADS, GDN_DV))
    branch_a = jnp.einsum('bte,ed->btd', o_a.reshape(b, t, GDN_WIDTH), lp['w_branch_a'])

    q = rms_norm(b_q.reshape(b, t, N_HEADS, HEAD_DIM), lp['attn_q_norm'])
    k = rms_norm(b_k.reshape(b, t, KV_HEADS, HEAD_DIM), lp['attn_k_norm'])
    v = b_v.reshape(b, t, KV_HEADS, HEAD_DIM)
    if ctx is None:
        keys, vals = k, v
    else:
        q = axial_rope(q)
        keys = jnp.concatenate([axial_rope(k), ctx[0].astype(k.dtype)], axis=1)
        vals = jnp.concatenate([v, ctx[1].astype(v.dtype)], axis=1)
    o_att = blocked_attention(q, keys, vals) * jax.nn.silu(b_z)
    branch_b = jnp.einsum('bte,ed->btd', o_att, lp['w_branch_b'])

    xbc = jax.nn.silu(centred_dwconv(c_xbc, lp['ssm_conv_w']) + lp['ssm_conv_b'])
    c_x, c_b, c_c = split_cols(xbc, (D_INNER, SSM_GROUPS * SSM_STATE, SSM_GROUPS * SSM_STATE))
    c_x = c_x.reshape(b, t, SSM_HEADS, SSM_HEAD_DIM)
    c_b = c_b.reshape(b, t, SSM_GROUPS, SSM_STATE)
    c_c = c_c.reshape(b, t, SSM_GROUPS, SSM_STATE)
    dt = jax.nn.softplus(c_dt.reshape(b, t, 2, SSM_HEADS) + lp['ssm_dt_bias'])
    da = dt * (-jnp.exp(lp['ssm_a_log']))
    s_ssm0 = jnp.zeros((b, 2, SSM_HEADS, SSM_HEAD_DIM, SSM_STATE), jnp.float32) if ctx is None else ctx[3]
    y_f, ss_f = ssd_scan(c_x * dt[:, :, 0, :, None], da[:, :, 0], c_b, c_c, s_ssm0[:, 0])
    y_b, ss_b = ssd_scan(rev(c_x * dt[:, :, 1, :, None]), rev(da[:, :, 1]), rev(c_b), rev(c_c), s_ssm0[:, 1])
    y = y_f + rev(y_b) + lp['ssm_d'][:, None] * c_x
    y = rms_norm(y.reshape(b, t, D_INNER) * jax.nn.silu(c_z), lp['ssm_norm_w'])
    branch_c = jnp.einsum('bte,ed->btd', y, lp['w_branch_c'])

    g_a, g_b, g_c = split_cols(gates, (D_MODEL, D_MODEL, D_MODEL))
    merged = jax.nn.sigmoid(g_a) * branch_a + jax.nn.sigmoid(g_b) * branch_b + jax.nn.sigmoid(g_c) * branch_c
    out = jnp.einsum('btd,de->bte', merged, lp['w_out'])
    ctx_out = None
    if ctx is None:
        ctx_out = (k, v, jnp.stack([sg_f, sg_b], axis=1), jnp.stack([ss_f, ss_b], axis=1))
    return out, ctx_out


def setup_inputs(seed: int = 0) -> dict:
    key = jax.random.key(seed)
    ks = iter(jax.random.split(key, 40))
    nrm = lambda shape, s: jax.random.normal(next(ks), shape, jnp.float32) * s

    def a_log(shape):
        return jnp.log(jax.random.uniform(next(ks), shape, jnp.float32, 1.0, 16.0))

    def dt_bias(shape):
        dt = jnp.exp(jax.random.uniform(next(ks), shape, jnp.float32, math.log(1e-3), math.log(1e-1)))
        return dt + jnp.log(-jnp.expm1(-dt))

    return {
        'x_prompt': nrm((BATCH, SEQ, D_MODEL), 1.0),
        'x_sample': nrm((DEC_BATCH, DEC_SEQ, D_MODEL), 1.0),
        'cache_k': nrm((DEC_BATCH, DEPTH, PAST_LEN, KV_HEADS, HEAD_DIM), 1.0),
        'cache_v': nrm((DEC_BATCH, DEPTH, PAST_LEN, KV_HEADS, HEAD_DIM), 1.0),
        'state_gdn': nrm((DEC_BATCH, DEPTH, 2, GDN_HEADS, GDN_DK, GDN_DV), 0.3),
        'state_ssm': nrm((DEC_BATCH, DEPTH, 2, SSM_HEADS, SSM_HEAD_DIM, SSM_STATE), 0.1),
        'c': nrm((DEC_BATCH, D_MODEL), 1.0),
        'c_ctx': nrm((D_MODEL,), 1.0),
        'norm_w': 1.0 + nrm((DEPTH, D_MODEL), 0.02),
        'w_mod': nrm((DEPTH, D_MODEL, 3 * D_MODEL), 0.5 * D_MODEL ** -0.5),
        'b_mod': nrm((DEPTH, 3 * D_MODEL), 0.02),
        'w_in': nrm((DEPTH, D_MODEL, IN_WIDTH), D_MODEL ** -0.5),
        'gdn_conv_w': nrm((DEPTH, CONV_W, 3 * GDN_WIDTH), CONV_W ** -0.5),
        'gdn_a_log': a_log((DEPTH, 2, GDN_HEADS)),
        'gdn_dt_bias': dt_bias((DEPTH, 2, GDN_HEADS)),
        'gdn_norm_w': 1.0 + nrm((DEPTH, GDN_DV), 0.02),
        'attn_q_norm': 1.0 + nrm((DEPTH, HEAD_DIM), 0.02),
        'attn_k_norm': 1.0 + nrm((DEPTH, HEAD_DIM), 0.02),
        'ssm_conv_w': nrm((DEPTH, CONV_W, SSM_XBC), CONV_W ** -0.5),
        'ssm_conv_b': nrm((DEPTH, SSM_XBC), 0.02),
        'ssm_a_log': a_log((DEPTH, 2, SSM_HEADS)),
        'ssm_dt_bias': dt_bias((DEPTH, 2, SSM_HEADS)),
        'ssm_d': 1.0 + nrm((DEPTH, SSM_HEADS), 0.1),
        'ssm_norm_w': 1.0 + nrm((DEPTH, D_INNER), 0.02),
        'w_branch_a': nrm((DEPTH, GDN_WIDTH, D_MODEL), GDN_WIDTH ** -0.5),
        'w_branch_b': nrm((DEPTH, ATTN_WIDTH, D_MODEL), ATTN_WIDTH ** -0.5),
        'w_branch_c': nrm((DEPTH, D_INNER, D_MODEL), D_INNER ** -0.5),
        'w_out': nrm((DEPTH, D_MODEL, D_MODEL), D_MODEL ** -0.5),
        'final_norm_w': 1.0 + nrm((D_MODEL,), 0.02),
    }


def reference(x_prompt, x_sample, cache_k, cache_v, state_gdn, state_ssm, c, c_ctx,
              norm_w, w_mod, b_mod, w_in, gdn_conv_w, gdn_a_log, gdn_dt_bias, gdn_norm_w,
              attn_q_norm, attn_k_norm, ssm_conv_w, ssm_conv_b, ssm_a_log, ssm_dt_bias,
              ssm_d, ssm_norm_w, w_branch_a, w_branch_b, w_branch_c, w_out, final_norm_w):
    xp = x_prompt
    xs = x_sample
    new_k, new_v, new_gdn, new_ssm = [], [], [], []
    for l in range(DEPTH):
        lp = {
            'w_in': w_in[l], 'gdn_conv_w': gdn_conv_w[l], 'gdn_a_log': gdn_a_log[l],
            'gdn_dt_bias': gdn_dt_bias[l], 'gdn_norm_w': gdn_norm_w[l],
            'attn_q_norm': attn_q_norm[l], 'attn_k_norm': attn_k_norm[l],
            'ssm_conv_w': ssm_conv_w[l], 'ssm_conv_b': ssm_conv_b[l], 'ssm_a_log': ssm_a_log[l],
            'ssm_dt_bias': ssm_dt_bias[l], 'ssm_d': ssm_d[l], 'ssm_norm_w': ssm_norm_w[l],
            'w_branch_a': w_branch_a[l], 'w_branch_b': w_branch_b[l], 'w_branch_c': w_branch_c[l],
            'w_out': w_out[l],
        }
        shift, scale, gate = modulation(c_ctx, w_mod[l], b_mod[l])
        h = rms_norm(xp, norm_w[l]) * (1.0 + scale) + shift
        out, (k_c, v_c, s_g, s_s) = mixer(h, lp, None)
        xp = xp + gate * out
        new_k.append(k_c)
        new_v.append(v_c)
        new_gdn.append(s_g)
        new_ssm.append(s_s)
        shift, scale, gate = modulation(c, w_mod[l], b_mod[l])
        h = rms_norm(xs, norm_w[l]) * (1.0 + scale[:, None, :]) + shift[:, None, :]
        out, _ = mixer(h, lp, (cache_k[:, l], cache_v[:, l], state_gdn[:, l], state_ssm[:, l]))
        xs = xs + gate[:, None, :] * out
    y_prompt = rms_norm(xp, final_norm_w)
    y_sample = rms_norm(xs, final_norm_w)
    new_cache_k = jnp.stack(new_k, axis=1)
    new_cache_v = jnp.stack(new_v, axis=1)
    new_state_gdn = jnp.stack(new_gdn, axis=1)
    new_state_ssm = jnp.stack(new_ssm, axis=1)
    return (y_prompt, y_sample, new_cache_k, new_cache_v, new_state_gdn, new_state_ssm)
```

```python
import functools
import math

import jax
import jax.numpy as jnp
from jax import lax
from jax.experimental import pallas as pl
from jax.experimental.pallas import tpu as pltpu

F32 = jnp.float32
BF16 = jnp.bfloat16

D_MODEL = 1024
EPS = 1e-6
GDN_HEADS = 4
GDN_DK = 128
GDN_WIDTH = 512
N_HEADS = 8
KV_HEADS = 2
HEAD_DIM = 64
ATTN_WIDTH = 512
KV_WIDTH = 128
GRID_W = 64
ROPE_BASE = 10000.0
SSM_HEADS = 16
SSM_HEAD_DIM = 64
D_INNER = 1024
SSM_STATE = 128
SSM_XBC = 1536
CHUNK = 128
LANES = 128
SMALL_ROWS = 48
VMEM_LIMIT = 56 * 1024 * 1024

_SRC = dict(qkv=(0, 1536), az=(1536, 512), beta=(2048, 8), alpha=(2056, 8), bq=(2064, 512),
            kv=(2576, 256), bz=(2832, 512), xbc=(3344, 1536), cz=(4880, 1024), dt=(5904, 32),
            gates=(5936, 3072))
_MAIN = ("qkv", "az", "bq", "kv", "bz", "xbc", "cz", "gates")
_MAIN_OFF = {}
_o = 0
for _n in _MAIN:
    _MAIN_OFF[_n] = _o
    _o += _SRC[_n][1]
MAIN_WIDTH = _o


def _nt(a, b):
    return lax.dot_general(a, b, (((1,), (1,)), ((), ())), preferred_element_type=F32)


def _tn(a, b):
    return lax.dot_general(a, b, (((0,), (0,)), ((), ())), preferred_element_type=F32)


def _mm(a, b):
    return jnp.dot(a, b, preferred_element_type=F32)


def _silu(x):
    return x * jax.nn.sigmoid(x)


def _softplus(x):
    return jnp.maximum(x, 0.0) + jnp.log1p(jnp.exp(-jnp.abs(x)))


def _scan(x, axis, reverse):
    n = x.shape[axis]
    idx = lax.broadcasted_iota(jnp.int32, x.shape, axis)
    k = 1
    while k < n:
        if reverse:
            x = x + jnp.where(idx < n - k, pltpu.roll(x, n - k, axis), 0.0)
        else:
            x = x + jnp.where(idx >= k, pltpu.roll(x, k, axis), 0.0)
        k *= 2
    return x


def _pick_row(block, idx):
    rows = lax.broadcasted_iota(jnp.int32, block.shape, 0)
    return jnp.sum(jnp.where(rows == idx, block, 0.0), axis=0, keepdims=True)


def _to_col(row, eye):
    return jnp.sum(jnp.where(eye, row, 0.0), axis=1, keepdims=True)


def _conv_tile(src_ref, r, rt, n_tiles, w, bias):
    t_len = src_ref.shape[0]
    start = pl.multiple_of(r * rt, rt)
    cur = src_ref[pl.ds(start, rt), :].astype(F32)
    pstart = pl.multiple_of(jnp.maximum(start - 16, 0), 16)
    nstart = pl.multiple_of(jnp.minimum(start + rt, t_len - 16), 16)
    prev_row = src_ref[pl.ds(pstart, 16), :][15:16, :].astype(F32)
    next_row = src_ref[pl.ds(nstart, 16), :][0:1, :].astype(F32)
    prev_row = jnp.where(r > 0, prev_row, 0.0)
    next_row = jnp.where(r < n_tiles - 1, next_row, 0.0)
    ri = lax.broadcasted_iota(jnp.int32, (rt, LANES), 0)
    x_prev = jnp.where(ri == 0, prev_row, pltpu.roll(cur, 1, 0))
    x_next = jnp.where(ri == rt - 1, next_row, pltpu.roll(cur, rt - 1, 0))
    y = w[0:1, :] * x_prev + w[1:2, :] * cur + w[2:3, :] * x_next
    if bias is not None:
        y = y + bias
    return y


def _row_tile(t_len):
    return min(t_len, 256)


def _mod_kernel(c_ref, w_ref, b_ref, o_ref):
    s = _silu(c_ref[...])
    w = w_ref[...]
    s_hi = s.astype(BF16)
    s_lo = (s - s_hi.astype(F32)).astype(BF16)
    w_hi = w.astype(BF16)
    w_lo = (w - w_hi.astype(F32)).astype(BF16)
    o_ref[...] = _mm(s_hi, w_hi) + _mm(s_lo, w_hi) + _mm(s_hi, w_lo) + b_ref[...]


def _modulation(cond, w_mod, b_mod):
    depth = w_mod.shape[0]
    rows = cond.shape[0]
    tn = 1024
    return pl.pallas_call(
        _mod_kernel,
        grid=(depth, 3 * D_MODEL // tn),
        in_specs=[
            pl.BlockSpec((rows, D_MODEL), lambda l, j: (0, 0)),
            pl.BlockSpec((None, D_MODEL, tn), lambda l, j: (l, 0, j)),
            pl.BlockSpec((None, 1, tn), lambda l, j: (l, 0, j)),
        ],
        out_specs=pl.BlockSpec((None, rows, tn), lambda l, j: (l, 0, j)),
        out_shape=jax.ShapeDtypeStruct((depth, rows, 3 * D_MODEL), F32),
        name="modulation",
    )(cond, w_mod, b_mod.reshape(depth, 1, 3 * D_MODEL))


def _in_kernel(x_ref, nw_ref, mod_ref, w_ref, wst_ref,
               o_qkv, o_az, o_bq, o_kv, o_bz, o_xbc, o_cz, o_gates, o_small):
    x = x_ref[...]
    y = x * lax.rsqrt(jnp.mean(x * x, axis=-1, keepdims=True) + EPS) * nw_ref[...]
    shift = mod_ref[:, 0:D_MODEL]
    scale = mod_ref[:, D_MODEL:2 * D_MODEL]
    h = (y * (1.0 + scale) + shift).astype(BF16)
    outs = dict(qkv=o_qkv, az=o_az, bq=o_bq, kv=o_kv, bz=o_bz, xbc=o_xbc, cz=o_cz, gates=o_gates)
    for name in _MAIN:
        off, width = _MAIN_OFF[name], _SRC[name][1]
        outs[name][...] = _mm(h, w_ref[:, off:off + width]).astype(outs[name].dtype)
    o_small[...] = _nt(wst_ref[...], h)


def _in_proj(x, norm_w, mod, w_main, w_small_t, *, t_len, mod_base, mod_per_batch):
    n = x.shape[0]
    tm = 256
    assert n % tm == 0 and t_len % tm == 0

    def mod_idx(i):
        return (mod_base + (i * tm) // t_len if mod_per_batch else mod_base, 0, 0)

    widths = {k: _SRC[k][1] for k in _MAIN}
    dtypes = {k: (F32 if k == "kv" else BF16) for k in _MAIN}
    out_shape = [jax.ShapeDtypeStruct((n, widths[k]), dtypes[k]) for k in _MAIN]
    out_shape.append(jax.ShapeDtypeStruct((SMALL_ROWS, n), F32))
    out_specs = [pl.BlockSpec((tm, widths[k]), lambda i: (i, 0)) for k in _MAIN]
    out_specs.append(pl.BlockSpec((SMALL_ROWS, tm), lambda i: (0, i)))
    outs = pl.pallas_call(
        _in_kernel,
        grid=(n // tm,),
        in_specs=[
            pl.BlockSpec((tm, D_MODEL), lambda i: (i, 0)),
            pl.BlockSpec((1, D_MODEL), lambda i: (0, 0)),
            pl.BlockSpec((None, 1, 3 * D_MODEL), mod_idx),
            pl.BlockSpec((D_MODEL, MAIN_WIDTH), lambda i: (0, 0), pipeline_mode=pl.Buffered(1)),
            pl.BlockSpec((SMALL_ROWS, D_MODEL), lambda i: (0, 0)),
        ],
        out_specs=out_specs,
        out_shape=out_shape,
        compiler_params=pltpu.CompilerParams(
            dimension_semantics=("arbitrary",), vmem_limit_bytes=VMEM_LIMIT),
        name="in_proj",
    )(x, norm_w, mod, w_main, w_small_t)
    res = dict(zip(_MAIN, outs[:-1]))
    res["small_t"] = outs[-1]
    return res


INV_BASE = 8


def _tri_inverse(nm, sub, lane, eye_f):
    def same_block(s):
        sh = int(math.log2(s))
        return (sub >> sh) == (lane >> sh)

    n8 = jnp.where(same_block(INV_BASE), nm, 0.0)
    p = n8.astype(BF16)
    y = eye_f - n8
    p2 = _mm(p, p).astype(BF16)
    both = _mm(p2, jnp.concatenate([p2, y.astype(BF16)], axis=1))
    y = y + both[:, CHUNK:]
    y = y + _mm(both[:, :CHUNK].astype(BF16), y.astype(BF16))
    s = INV_BASE
    while s < CHUNK:
        off = jnp.where(jnp.logical_and(same_block(2 * s), jnp.logical_not(same_block(s))), nm, 0.0)
        yb = y.astype(BF16)
        y = y - _mm(_mm(yb, off.astype(BF16)).astype(BF16), yb)
        s *= 2
    return y


def _gdn_kernel(*refs, t_len, has_ctx):
    if has_ctx:
        (q_ref, k_ref, v_ref, z_ref, gt_ref, wq_ref, wk_ref, wv_ref, nw_ref, alog_ref, bias_ref,
         s0_ref, o_ref, qs, ks, vs, of, ob) = refs
        sout_ref = None
    else:
        (q_ref, k_ref, v_ref, z_ref, gt_ref, wq_ref, wk_ref, wv_ref, nw_ref, alog_ref, bias_ref,
         o_ref, sout_ref, qs, ks, vs, of, ob) = refs
        s0_ref = None
    h = pl.program_id(1)
    rt = _row_tile(t_len)
    n_tiles = t_len // rt
    n_chunks = t_len // CHUNK

    def prep(r, carry):
        start = pl.multiple_of(r * rt, rt)
        q = _silu(_conv_tile(q_ref, r, rt, n_tiles, wq_ref[...], None))
        k = _silu(_conv_tile(k_ref, r, rt, n_tiles, wk_ref[...], None))
        v = _silu(_conv_tile(v_ref, r, rt, n_tiles, wv_ref[...], None))
        q = q * lax.rsqrt(jnp.sum(q * q, axis=-1, keepdims=True) + EPS) * (GDN_DK ** -0.5)
        k = k * lax.rsqrt(jnp.sum(k * k, axis=-1, keepdims=True) + EPS)
        qs[pl.ds(start, rt), :] = q
        ks[pl.ds(start, rt), :] = k
        vs[pl.ds(start, rt), :] = v
        return carry

    lax.fori_loop(0, n_tiles, prep, 0)

    sub = lax.broadcasted_iota(jnp.int32, (CHUNK, CHUNK), 0)
    lane = lax.broadcasted_iota(jnp.int32, (CHUNK, CHUNK), 1)
    eye = sub == lane
    eye_f = eye.astype(F32)

    def chunk_step(d, cc, state):
        c0 = pl.multiple_of(cc * CHUNK, CHUNK)
        q = qs[pl.ds(c0, CHUNK), :]
        k = ks[pl.ds(c0, CHUNK), :]
        v = vs[pl.ds(c0, CHUNK), :]
        gates = gt_ref[:, pl.ds(c0, CHUNK)]
        beta = jax.nn.sigmoid(_pick_row(gates, d * GDN_HEADS + h))
        alpha = _pick_row(gates, 2 * GDN_HEADS + d * GDN_HEADS + h)
        neg_a = -jnp.exp(jnp.full((1, CHUNK), alog_ref[d, h], F32))
        g = neg_a * _softplus(alpha + bias_ref[d, h])
        gc = _scan(g, 1, reverse=(d == 1))
        beta_col = _to_col(beta, eye)
        gc_col = _to_col(gc, eye)
        incl = (lane <= sub) if d == 0 else (lane >= sub)
        strict = (lane < sub) if d == 0 else (lane > sub)
        kb = k.astype(BF16)
        kk = _nt(kb, kb)
        qk = _nt(q.astype(BF16), kb)
        dec = jnp.where(incl, jnp.exp(jnp.where(incl, gc_col - gc, 0.0)), 0.0)
        nm = jnp.where(strict, kk * dec * beta_col, 0.0)
        t_inv = _tri_inverse(nm, sub, lane, eye_f)
        eg = jnp.exp(gc_col)
        rhs = jnp.concatenate([v * beta_col, k * (beta_col * eg)], axis=1)
        sol = _mm(t_inv.astype(BF16), rhs.astype(BF16))
        u = sol[:, :CHUNK]
        w = sol[:, CHUNK:]
        tot = gc[:, CHUNK - 1:CHUNK] if d == 0 else gc[:, 0:1]
        q_dec = (q * eg).astype(BF16)
        k_dec = (k * jnp.exp(tot - gc_col)).astype(BF16)
        sb = state.astype(BF16)
        v_new = u - _mm(w.astype(BF16), sb)
        vb = v_new.astype(BF16)
        o = _mm(q_dec, sb) + _mm((qk * dec).astype(BF16), vb)
        state = state * jnp.exp(tot) + _tn(k_dec, vb)
        return o, state

    def body(c, carry):
        s_f, s_b = carry
        o_f, s_f = chunk_step(0, c, s_f)
        of[pl.ds(pl.multiple_of(c * CHUNK, CHUNK), CHUNK), :] = o_f
        cb = n_chunks - 1 - c
        o_b, s_b = chunk_step(1, cb, s_b)
        ob[pl.ds(pl.multiple_of(cb * CHUNK, CHUNK), CHUNK), :] = o_b
        return s_f, s_b

    if has_ctx:
        init = (s0_ref[0], s0_ref[1])
    else:
        init = (jnp.zeros((GDN_DK, CHUNK), F32), jnp.zeros((GDN_DK, CHUNK), F32))
    s_f, s_b = lax.fori_loop(0, n_chunks, body, init)
    if sout_ref is not None:
        sout_ref[0] = s_f
        sout_ref[1] = s_b

    def fin(r, carry):
        start = pl.multiple_of(r * rt, rt)
        o = of[pl.ds(start, rt), :] + ob[pl.ds(start, rt), :]
        o = o * lax.rsqrt(jnp.mean(o * o, axis=-1, keepdims=True) + EPS) * nw_ref[...]
        o_ref[pl.ds(start, rt), :] = (o * _silu(z_ref[pl.ds(start, rt), :].astype(F32))).astype(o_ref.dtype)
        return carry

    lax.fori_loop(0, n_tiles, fin, 0)


def _gdn(p, conv_w, norm_w, a_log, dt_bias, s0, *, batch, t_len):
    n = batch * t_len
    has_ctx = s0 is not None
    seq = lambda off: pl.BlockSpec((t_len, LANES), lambda b, h, off=off: (b, off + h))
    cw = lambda off: pl.BlockSpec((3, LANES), lambda b, h, off=off: (0, off + h))
    smem = pl.BlockSpec(memory_space=pltpu.SMEM)
    state_spec = pl.BlockSpec((None, 2, None, GDN_DK, CHUNK), lambda b, h: (b, 0, h, 0, 0))
    in_specs = [seq(0), seq(GDN_HEADS), seq(2 * GDN_HEADS), seq(0),
                pl.BlockSpec((16, t_len), lambda b, h: (2, b)),
                cw(0), cw(GDN_HEADS), cw(2 * GDN_HEADS),
                pl.BlockSpec((1, LANES), lambda b, h: (0, 0)), smem, smem]
    args = [p["qkv"], p["qkv"], p["qkv"], p["az"], p["small_t"], conv_w, conv_w, conv_w,
            norm_w, a_log, dt_bias]
    out_shape = [jax.ShapeDtypeStruct((n, GDN_WIDTH), BF16)]
    out_specs = [pl.BlockSpec((t_len, LANES), lambda b, h: (b, h))]
    if has_ctx:
        in_specs.append(state_spec)
        args.append(s0)
    else:
        out_shape.append(jax.ShapeDtypeStruct((batch, 2, GDN_HEADS, GDN_DK, CHUNK), F32))
        out_specs.append(state_spec)
    scratch = [pltpu.VMEM((t_len, LANES), F32) for _ in range(5)]
    outs = pl.pallas_call(
        functools.partial(_gdn_kernel, t_len=t_len, has_ctx=has_ctx),
        grid=(batch, GDN_HEADS),
        in_specs=in_specs, out_specs=out_specs, out_shape=out_shape, scratch_shapes=scratch,
        compiler_params=pltpu.CompilerParams(
            dimension_semantics=("arbitrary", "arbitrary"), vmem_limit_bytes=VMEM_LIMIT),
        name="gdn_ctx" if has_ctx else "gdn",
    )(*args)
    return outs[0], (None if has_ctx else outs[1])


def _ssd_kernel(*refs, t_len, has_ctx):
    if has_ctx:
        (x_ref, b_ref, c_ref, z_ref, dtt_ref, wx_ref, wb_ref, wc_ref, bx_ref, bb_ref, bc_ref, dvec_ref,
         alog_ref, bias_ref, s0_ref, o_ref, xs, bs, cs, yf, yb) = refs
        sout_ref = None
    else:
        (x_ref, b_ref, c_ref, z_ref, dtt_ref, wx_ref, wb_ref, wc_ref, bx_ref, bb_ref, bc_ref, dvec_ref,
         alog_ref, bias_ref, o_ref, sout_ref, xs, bs, cs, yf, yb) = refs
        s0_ref = None
    pair = pl.program_id(1)
    rt = _row_tile(t_len)
    n_tiles = t_len // rt
    n_chunks = t_len // CHUNK

    def prep(r, carry):
        start = pl.multiple_of(r * rt, rt)
        xs[pl.ds(start, rt), :] = _silu(_conv_tile(x_ref, r, rt, n_tiles, wx_ref[...], bx_ref[...]))
        bs[pl.ds(start, rt), :] = _silu(_conv_tile(b_ref, r, rt, n_tiles, wb_ref[...], bb_ref[...])).astype(BF16)
        cs[pl.ds(start, rt), :] = _silu(_conv_tile(c_ref, r, rt, n_tiles, wc_ref[...], bc_ref[...])).astype(BF16)
        return carry

    lax.fori_loop(0, n_tiles, prep, 0)

    sub = lax.broadcasted_iota(jnp.int32, (CHUNK, CHUNK), 0)
    lane = lax.broadcasted_iota(jnp.int32, (CHUNK, CHUNK), 1)
    eye = sub == lane
    head0_lane = lane < SSM_HEAD_DIM
    head0_sub = sub < SSM_HEAD_DIM

    def chunk_step(d, cc, state):
        c0 = pl.multiple_of(cc * CHUNK, CHUNK)
        x = xs[pl.ds(c0, CHUNK), :]
        bm = bs[pl.ds(c0, CHUNK), :]
        cm = cs[pl.ds(c0, CHUNK), :]
        cb = _nt(cm, bm)
        incl = (lane <= sub) if d == 0 else (lane >= sub)
        m_mats, w_cols, din_cols, cds = [], [], [], []
        dt_rows = dtt_ref[:, pl.ds(c0, CHUNK)]
        for j in range(2):
            hh = pair * 2 + j
            raw = _pick_row(dt_rows, d * SSM_HEADS + hh)
            dt = _softplus(raw + bias_ref[d, hh])
            da = dt * (-jnp.exp(jnp.full((1, CHUNK), alog_ref[d, hh], F32)))
            acum = _scan(da, 1, reverse=(d == 1))
            acum_col = _to_col(acum, eye)
            lmat = jnp.where(incl, jnp.exp(jnp.where(incl, acum_col - acum, 0.0)), 0.0)
            m_mats.append((cb * lmat * dt).astype(BF16))
            tot = acum[:, CHUNK - 1:CHUNK] if d == 0 else acum[:, 0:1]
            w_cols.append(_to_col(dt * jnp.exp(tot - acum), eye))
            din_cols.append(jnp.exp(acum_col))
            cds.append(jnp.exp(tot))
        x0 = jnp.where(head0_lane, x, 0.0).astype(BF16)
        x1 = jnp.where(head0_lane, 0.0, x).astype(BF16)
        y = _mm(m_mats[0], x0) + _mm(m_mats[1], x1)
        y = y + _nt(cm, state.astype(BF16)) * jnp.where(head0_lane, din_cols[0], din_cols[1])
        xw = (x * jnp.where(head0_lane, w_cols[0], w_cols[1])).astype(BF16)
        state = state * jnp.where(head0_sub, cds[0], cds[1]) + _tn(xw, bm)
        return y, state

    def body(c, carry):
        s_f, s_b = carry
        y_f, s_f = chunk_step(0, c, s_f)
        yf[pl.ds(pl.multiple_of(c * CHUNK, CHUNK), CHUNK), :] = y_f
        cb = n_chunks - 1 - c
        y_b, s_b = chunk_step(1, cb, s_b)
        yb[pl.ds(pl.multiple_of(cb * CHUNK, CHUNK), CHUNK), :] = y_b
        return s_f, s_b

    if has_ctx:
        init = tuple(jnp.concatenate([s0_ref[d, 0], s0_ref[d, 1]], axis=0) for d in range(2))
    else:
        init = (jnp.zeros((2 * SSM_HEAD_DIM, SSM_STATE), F32), jnp.zeros((2 * SSM_HEAD_DIM, SSM_STATE), F32))
    s_f, s_b = lax.fori_loop(0, n_chunks, body, init)
    if sout_ref is not None:
        for d, s in enumerate((s_f, s_b)):
            sout_ref[d, 0] = s[:SSM_HEAD_DIM]
            sout_ref[d, 1] = s[SSM_HEAD_DIM:]

    def fin(r, carry):
        start = pl.multiple_of(r * rt, rt)
        y = yf[pl.ds(start, rt), :] + yb[pl.ds(start, rt), :] + dvec_ref[...] * xs[pl.ds(start, rt), :]
        o_ref[pl.ds(start, rt), :] = (y * _silu(z_ref[pl.ds(start, rt), :].astype(F32))).astype(o_ref.dtype)
        return carry

    lax.fori_loop(0, n_tiles, fin, 0)


def _ssd(p, conv_w, conv_b, d_vec, a_log, dt_bias, s0, *, batch, t_len):
    n = batch * t_len
    has_ctx = s0 is not None
    n_pairs = SSM_HEADS // 2
    pairs_per_group = n_pairs // 2
    x_blocks = D_INNER // LANES

    def cols(kind):
        if kind == "x":
            return lambda b, q: q
        off = x_blocks if kind == "b" else x_blocks + 2
        return lambda b, q, off=off: off + q // pairs_per_group

    seq = lambda kind: pl.BlockSpec((t_len, LANES), lambda b, q, f=cols(kind): (b, f(b, q)))
    cw = lambda kind: pl.BlockSpec((3, LANES), lambda b, q, f=cols(kind): (0, f(b, q)))
    cbias = lambda kind: pl.BlockSpec((1, LANES), lambda b, q, f=cols(kind): (0, f(b, q)))
    smem = pl.BlockSpec(memory_space=pltpu.SMEM)
    state_spec = pl.BlockSpec((None, 2, 2, SSM_HEAD_DIM, SSM_STATE), lambda b, q: (b, 0, q, 0, 0))
    in_specs = [seq("x"), seq("b"), seq("c"),
                pl.BlockSpec((t_len, LANES), lambda b, q: (b, q)),
                pl.BlockSpec((2 * SSM_HEADS, t_len), lambda b, q: (0, b)),
                cw("x"), cw("b"), cw("c"), cbias("x"), cbias("b"), cbias("c"),
                pl.BlockSpec((1, LANES), lambda b, q: (0, q)), smem, smem]
    args = [p["xbc"], p["xbc"], p["xbc"], p["cz"], p["small_t"], conv_w, conv_w, conv_w,
            conv_b, conv_b, conv_b, d_vec, a_log, dt_bias]
    out_shape = [jax.ShapeDtypeStruct((n, D_INNER), BF16)]
    out_specs = [pl.BlockSpec((t_len, LANES), lambda b, q: (b, q))]
    if has_ctx:
        in_specs.append(state_spec)
        args.append(s0)
    else:
        out_shape.append(jax.ShapeDtypeStruct((batch, 2, SSM_HEADS, SSM_HEAD_DIM, SSM_STATE), F32))
        out_specs.append(state_spec)
    scratch = [pltpu.VMEM((t_len, LANES), F32), pltpu.VMEM((t_len, LANES), BF16),
               pltpu.VMEM((t_len, LANES), BF16), pltpu.VMEM((t_len, LANES), F32),
               pltpu.VMEM((t_len, LANES), F32)]
    outs = pl.pallas_call(
        functools.partial(_ssd_kernel, t_len=t_len, has_ctx=has_ctx),
        grid=(batch, n_pairs),
        in_specs=in_specs, out_specs=out_specs, out_shape=out_shape, scratch_shapes=scratch,
        compiler_params=pltpu.CompilerParams(
            dimension_semantics=("arbitrary", "arbitrary"), vmem_limit_bytes=VMEM_LIMIT),
        name="ssd_ctx" if has_ctx else "ssd",
    )(*args)
    return outs[0], (None if has_ctx else outs[1])


def _seg_rmsnorm(x, bd):
    x2 = x * x
    hi = x2.astype(BF16)
    lo = (x2 - hi.astype(F32)).astype(BF16)
    ms = _mm(hi, bd) + _mm(lo, bd)
    return x * lax.rsqrt(ms + EPS)


def _rope128(x, cos, sin_signed):
    lane = lax.broadcasted_iota(jnp.int32, x.shape, 1)
    first = (lane % 32) < 16
    swapped = jnp.where(first, pltpu.roll(x, LANES - 16, 1), pltpu.roll(x, 16, 1))
    return x * cos + swapped * sin_signed


def _attn_kernel(*refs, t_len, tq, past, has_ctx):
    if has_ctx:
        (q_ref, kv_ref, z_ref, qnw_ref, knw_ref, bd_ref, cosq_ref, sinq_ref, cosk_ref, sink_ref,
         ck_ref, cv_ref, o_ref, k_s, v_s) = refs
        kn_ref = None
    else:
        (q_ref, kv_ref, z_ref, qnw_ref, knw_ref, bd_ref, o_ref, kn_ref, k_s, v_s) = refs
    qi = pl.program_id(1)
    rt = _row_tile(t_len)
    n_tiles = t_len // rt

    @pl.when(qi == 0)
    def _():
        def prep(r, carry):
            start = pl.multiple_of(r * rt, rt)
            kv = kv_ref[pl.ds(start, rt), :]
            kn = _seg_rmsnorm(kv[:, :KV_WIDTH], bd_ref[0:KV_WIDTH, 0:KV_WIDTH]) * knw_ref[...]
            if kn_ref is not None:
                kn_ref[pl.ds(start, rt), :] = kn
            if has_ctx:
                kn = _rope128(kn, cosk_ref[pl.ds(start, rt), :], sink_ref[pl.ds(start, rt), :])
            k_s[pl.ds(start, rt), :] = kn.astype(BF16)
            v_s[pl.ds(start, rt), :] = kv[:, KV_WIDTH:].astype(BF16)
            return carry

        lax.fori_loop(0, n_tiles, prep, 0)
        if has_ctx:
            k_s[t_len:t_len + past, :] = ck_ref[...].astype(BF16)
            v_s[t_len:t_len + past, :] = cv_ref[...].astype(BF16)

    q = _seg_rmsnorm(q_ref[...].astype(F32), bd_ref[...]) * qnw_ref[...]
    blocks = []
    for cbk in range(ATTN_WIDTH // LANES):
        blk = q[:, cbk * LANES:(cbk + 1) * LANES]
        if has_ctx:
            blk = _rope128(blk, cosq_ref[...], sinq_ref[...])
        blocks.append((blk * (HEAD_DIM ** -0.5)).astype(BF16))
    lane = lax.broadcasted_iota(jnp.int32, (tq, LANES), 1)
    low = lane < HEAD_DIM
    group = N_HEADS // KV_HEADS
    kk = k_s[...]
    vv = v_s[...]
    head_out = [None] * N_HEADS
    for kvh in range(KV_HEADS):
        keep = low if kvh == 0 else jnp.logical_not(low)
        rows = []
        for g in range(group):
            head = kvh * group + g
            blk = blocks[head // 2]
            if head % 2 != kvh:
                blk = pltpu.roll(blk, HEAD_DIM, 1)
            rows.append(jnp.where(keep, blk, jnp.zeros_like(blk)))
        qg = jnp.concatenate(rows, axis=0)
        s = _nt(qg, kk)
        m = jnp.max(s, axis=-1, keepdims=True)
        pexp = jnp.exp(s - m)
        l = jnp.sum(pexp, axis=-1, keepdims=True)
        o = _mm(pexp.astype(BF16), vv) / l
        for g in range(group):
            head = kvh * group + g
            og = o[g * tq:(g + 1) * tq, :]
            if head % 2 != kvh:
                og = pltpu.roll(og, HEAD_DIM, 1)
            head_out[head] = og
    outs = [jnp.where(low, head_out[2 * i], head_out[2 * i + 1]) for i in range(N_HEADS // 2)]
    o_all = jnp.concatenate(outs, axis=1)
    o_ref[...] = (o_all * _silu(z_ref[...].astype(F32))).astype(o_ref.dtype)


def _attention(p, qnw, knw, bd, rope_tabs, cache_k, cache_v, *, batch, t_len):
    n = batch * t_len
    has_ctx = cache_k is not None
    past = cache_k.shape[1] if has_ctx else 0
    tq = 128 if has_ctx else min(t_len, 256)
    nq = t_len // tq
    in_specs = [pl.BlockSpec((tq, ATTN_WIDTH), lambda b, i: (b * nq + i, 0)),
                pl.BlockSpec((t_len, 2 * KV_WIDTH), lambda b, i: (b, 0)),
                pl.BlockSpec((tq, ATTN_WIDTH), lambda b, i: (b * nq + i, 0)),
                pl.BlockSpec((1, ATTN_WIDTH), lambda b, i: (0, 0)),
                pl.BlockSpec((1, KV_WIDTH), lambda b, i: (0, 0)),
                pl.BlockSpec((ATTN_WIDTH, ATTN_WIDTH), lambda b, i: (0, 0))]
    args = [p["bq"], p["kv"], p["bz"], qnw, knw, bd]
    out_shape = [jax.ShapeDtypeStruct((n, ATTN_WIDTH), BF16)]
    out_specs = [pl.BlockSpec((tq, ATTN_WIDTH), lambda b, i: (b * nq + i, 0))]
    if has_ctx:
        cos, sin = rope_tabs
        in_specs += [pl.BlockSpec((tq, LANES), lambda b, i: (i, 0)),
                     pl.BlockSpec((tq, LANES), lambda b, i: (i, 0)),
                     pl.BlockSpec((t_len, LANES), lambda b, i: (0, 0)),
                     pl.BlockSpec((t_len, LANES), lambda b, i: (0, 0)),
                     pl.BlockSpec((None, past, KV_WIDTH), lambda b, i: (b, 0, 0)),
                     pl.BlockSpec((None, past, KV_WIDTH), lambda b, i: (b, 0, 0))]
        args += [cos, sin, cos, sin, cache_k, cache_v]
    else:
        out_shape.append(jax.ShapeDtypeStruct((n, KV_WIDTH), F32))
        out_specs.append(pl.BlockSpec((t_len, KV_WIDTH), lambda b, i: (b, 0)))
    scratch = [pltpu.VMEM((t_len + past, KV_WIDTH), BF16), pltpu.VMEM((t_len + past, KV_WIDTH), BF16)]
    outs = pl.pallas_call(
        functools.partial(_attn_kernel, t_len=t_len, tq=tq, past=past, has_ctx=has_ctx),
        grid=(batch, nq),
        in_specs=in_specs, out_specs=out_specs, out_shape=out_shape, scratch_shapes=scratch,
        compiler_params=pltpu.CompilerParams(
            dimension_semantics=("arbitrary", "arbitrary"), vmem_limit_bytes=VMEM_LIMIT),
        name="attn_ctx" if has_ctx else "attn",
    )(*args)
    return outs[0], (None if has_ctx else outs[1])


def _out_kernel(x_ref, oa_ref, ob_ref, yc_ref, g_ref, mod_ref, snw_ref, wa_ref, wb_ref, wc_ref, wo_ref,
                fnw_ref, o_ref, *, final):
    ba = _mm(oa_ref[...], wa_ref[...])
    bb = _mm(ob_ref[...], wb_ref[...])
    yc = yc_ref[...].astype(F32)
    yn = yc * lax.rsqrt(jnp.mean(yc * yc, axis=-1, keepdims=True) + EPS) * snw_ref[...]
    bc = _mm(yn.astype(BF16), wc_ref[...])
    g = g_ref[...].astype(F32)
    merged = (jax.nn.sigmoid(g[:, 0:D_MODEL]) * ba + jax.nn.sigmoid(g[:, D_MODEL:2 * D_MODEL]) * bb
              + jax.nn.sigmoid(g[:, 2 * D_MODEL:]) * bc)
    out = _mm(merged.astype(BF16), wo_ref[...])
    xn = x_ref[...] + mod_ref[:, 2 * D_MODEL:] * out
    if final:
        xn = xn * lax.rsqrt(jnp.mean(xn * xn, axis=-1, keepdims=True) + EPS) * fnw_ref[...]
    o_ref[...] = xn


def _out_proj(x, o_a, o_b, y_c, gates, mod, ssm_norm_w, wa, wb, wc, wo, final_norm_w, *,
              t_len, mod_base, mod_per_batch, final):
    n = x.shape[0]
    tm = 512 if t_len % 512 == 0 else 256
    assert n % tm == 0 and t_len % tm == 0

    def mod_idx(i):
        return (mod_base + (i * tm) // t_len if mod_per_batch else mod_base, 0, 0)

    tok = lambda w: pl.BlockSpec((tm, w), lambda i: (i, 0))
    full = lambda a, b: pl.BlockSpec((a, b), lambda i: (0, 0))
    return pl.pallas_call(
        functools.partial(_out_kernel, final=final),
        grid=(n // tm,),
        in_specs=[tok(D_MODEL), tok(GDN_WIDTH), tok(ATTN_WIDTH), tok(D_INNER), tok(3 * D_MODEL),
                  pl.BlockSpec((None, 1, 3 * D_MODEL), mod_idx), full(1, D_INNER),
                  full(GDN_WIDTH, D_MODEL), full(ATTN_WIDTH, D_MODEL), full(D_INNER, D_MODEL),
                  full(D_MODEL, D_MODEL), full(1, D_MODEL)],
        out_specs=tok(D_MODEL),
        out_shape=jax.ShapeDtypeStruct((n, D_MODEL), F32),
        compiler_params=pltpu.CompilerParams(
            dimension_semantics=("arbitrary",), vmem_limit_bytes=VMEM_LIMIT),
        name="out_proj",
    )(x, o_a, o_b, y_c, gates, mod, ssm_norm_w, wa, wb, wc, wo, final_norm_w)


def _rope_tables(t_len):
    rot = HEAD_DIM // 2
    pos = jnp.arange(t_len)
    freqs = ROPE_BASE ** (-jnp.arange(rot // 2, dtype=F32) / (rot // 2))
    ang_row = (pos // GRID_W).astype(F32)[:, None] * freqs
    ang_col = (pos % GRID_W).astype(F32)[:, None] * freqs
    ang = jnp.concatenate([ang_row, ang_row, ang_col, ang_col], axis=1)
    sign = jnp.tile(jnp.concatenate([-jnp.ones((rot // 2,), F32), jnp.ones((rot // 2,), F32)]), 2)
    cos = jnp.tile(jnp.cos(ang), (1, 2))
    sin = jnp.tile(jnp.sin(ang) * sign, (1, 2))
    return cos, sin


def _layer_params(l, w_in, gdn_conv_w, gdn_a_log, gdn_dt_bias, gdn_norm_w, attn_q_norm, attn_k_norm,
                  ssm_conv_w, ssm_conv_b, ssm_a_log, ssm_dt_bias, ssm_d, ssm_norm_w,
                  w_branch_a, w_branch_b, w_branch_c, w_out):
    w = w_in[l]
    sl = lambda name: w[:, _SRC[name][0]:_SRC[name][0] + _SRC[name][1]]
    w_main = jnp.concatenate([sl(k) for k in _MAIN], axis=1).astype(BF16)
    w_small_t = jnp.concatenate([sl("dt"), sl("beta"), sl("alpha")], axis=1).T.astype(BF16)
    return dict(
        w_main=w_main, w_small_t=w_small_t,
        gdn_conv_w=gdn_conv_w[l], gdn_a_log=gdn_a_log[l], gdn_dt_bias=gdn_dt_bias[l],
        gdn_norm_w=gdn_norm_w[l].reshape(1, LANES),
        qnw=jnp.tile(attn_q_norm[l], N_HEADS).reshape(1, ATTN_WIDTH),
        knw=jnp.tile(attn_k_norm[l], KV_HEADS).reshape(1, KV_WIDTH),
        ssm_conv_w=ssm_conv_w[l], ssm_conv_b=ssm_conv_b[l].reshape(1, SSM_XBC),
        ssm_a_log=ssm_a_log[l], ssm_dt_bias=ssm_dt_bias[l],
        ssm_d=jnp.repeat(ssm_d[l], SSM_HEAD_DIM).reshape(1, D_INNER),
        ssm_norm_w=ssm_norm_w[l].reshape(1, D_INNER),
        wa=w_branch_a[l].astype(BF16), wb=w_branch_b[l].astype(BF16),
        wc=w_branch_c[l].astype(BF16), wo=w_out[l].astype(BF16))


def _stream_layer(x, lp, mod_l, norm_w_l, final_norm_w, bd, rope_tabs, ctx, *,
                  batch, t_len, mod_base, mod_per_batch, final):
    p = _in_proj(x, norm_w_l, mod_l, lp["w_main"], lp["w_small_t"],
                 t_len=t_len, mod_base=mod_base, mod_per_batch=mod_per_batch)
    ck, cv, sg, ss = ctx if ctx is not None else (None, None, None, None)
    o_a, new_sg = _gdn(p, lp["gdn_conv_w"], lp["gdn_norm_w"], lp["gdn_a_log"], lp["gdn_dt_bias"], sg,
                       batch=batch, t_len=t_len)
    y_c, new_ss = _ssd(p, lp["ssm_conv_w"], lp["ssm_conv_b"], lp["ssm_d"], lp["ssm_a_log"],
                       lp["ssm_dt_bias"], ss, batch=batch, t_len=t_len)
    o_b, k_norm = _attention(p, lp["qnw"], lp["knw"], bd, rope_tabs, ck, cv, batch=batch, t_len=t_len)
    x_new = _out_proj(x, o_a, o_b, y_c, p["gates"], mod_l, lp["ssm_norm_w"], lp["wa"], lp["wb"],
                      lp["wc"], lp["wo"], final_norm_w, t_len=t_len, mod_base=mod_base,
                      mod_per_batch=mod_per_batch, final=final)
    return x_new, (k_norm, p["kv"], new_sg, new_ss)


def kernel(x_prompt, x_sample, cache_k, cache_v, state_gdn, state_ssm, c, c_ctx, norm_w, w_mod, b_mod, w_in, gdn_conv_w, gdn_a_log, gdn_dt_bias, gdn_norm_w, attn_q_norm, attn_k_norm, ssm_conv_w, ssm_conv_b, ssm_a_log, ssm_dt_bias, ssm_d, ssm_norm_w, w_branch_a, w_branch_b, w_branch_c, w_out, final_norm_w):
    batch, seq, d_model = x_prompt.shape
    dec_batch, dec_seq, _ = x_sample.shape
    depth = w_in.shape[0]
    past = cache_k.shape[2]
    assert d_model == D_MODEL and seq % CHUNK == 0 and dec_seq % CHUNK == 0

    rows = -(-(1 + dec_batch) // 8) * 8
    cond = jnp.zeros((rows, D_MODEL), F32).at[0].set(c_ctx).at[1:1 + dec_batch].set(c)
    mod = _modulation(cond, w_mod, b_mod).reshape(depth, rows, 1, 3 * D_MODEL)

    seg = jnp.arange(ATTN_WIDTH) // HEAD_DIM
    bd = jnp.where(seg[:, None] == seg[None, :], 1.0 / HEAD_DIM, 0.0).astype(BF16)
    rope_tabs = _rope_tables(dec_seq)
    fnw = final_norm_w.reshape(1, D_MODEL)

    xp = x_prompt.reshape(batch * seq, D_MODEL)
    xs = x_sample.reshape(dec_batch * dec_seq, D_MODEL)
    new_k, new_v, new_gdn, new_ssm = [], [], [], []
    for l in range(depth):
        lp = _layer_params(l, w_in, gdn_conv_w, gdn_a_log, gdn_dt_bias, gdn_norm_w, attn_q_norm,
                           attn_k_norm, ssm_conv_w, ssm_conv_b, ssm_a_log, ssm_dt_bias, ssm_d,
                           ssm_norm_w, w_branch_a, w_branch_b, w_branch_c, w_out)
        nw = norm_w[l].reshape(1, D_MODEL)
        final = l == depth - 1
        xp, (k_norm, kv, s_g, s_s) = _stream_layer(
            xp, lp, mod[l], nw, fnw, bd, None, None,
            batch=batch, t_len=seq, mod_base=0, mod_per_batch=False, final=final)
        new_k.append(k_norm.reshape(batch, seq, KV_HEADS, HEAD_DIM))
        new_v.append(kv[:, KV_WIDTH:].reshape(batch, seq, KV_HEADS, HEAD_DIM))
        new_gdn.append(s_g)
        new_ssm.append(s_s)
        ctx = (cache_k[:, l].reshape(dec_batch, past, KV_WIDTH),
               cache_v[:, l].reshape(dec_batch, past, KV_WIDTH),
               state_gdn[:, l], state_ssm[:, l])
        xs, _ = _stream_layer(
            xs, lp, mod[l], nw, fnw, bd, rope_tabs, ctx,
            batch=dec_batch, t_len=dec_seq, mod_base=1, mod_per_batch=True, final=final)
    return (xp.reshape(batch, seq, D_MODEL), xs.reshape(dec_batch, dec_seq, D_MODEL),
            jnp.stack(new_k, axis=1), jnp.stack(new_v, axis=1),
            jnp.stack(new_gdn, axis=1), jnp.stack(new_ssm, axis=1))
```

```python
import functools
import math

import jax
import jax.numpy as jnp
from jax import lax
from jax.experimental import pallas as pl
from jax.experimental.pallas import tpu as pltpu

F32 = jnp.float32
BF16 = jnp.bfloat16

D_MODEL = 1024
EPS = 1e-6
GDN_HEADS = 4
GDN_DK = 128
GDN_WIDTH = 512
N_HEADS = 8
KV_HEADS = 2
HEAD_DIM = 64
ATTN_WIDTH = 512
KV_WIDTH = 128
GRID_W = 64
ROPE_BASE = 10000.0
SSM_HEADS = 16
SSM_HEAD_DIM = 64
D_INNER = 1024
SSM_STATE = 128
SSM_XBC = 1536
CHUNK = 128
LANES = 128
SMALL_ROWS = 48
VMEM_LIMIT = 56 * 1024 * 1024

_SRC = dict(qkv=(0, 1536), az=(1536, 512), beta=(2048, 8), alpha=(2056, 8), bq=(2064, 512),
            kv=(2576, 256), bz=(2832, 512), xbc=(3344, 1536), cz=(4880, 1024), dt=(5904, 32),
            gates=(5936, 3072))
_MAIN = ("qkv", "az", "bq", "kv", "bz", "xbc", "cz", "gates")
_MAIN_OFF = {}
_o = 0
for _n in _MAIN:
    _MAIN_OFF[_n] = _o
    _o += _SRC[_n][1]
MAIN_WIDTH = _o


def _nt(a, b):
    return lax.dot_general(a, b, (((1,), (1,)), ((), ())), preferred_element_type=F32)


def _tn(a, b):
    return lax.dot_general(a, b, (((0,), (0,)), ((), ())), preferred_element_type=F32)


def _mm(a, b):
    return jnp.dot(a, b, preferred_element_type=F32)


def _silu(x):
    return x * jax.nn.sigmoid(x)


def _softplus(x):
    return jnp.maximum(x, 0.0) + jnp.log1p(jnp.exp(-jnp.abs(x)))


def _scan(x, axis, reverse):
    n = x.shape[axis]
    idx = lax.broadcasted_iota(jnp.int32, x.shape, axis)
    k = 1
    while k < n:
        if reverse:
            x = x + jnp.where(idx < n - k, pltpu.roll(x, n - k, axis), 0.0)
        else:
            x = x + jnp.where(idx >= k, pltpu.roll(x, k, axis), 0.0)
        k *= 2
    return x


def _to_col(row, eye):
    return jnp.sum(jnp.where(eye, row, 0.0), axis=1, keepdims=True)


def _conv_tile(src_ref, r, rt, n_tiles, w, bias, off):
    t_len = src_ref.shape[0]
    cols = slice(off, off + LANES)
    start = pl.multiple_of(r * rt, rt)
    cur = src_ref[pl.ds(start, rt), cols].astype(F32)
    pstart = pl.multiple_of(jnp.maximum(start - 16, 0), 16)
    nstart = pl.multiple_of(jnp.minimum(start + rt, t_len - 16), 16)
    prev_row = src_ref[pl.ds(pstart, 16), cols][15:16, :].astype(F32)
    next_row = src_ref[pl.ds(nstart, 16), cols][0:1, :].astype(F32)
    prev_row = jnp.where(r > 0, prev_row, 0.0)
    next_row = jnp.where(r < n_tiles - 1, next_row, 0.0)
    ri = lax.broadcasted_iota(jnp.int32, (rt, LANES), 0)
    x_prev = jnp.where(ri == 0, prev_row, pltpu.roll(cur, 1, 0))
    x_next = jnp.where(ri == rt - 1, next_row, pltpu.roll(cur, rt - 1, 0))
    y = w[0:1, :] * x_prev + w[1:2, :] * cur + w[2:3, :] * x_next
    if bias is not None:
        y = y + bias
    return y


def _row_tile(t_len):
    return min(t_len, 256)


def _mod_kernel(c_ref, w_ref, b_ref, o_ref):
    s = _silu(c_ref[...])
    w = w_ref[...]
    s_hi = s.astype(BF16)
    s_lo = (s - s_hi.astype(F32)).astype(BF16)
    w_hi = w.astype(BF16)
    w_lo = (w - w_hi.astype(F32)).astype(BF16)
    o_ref[...] = _mm(s_hi, w_hi) + _mm(s_lo, w_hi) + _mm(s_hi, w_lo) + b_ref[...]


def _modulation(cond, w_mod, b_mod):
    depth = w_mod.shape[0]
    rows = cond.shape[0]
    tn = 1024
    return pl.pallas_call(
        _mod_kernel,
        grid=(depth, 3 * D_MODEL // tn),
        in_specs=[
            pl.BlockSpec((rows, D_MODEL), lambda l, j: (0, 0)),
            pl.BlockSpec((None, D_MODEL, tn), lambda l, j: (l, 0, j)),
            pl.BlockSpec((None, 1, tn), lambda l, j: (l, 0, j)),
        ],
        out_specs=pl.BlockSpec((None, rows, tn), lambda l, j: (l, 0, j)),
        out_shape=jax.ShapeDtypeStruct((depth, rows, 3 * D_MODEL), F32),
        name="modulation",
    )(cond, w_mod, b_mod.reshape(depth, 1, 3 * D_MODEL))


def _in_kernel(x_ref, nw_ref, mod_ref, w_ref, wst_ref,
               o_qkv, o_az, o_bq, o_kv, o_bz, o_xbc, o_cz, o_gates, o_small):
    x = x_ref[...]
    y = x * lax.rsqrt(jnp.mean(x * x, axis=-1, keepdims=True) + EPS) * nw_ref[...]
    shift = mod_ref[:, 0:D_MODEL]
    scale = mod_ref[:, D_MODEL:2 * D_MODEL]
    h = (y * (1.0 + scale) + shift).astype(BF16)
    outs = dict(qkv=o_qkv, az=o_az, bq=o_bq, kv=o_kv, bz=o_bz, xbc=o_xbc, cz=o_cz, gates=o_gates)
    for name in _MAIN:
        off, width = _MAIN_OFF[name], _SRC[name][1]
        outs[name][...] = _mm(h, w_ref[:, off:off + width]).astype(outs[name].dtype)
    o_small[...] = _nt(wst_ref[...], h)


def _in_proj(x, norm_w, mod, w_main, w_small_t, *, t_len, mod_base, mod_per_batch):
    n = x.shape[0]
    tm = 256
    assert n % tm == 0 and t_len % tm == 0

    def mod_idx(i):
        return (mod_base + (i * tm) // t_len if mod_per_batch else mod_base, 0, 0)

    widths = {k: _SRC[k][1] for k in _MAIN}
    dtypes = {k: (F32 if k == "kv" else BF16) for k in _MAIN}
    out_shape = [jax.ShapeDtypeStruct((n, widths[k]), dtypes[k]) for k in _MAIN]
    out_shape.append(jax.ShapeDtypeStruct((SMALL_ROWS, n), F32))
    out_specs = [pl.BlockSpec((tm, widths[k]), lambda i: (i, 0)) for k in _MAIN]
    out_specs.append(pl.BlockSpec((SMALL_ROWS, tm), lambda i: (0, i)))
    outs = pl.pallas_call(
        _in_kernel,
        grid=(n // tm,),
        in_specs=[
            pl.BlockSpec((tm, D_MODEL), lambda i: (i, 0)),
            pl.BlockSpec((1, D_MODEL), lambda i: (0, 0)),
            pl.BlockSpec((None, 1, 3 * D_MODEL), mod_idx),
            pl.BlockSpec((D_MODEL, MAIN_WIDTH), lambda i: (0, 0), pipeline_mode=pl.Buffered(1)),
            pl.BlockSpec((SMALL_ROWS, D_MODEL), lambda i: (0, 0)),
        ],
        out_specs=out_specs,
        out_shape=out_shape,
        compiler_params=pltpu.CompilerParams(
            dimension_semantics=("arbitrary",), vmem_limit_bytes=VMEM_LIMIT),
        name="in_proj",
    )(x, norm_w, mod, w_main, w_small_t)
    res = dict(zip(_MAIN, outs[:-1]))
    res["small_t"] = outs[-1]
    return res


INV_BASE = 8


def _tri_inverse(nms, sub, lane, eye_f):
    def same_block(s):
        sh = int(math.log2(s))
        return (sub >> sh) == (lane >> sh)

    base = same_block(INV_BASE)
    n8 = [jnp.where(base, nm, 0.0) for nm in nms]
    p = [n.astype(BF16) for n in n8]
    y = [eye_f - n for n in n8]
    p2 = [_mm(a, a).astype(BF16) for a in p]
    both = [_mm(a, jnp.concatenate([a, b.astype(BF16)], axis=1)) for a, b in zip(p2, y)]
    y = [b + c[:, CHUNK:] for b, c in zip(y, both)]
    p4 = [c[:, :CHUNK].astype(BF16) for c in both]
    y = [b + _mm(a, b.astype(BF16)) for a, b in zip(p4, y)]
    s = INV_BASE
    while s < CHUNK:
        sel = jnp.logical_and(same_block(2 * s), jnp.logical_not(same_block(s)))
        off = [jnp.where(sel, nm, 0.0).astype(BF16) for nm in nms]
        yb = [b.astype(BF16) for b in y]
        t = [_mm(b, o).astype(BF16) for b, o in zip(yb, off)]
        y = [b - _mm(a, c) for b, a, c in zip(y, t, yb)]
        s *= 2
    return y


def _gdn_kernel(*refs, t_len, has_ctx):
    if has_ctx:
        (q_ref, k_ref, v_ref, z_ref, gt_ref, wq_ref, wk_ref, wv_ref, nw_ref, alog_ref, bias_ref,
         s0_ref, o_ref, qs, ks, vs, oacc, st) = refs
        sout_ref = None
    else:
        (q_ref, k_ref, v_ref, z_ref, gt_ref, wq_ref, wk_ref, wv_ref, nw_ref, alog_ref, bias_ref,
         o_ref, sout_ref, qs, ks, vs, oacc, st) = refs
        s0_ref = None
    rt = _row_tile(t_len)
    n_tiles = t_len // rt
    n_chunks = t_len // CHUNK
    chains = [(d, h) for d in range(2) for h in range(GDN_HEADS)]

    def prep(r, carry):
        start = pl.multiple_of(r * rt, rt)
        for h in range(GDN_HEADS):
            off = h * LANES
            q = _silu(_conv_tile(q_ref, r, rt, n_tiles, wq_ref[:, off:off + LANES], None, off))
            k = _silu(_conv_tile(k_ref, r, rt, n_tiles, wk_ref[:, off:off + LANES], None, off))
            v = _silu(_conv_tile(v_ref, r, rt, n_tiles, wv_ref[:, off:off + LANES], None, off))
            q = q * lax.rsqrt(jnp.sum(q * q, axis=-1, keepdims=True) + EPS) * (GDN_DK ** -0.5)
            k = k * lax.rsqrt(jnp.sum(k * k, axis=-1, keepdims=True) + EPS)
            qs[pl.ds(start, rt), off:off + LANES] = q.astype(BF16)
            ks[pl.ds(start, rt), off:off + LANES] = k.astype(BF16)
            vs[pl.ds(start, rt), off:off + LANES] = v.astype(BF16)
        return carry

    lax.fori_loop(0, n_tiles, prep, 0)

    for ci, (d, h) in enumerate(chains):
        st[ci] = s0_ref[d, h] if has_ctx else jnp.zeros((GDN_DK, CHUNK), F32)

    sub = lax.broadcasted_iota(jnp.int32, (CHUNK, CHUNK), 0)
    lane = lax.broadcasted_iota(jnp.int32, (CHUNK, CHUNK), 1)
    eye = sub == lane
    eye_f = eye.astype(F32)
    incl = [lane <= sub, lane >= sub]
    strict = [lane < sub, lane > sub]
    neg_a = [-jnp.exp(jnp.full((1, CHUNK), alog_ref[d, h], F32)) for d, h in chains]

    def step(i, second):
        c0 = [pl.multiple_of(i * CHUNK, CHUNK), pl.multiple_of((n_chunks - 1 - i) * CHUNK, CHUNK)]
        gates = [gt_ref[:, pl.ds(c0[d], CHUNK)] for d in range(2)]
        cols = [slice(h * LANES, (h + 1) * LANES) for _, h in chains]
        qb = [qs[pl.ds(c0[d], CHUNK), cs] for (d, _), cs in zip(chains, cols)]
        kb = [ks[pl.ds(c0[d], CHUNK), cs] for (d, _), cs in zip(chains, cols)]
        vf = [vs[pl.ds(c0[d], CHUNK), cs].astype(F32) for (d, _), cs in zip(chains, cols)]
        qf = [a.astype(F32) for a in qb]
        kf = [a.astype(F32) for a in kb]
        beta = [jax.nn.sigmoid(gates[d][d * GDN_HEADS + h:d * GDN_HEADS + h + 1, :]) for d, h in chains]
        alpha = [gates[d][(2 + d) * GDN_HEADS + h:(2 + d) * GDN_HEADS + h + 1, :] for d, h in chains]
        g = [na * _softplus(al + bias_ref[d, h]) for na, al, (d, h) in zip(neg_a, alpha, chains)]
        gc = [_scan(a, 1, reverse=(d == 1)) for a, (d, _) in zip(g, chains)]
        beta_col = [_to_col(a, eye) for a in beta]
        gc_col = [_to_col(a, eye) for a in gc]
        kk = [_nt(a, a) for a in kb]
        qk = [_nt(a, b) for a, b in zip(qb, kb)]
        dec = [jnp.where(incl[d], jnp.exp(jnp.where(incl[d], c - r, 0.0)), 0.0)
               for c, r, (d, _) in zip(gc_col, gc, chains)]
        nm = [jnp.where(strict[d], a * b * c, 0.0) for a, b, c, (d, _) in zip(kk, dec, beta_col, chains)]
        t_inv = _tri_inverse(nm, sub, lane, eye_f)
        eg = [jnp.exp(a) for a in gc_col]
        rhs = [jnp.concatenate([a * b, c * (b * e)], axis=1).astype(BF16)
               for a, b, c, e in zip(vf, beta_col, kf, eg)]
        sol = [_mm(a.astype(BF16), b) for a, b in zip(t_inv, rhs)]
        tot = [a[:, CHUNK - 1:CHUNK] if d == 0 else a[:, 0:1] for a, (d, _) in zip(gc, chains)]
        q_dec = [(a * e).astype(BF16) for a, e in zip(qf, eg)]
        k_dec = [(a * jnp.exp(t - c)).astype(BF16) for a, t, c in zip(kf, tot, gc_col)]
        intra = [(a * b).astype(BF16) for a, b in zip(qk, dec)]
        s_old = [st[ci] for ci in range(len(chains))]
        sb = [a.astype(BF16) for a in s_old]
        v_new = [a[:, :CHUNK] - _mm(a[:, CHUNK:].astype(BF16), b) for a, b in zip(sol, sb)]
        vb = [a.astype(BF16) for a in v_new]
        o = [_mm(a, b) + _mm(c, e) for a, b, c, e in zip(q_dec, sb, intra, vb)]
        s_new = [a * jnp.exp(t) + _tn(b, c) for a, t, b, c in zip(s_old, tot, k_dec, vb)]
        for ci, ((d, _), cs) in enumerate(zip(chains, cols)):
            st[ci] = s_new[ci]
            rows = pl.ds(c0[d], CHUNK)
            if not second:
                oacc[rows, cs] = o[ci]
            else:
                ot = oacc[rows, cs] + o[ci]
                ot = ot * lax.rsqrt(jnp.mean(ot * ot, axis=-1, keepdims=True) + EPS) * nw_ref[...]
                o_ref[rows, cs] = (ot * _silu(z_ref[rows, cs].astype(F32))).astype(o_ref.dtype)

    half = n_chunks // 2
    lax.fori_loop(0, half, lambda i, c: (step(i, False), c)[1], 0)
    lax.fori_loop(half, n_chunks, lambda i, c: (step(i, True), c)[1], 0)

    if sout_ref is not None:
        for ci, (d, h) in enumerate(chains):
            sout_ref[d, h] = st[ci]


def _gdn(p, conv_w, norm_w, a_log, dt_bias, s0, *, batch, t_len):
    n = batch * t_len
    has_ctx = s0 is not None
    assert (t_len // CHUNK) % 2 == 0
    big = dict(pipeline_mode=pl.Buffered(1)) if t_len * GDN_WIDTH * 2 >= (2 << 20) else {}
    seq = lambda j: pl.BlockSpec((t_len, GDN_WIDTH), lambda b, j=j: (b, j), **big)
    cw = lambda j: pl.BlockSpec((3, GDN_WIDTH), lambda b, j=j: (0, j))
    smem = pl.BlockSpec(memory_space=pltpu.SMEM)
    state_spec = pl.BlockSpec((None, 2, GDN_HEADS, GDN_DK, CHUNK), lambda b: (b, 0, 0, 0, 0))
    in_specs = [seq(0), seq(1), seq(2), seq(0),
                pl.BlockSpec((16, t_len), lambda b: (2, b)),
                cw(0), cw(1), cw(2),
                pl.BlockSpec((1, LANES), lambda b: (0, 0)), smem, smem]
    args = [p["qkv"], p["qkv"], p["qkv"], p["az"], p["small_t"], conv_w, conv_w, conv_w,
            norm_w, a_log, dt_bias]
    out_shape = [jax.ShapeDtypeStruct((n, GDN_WIDTH), BF16)]
    out_specs = [pl.BlockSpec((t_len, GDN_WIDTH), lambda b: (b, 0))]
    if has_ctx:
        in_specs.append(state_spec)
        args.append(s0)
    else:
        out_shape.append(jax.ShapeDtypeStruct((batch, 2, GDN_HEADS, GDN_DK, CHUNK), F32))
        out_specs.append(state_spec)
    scratch = [pltpu.VMEM((t_len, GDN_WIDTH), BF16) for _ in range(3)]
    scratch += [pltpu.VMEM((t_len, GDN_WIDTH), F32), pltpu.VMEM((2 * GDN_HEADS, GDN_DK, CHUNK), F32)]
    outs = pl.pallas_call(
        functools.partial(_gdn_kernel, t_len=t_len, has_ctx=has_ctx),
        grid=(batch,),
        in_specs=in_specs, out_specs=out_specs, out_shape=out_shape, scratch_shapes=scratch,
        compiler_params=pltpu.CompilerParams(
            dimension_semantics=("arbitrary",), vmem_limit_bytes=VMEM_LIMIT),
        name="gdn_ctx" if has_ctx else "gdn",
    )(*args)
    return outs[0], (None if has_ctx else outs[1])


SSD_GROUP_HEADS = 8
SSD_PAIRS = SSD_GROUP_HEADS // 2
SSD_GROUP_WIDTH = SSD_GROUP_HEADS * SSM_HEAD_DIM


def _ssd_kernel(*refs, t_len, has_ctx):
    if has_ctx:
        (x_ref, b_ref, c_ref, z_ref, dtf_ref, dtb_ref, wx_ref, wb_ref, wc_ref, bx_ref, bb_ref, bc_ref,
         dvec_ref, alog_ref, bias_ref, s0_ref, o_ref, xs, bs, cs, yacc, st) = refs
        sout_ref = None
    else:
        (x_ref, b_ref, c_ref, z_ref, dtf_ref, dtb_ref, wx_ref, wb_ref, wc_ref, bx_ref, bb_ref, bc_ref,
         dvec_ref, alog_ref, bias_ref, o_ref, sout_ref, xs, bs, cs, yacc, st) = refs
        s0_ref = None
    grp = pl.program_id(1)
    rt = _row_tile(t_len)
    n_tiles = t_len // rt
    n_chunks = t_len // CHUNK
    chains = [(d, q) for d in range(2) for q in range(SSD_PAIRS)]
    heads = [(d, q, j) for d, q in chains for j in range(2)]
    dt_refs = (dtf_ref, dtb_ref)

    def prep(r, carry):
        start = pl.multiple_of(r * rt, rt)
        for q in range(SSD_PAIRS):
            off = q * LANES
            xs[pl.ds(start, rt), off:off + LANES] = _silu(_conv_tile(
                x_ref, r, rt, n_tiles, wx_ref[:, off:off + LANES], bx_ref[:, off:off + LANES], off)).astype(BF16)
        bs[pl.ds(start, rt), :] = _silu(_conv_tile(b_ref, r, rt, n_tiles, wb_ref[...], bb_ref[...], 0)).astype(BF16)
        cs[pl.ds(start, rt), :] = _silu(_conv_tile(c_ref, r, rt, n_tiles, wc_ref[...], bc_ref[...], 0)).astype(BF16)
        return carry

    lax.fori_loop(0, n_tiles, prep, 0)

    for ci, (d, q) in enumerate(chains):
        if has_ctx:
            st[ci] = jnp.concatenate([s0_ref[d, 2 * q], s0_ref[d, 2 * q + 1]], axis=0)
        else:
            st[ci] = jnp.zeros((2 * SSM_HEAD_DIM, SSM_STATE), F32)

    sub = lax.broadcasted_iota(jnp.int32, (CHUNK, CHUNK), 0)
    lane = lax.broadcasted_iota(jnp.int32, (CHUNK, CHUNK), 1)
    eye = sub == lane
    incl = [lane <= sub, lane >= sub]
    head0_lane = lane < SSM_HEAD_DIM
    head0_sub = sub < SSM_HEAD_DIM
    hidx = [grp * SSD_GROUP_HEADS + 2 * q + j for _, q, j in heads]
    neg_a = [-jnp.exp(jnp.full((1, CHUNK), alog_ref[d, hi], F32)) for (d, _, _), hi in zip(heads, hidx)]

    def step(i, second):
        c0 = [pl.multiple_of(i * CHUNK, CHUNK), pl.multiple_of((n_chunks - 1 - i) * CHUNK, CHUNK)]
        cols = [slice(q * LANES, (q + 1) * LANES) for _, q in chains]
        xb = [xs[pl.ds(c0[d], CHUNK), cs_] for (d, _), cs_ in zip(chains, cols)]
        xf = [a.astype(F32) for a in xb]
        bm = [bs[pl.ds(c0[d], CHUNK), :] for d in range(2)]
        cm = [cs[pl.ds(c0[d], CHUNK), :] for d in range(2)]
        cb = [_nt(cm[d], bm[d]) for d in range(2)]
        dtr = [dt_refs[d][:, pl.ds(c0[d], CHUNK)] for d in range(2)]
        raw = [dtr[d][2 * q + j:2 * q + j + 1, :] for d, q, j in heads]
        dt = [_softplus(a + bias_ref[d, hi]) for a, (d, _, _), hi in zip(raw, heads, hidx)]
        acum = [_scan(a * na, 1, reverse=(d == 1)) for a, na, (d, _, _) in zip(dt, neg_a, heads)]
        acum_col = [_to_col(a, eye) for a in acum]
        lmat = [jnp.where(incl[d], jnp.exp(jnp.where(incl[d], c - r, 0.0)), 0.0)
                for c, r, (d, _, _) in zip(acum_col, acum, heads)]
        m_mat = [(cb[d] * l * a).astype(BF16) for l, a, (d, _, _) in zip(lmat, dt, heads)]
        tot = [a[:, CHUNK - 1:CHUNK] if d == 0 else a[:, 0:1] for a, (d, _, _) in zip(acum, heads)]
        w_col = [_to_col(a * jnp.exp(t - c), eye) for a, t, c in zip(dt, tot, acum)]
        din_col = [jnp.exp(a) for a in acum_col]
        cd = [jnp.exp(t) for t in tot]
        x0 = [jnp.where(head0_lane, a, jnp.zeros_like(a)) for a in xb]
        x1 = [jnp.where(head0_lane, jnp.zeros_like(a), a) for a in xb]
        y = [_mm(m_mat[2 * ci], x0[ci]) + _mm(m_mat[2 * ci + 1], x1[ci]) for ci in range(len(chains))]
        s_old = [st[ci] for ci in range(len(chains))]
        y = [a + _nt(cm[d], s.astype(BF16)) * jnp.where(head0_lane, din_col[2 * ci], din_col[2 * ci + 1])
             for ci, (a, s, (d, _)) in enumerate(zip(y, s_old, chains))]
        xw = [(a * jnp.where(head0_lane, w_col[2 * ci], w_col[2 * ci + 1])).astype(BF16)
              for ci, a in enumerate(xf)]
        s_new = [s * jnp.where(head0_sub, cd[2 * ci], cd[2 * ci + 1]) + _tn(a, bm[d])
                 for ci, (s, a, (d, _)) in enumerate(zip(s_old, xw, chains))]
        for ci, ((d, _), cs_) in enumerate(zip(chains, cols)):
            st[ci] = s_new[ci]
            rows = pl.ds(c0[d], CHUNK)
            if not second:
                yacc[rows, cs_] = y[ci]
            else:
                yt = yacc[rows, cs_] + y[ci] + dvec_ref[:, cs_] * xf[ci]
                o_ref[rows, cs_] = (yt * _silu(z_ref[rows, cs_].astype(F32))).astype(o_ref.dtype)

    half = n_chunks // 2
    lax.fori_loop(0, half, lambda i, c: (step(i, False), c)[1], 0)
    lax.fori_loop(half, n_chunks, lambda i, c: (step(i, True), c)[1], 0)

    if sout_ref is not None:
        for ci, (d, q) in enumerate(chains):
            s = st[ci]
            sout_ref[d, 2 * q] = s[:SSM_HEAD_DIM]
            sout_ref[d, 2 * q + 1] = s[SSM_HEAD_DIM:]


def _ssd(p, conv_w, conv_b, d_vec, a_log, dt_bias, s0, *, batch, t_len):
    n = batch * t_len
    has_ctx = s0 is not None
    assert (t_len // CHUNK) % 2 == 0
    n_groups = SSM_HEADS // SSD_GROUP_HEADS
    x_blocks = D_INNER // LANES
    big = dict(pipeline_mode=pl.Buffered(1)) if t_len * SSD_GROUP_WIDTH * 2 >= (2 << 20) else {}
    smem = pl.BlockSpec(memory_space=pltpu.SMEM)
    state_spec = pl.BlockSpec((None, 2, SSD_GROUP_HEADS, SSM_HEAD_DIM, SSM_STATE), lambda b, g: (b, 0, g, 0, 0))
    in_specs = [pl.BlockSpec((t_len, SSD_GROUP_WIDTH), lambda b, g: (b, g), **big),
                pl.BlockSpec((t_len, LANES), lambda b, g: (b, x_blocks + g)),
                pl.BlockSpec((t_len, LANES), lambda b, g: (b, x_blocks + n_groups + g)),
                pl.BlockSpec((t_len, SSD_GROUP_WIDTH), lambda b, g: (b, g), **big),
                pl.BlockSpec((SSD_GROUP_HEADS, t_len), lambda b, g: (g, b)),
                pl.BlockSpec((SSD_GROUP_HEADS, t_len), lambda b, g: (n_groups + g, b)),
                pl.BlockSpec((3, SSD_GROUP_WIDTH), lambda b, g: (0, g)),
                pl.BlockSpec((3, LANES), lambda b, g: (0, x_blocks + g)),
                pl.BlockSpec((3, LANES), lambda b, g: (0, x_blocks + n_groups + g)),
                pl.BlockSpec((1, SSD_GROUP_WIDTH), lambda b, g: (0, g)),
                pl.BlockSpec((1, LANES), lambda b, g: (0, x_blocks + g)),
                pl.BlockSpec((1, LANES), lambda b, g: (0, x_blocks + n_groups + g)),
                pl.BlockSpec((1, SSD_GROUP_WIDTH), lambda b, g: (0, g)), smem, smem]
    args = [p["xbc"], p["xbc"], p["xbc"], p["cz"], p["small_t"], p["small_t"], conv_w, conv_w, conv_w,
            conv_b, conv_b, conv_b, d_vec, a_log, dt_bias]
    out_shape = [jax.ShapeDtypeStruct((n, D_INNER), BF16)]
    out_specs = [pl.BlockSpec((t_len, SSD_GROUP_WIDTH), lambda b, g: (b, g))]
    if has_ctx:
        in_specs.append(state_spec)
        args.append(s0)
    else:
        out_shape.append(jax.ShapeDtypeStruct((batch, 2, SSM_HEADS, SSM_HEAD_DIM, SSM_STATE), F32))
        out_specs.append(state_spec)
    scratch = [pltpu.VMEM((t_len, SSD_GROUP_WIDTH), BF16), pltpu.VMEM((t_len, LANES), BF16),
               pltpu.VMEM((t_len, LANES), BF16), pltpu.VMEM((t_len, SSD_GROUP_WIDTH), F32),
               pltpu.VMEM((2 * SSD_PAIRS, 2 * SSM_HEAD_DIM, SSM_STATE), F32)]
    outs = pl.pallas_call(
        functools.partial(_ssd_kernel, t_len=t_len, has_ctx=has_ctx),
        grid=(batch, n_groups),
        in_specs=in_specs, out_specs=out_specs, out_shape=out_shape, scratch_shapes=scratch,
        compiler_params=pltpu.CompilerParams(
            dimension_semantics=("arbitrary", "arbitrary"), vmem_limit_bytes=VMEM_LIMIT),
        name="ssd_ctx" if has_ctx else "ssd",
    )(*args)
    return outs[0], (None if has_ctx else outs[1])


def _seg_rmsnorm(x, bd):
    x2 = x * x
    hi = x2.astype(BF16)
    lo = (x2 - hi.astype(F32)).astype(BF16)
    ms = _mm(hi, bd) + _mm(lo, bd)
    return x * lax.rsqrt(ms + EPS)


def _rope128(x, cos, sin_signed):
    lane = lax.broadcasted_iota(jnp.int32, x.shape, 1)
    first = (lane % 32) < 16
    swapped = jnp.where(first, pltpu.roll(x, LANES - 16, 1), pltpu.roll(x, 16, 1))
    return x * cos + swapped * sin_signed


def _attn_kernel(*refs, t_len, tq, past, has_ctx):
    if has_ctx:
        (q_ref, kv_ref, z_ref, qnw_ref, knw_ref, bd_ref, cosq_ref, sinq_ref, cosk_ref, sink_ref,
         ck_ref, cv_ref, o_ref, k_s, v_s) = refs
        kn_ref = None
    else:
        (q_ref, kv_ref, z_ref, qnw_ref, knw_ref, bd_ref, o_ref, kn_ref, k_s, v_s) = refs
    qi = pl.program_id(1)
    rt = _row_tile(t_len)
    n_tiles = t_len // rt

    @pl.when(qi == 0)
    def _():
        def prep(r, carry):
            start = pl.multiple_of(r * rt, rt)
            kv = kv_ref[pl.ds(start, rt), :]
            kn = _seg_rmsnorm(kv[:, :KV_WIDTH], bd_ref[0:KV_WIDTH, 0:KV_WIDTH]) * knw_ref[...]
            if kn_ref is not None:
                kn_ref[pl.ds(start, rt), :] = kn
            if has_ctx:
                kn = _rope128(kn, cosk_ref[pl.ds(start, rt), :], sink_ref[pl.ds(start, rt), :])
            k_s[pl.ds(start, rt), :] = kn.astype(BF16)
            v_s[pl.ds(start, rt), :] = kv[:, KV_WIDTH:].astype(BF16)
            return carry

        lax.fori_loop(0, n_tiles, prep, 0)
        if has_ctx:
            k_s[t_len:t_len + past, :] = ck_ref[...].astype(BF16)
            v_s[t_len:t_len + past, :] = cv_ref[...].astype(BF16)

    q = _seg_rmsnorm(q_ref[...].astype(F32), bd_ref[...]) * qnw_ref[...]
    blocks = []
    for cbk in range(ATTN_WIDTH // LANES):
        blk = q[:, cbk * LANES:(cbk + 1) * LANES]
        if has_ctx:
            blk = _rope128(blk, cosq_ref[...], sinq_ref[...])
        blocks.append((blk * (HEAD_DIM ** -0.5)).astype(BF16))
    lane = lax.broadcasted_iota(jnp.int32, (tq, LANES), 1)
    low = lane < HEAD_DIM
    group = N_HEADS // KV_HEADS
    kk = k_s[...]
    vv = v_s[...]
    head_out = [None] * N_HEADS
    for kvh in range(KV_HEADS):
        keep = low if kvh == 0 else jnp.logical_not(low)
        rows = []
        for g in range(group):
            head = kvh * group + g
            blk = blocks[head // 2]
            if head % 2 != kvh:
                blk = pltpu.roll(blk, HEAD_DIM, 1)
            rows.append(jnp.where(keep, blk, jnp.zeros_like(blk)))
        qg = jnp.concatenate(rows, axis=0)
        s = _nt(qg, kk)
        m = jnp.max(s, axis=-1, keepdims=True)
        pexp = jnp.exp(s - m)
        l = jnp.sum(pexp, axis=-1, keepdims=True)
        o = _mm(pexp.astype(BF16), vv) / l
        for g in range(group):
            head = kvh * group + g
            og = o[g * tq:(g + 1) * tq, :]
            if head % 2 != kvh:
                og = pltpu.roll(og, HEAD_DIM, 1)
            head_out[head] = og
    outs = [jnp.where(low, head_out[2 * i], head_out[2 * i + 1]) for i in range(N_HEADS // 2)]
    o_all = jnp.concatenate(outs, axis=1)
    o_ref[...] = (o_all * _silu(z_ref[...].astype(F32))).astype(o_ref.dtype)


def _attention(p, qnw, knw, bd, rope_tabs, cache_k, cache_v, *, batch, t_len):
    n = batch * t_len
    has_ctx = cache_k is not None
    past = cache_k.shape[1] if has_ctx else 0
    tq = 128 if has_ctx else min(t_len, 256)
    nq = t_len // tq
    in_specs = [pl.BlockSpec((tq, ATTN_WIDTH), lambda b, i: (b * nq + i, 0)),
                pl.BlockSpec((t_len, 2 * KV_WIDTH), lambda b, i: (b, 0)),
                pl.BlockSpec((tq, ATTN_WIDTH), lambda b, i: (b * nq + i, 0)),
                pl.BlockSpec((1, ATTN_WIDTH), lambda b, i: (0, 0)),
                pl.BlockSpec((1, KV_WIDTH), lambda b, i: (0, 0)),
                pl.BlockSpec((ATTN_WIDTH, ATTN_WIDTH), lambda b, i: (0, 0))]
    args = [p["bq"], p["kv"], p["bz"], qnw, knw, bd]
    out_shape = [jax.ShapeDtypeStruct((n, ATTN_WIDTH), BF16)]
    out_specs = [pl.BlockSpec((tq, ATTN_WIDTH), lambda b, i: (b * nq + i, 0))]
    if has_ctx:
        cos, sin = rope_tabs
        in_specs += [pl.BlockSpec((tq, LANES), lambda b, i: (i, 0)),
                     pl.BlockSpec((tq, LANES), lambda b, i: (i, 0)),
                     pl.BlockSpec((t_len, LANES), lambda b, i: (0, 0)),
                     pl.BlockSpec((t_len, LANES), lambda b, i: (0, 0)),
                     pl.BlockSpec((None, past, KV_WIDTH), lambda b, i: (b, 0, 0)),
                     pl.BlockSpec((None, past, KV_WIDTH), lambda b, i: (b, 0, 0))]
        args += [cos, sin, cos, sin, cache_k, cache_v]
    else:
        out_shape.append(jax.ShapeDtypeStruct((n, KV_WIDTH), F32))
        out_specs.append(pl.BlockSpec((t_len, KV_WIDTH), lambda b, i: (b, 0)))
    scratch = [pltpu.VMEM((t_len + past, KV_WIDTH), BF16), pltpu.VMEM((t_len + past, KV_WIDTH), BF16)]
    outs = pl.pallas_call(
        functools.partial(_attn_kernel, t_len=t_len, tq=tq, past=past, has_ctx=has_ctx),
        grid=(batch, nq),
        in_specs=in_specs, out_specs=out_specs, out_shape=out_shape, scratch_shapes=scratch,
        compiler_params=pltpu.CompilerParams(
            dimension_semantics=("arbitrary", "arbitrary"), vmem_limit_bytes=VMEM_LIMIT),
        name="attn_ctx" if has_ctx else "attn",
    )(*args)
    return outs[0], (None if has_ctx else outs[1])


def _out_kernel(x_ref, oa_ref, ob_ref, yc_ref, g_ref, mod_ref, snw_ref, wa_ref, wb_ref, wc_ref, wo_ref,
                fnw_ref, o_ref, *, final):
    ba = _mm(oa_ref[...], wa_ref[...])
    bb = _mm(ob_ref[...], wb_ref[...])
    yc = yc_ref[...].astype(F32)
    yn = yc * lax.rsqrt(jnp.mean(yc * yc, axis=-1, keepdims=True) + EPS) * snw_ref[...]
    bc = _mm(yn.astype(BF16), wc_ref[...])
    g = g_ref[...].astype(F32)
    merged = (jax.nn.sigmoid(g[:, 0:D_MODEL]) * ba + jax.nn.sigmoid(g[:, D_MODEL:2 * D_MODEL]) * bb
              + jax.nn.sigmoid(g[:, 2 * D_MODEL:]) * bc)
    out = _mm(merged.astype(BF16), wo_ref[...])
    xn = x_ref[...] + mod_ref[:, 2 * D_MODEL:] * out
    if final:
        xn = xn * lax.rsqrt(jnp.mean(xn * xn, axis=-1, keepdims=True) + EPS) * fnw_ref[...]
    o_ref[...] = xn


def _out_proj(x, o_a, o_b, y_c, gates, mod, ssm_norm_w, wa, wb, wc, wo, final_norm_w, *,
              t_len, mod_base, mod_per_batch, final):
    n = x.shape[0]
    tm = 512 if t_len % 512 == 0 else 256
    assert n % tm == 0 and t_len % tm == 0

    def mod_idx(i):
        return (mod_base + (i * tm) // t_len if mod_per_batch else mod_base, 0, 0)

    tok = lambda w: pl.BlockSpec((tm, w), lambda i: (i, 0))
    full = lambda a, b: pl.BlockSpec((a, b), lambda i: (0, 0))
    return pl.pallas_call(
        functools.partial(_out_kernel, final=final),
        grid=(n // tm,),
        in_specs=[tok(D_MODEL), tok(GDN_WIDTH), tok(ATTN_WIDTH), tok(D_INNER), tok(3 * D_MODEL),
                  pl.BlockSpec((None, 1, 3 * D_MODEL), mod_idx), full(1, D_INNER),
                  full(GDN_WIDTH, D_MODEL), full(ATTN_WIDTH, D_MODEL), full(D_INNER, D_MODEL),
                  full(D_MODEL, D_MODEL), full(1, D_MODEL)],
        out_specs=tok(D_MODEL),
        out_shape=jax.ShapeDtypeStruct((n, D_MODEL), F32),
        compiler_params=pltpu.CompilerParams(
            dimension_semantics=("arbitrary",), vmem_limit_bytes=VMEM_LIMIT),
        name="out_proj",
    )(x, o_a, o_b, y_c, gates, mod, ssm_norm_w, wa, wb, wc, wo, final_norm_w)


def _rope_tables(t_len):
    rot = HEAD_DIM // 2
    pos = jnp.arange(t_len)
    freqs = ROPE_BASE ** (-jnp.arange(rot // 2, dtype=F32) / (rot // 2))
    ang_row = (pos // GRID_W).astype(F32)[:, None] * freqs
    ang_col = (pos % GRID_W).astype(F32)[:, None] * freqs
    ang = jnp.concatenate([ang_row, ang_row, ang_col, ang_col], axis=1)
    sign = jnp.tile(jnp.concatenate([-jnp.ones((rot // 2,), F32), jnp.ones((rot // 2,), F32)]), 2)
    cos = jnp.tile(jnp.cos(ang), (1, 2))
    sin = jnp.tile(jnp.sin(ang) * sign, (1, 2))
    return cos, sin


def _layer_params(l, w_in, gdn_conv_w, gdn_a_log, gdn_dt_bias, gdn_norm_w, attn_q_norm, attn_k_norm,
                  ssm_conv_w, ssm_conv_b, ssm_a_log, ssm_dt_bias, ssm_d, ssm_norm_w,
                  w_branch_a, w_branch_b, w_branch_c, w_out):
    w = w_in[l]
    sl = lambda name: w[:, _SRC[name][0]:_SRC[name][0] + _SRC[name][1]]
    w_main = jnp.concatenate([sl(k) for k in _MAIN], axis=1).astype(BF16)
    w_small_t = jnp.concatenate([sl("dt"), sl("beta"), sl("alpha")], axis=1).T.astype(BF16)
    return dict(
        w_main=w_main, w_small_t=w_small_t,
        gdn_conv_w=gdn_conv_w[l], gdn_a_log=gdn_a_log[l], gdn_dt_bias=gdn_dt_bias[l],
        gdn_norm_w=gdn_norm_w[l].reshape(1, LANES),
        qnw=jnp.tile(attn_q_norm[l], N_HEADS).reshape(1, ATTN_WIDTH),
        knw=jnp.tile(attn_k_norm[l], KV_HEADS).reshape(1, KV_WIDTH),
        ssm_conv_w=ssm_conv_w[l], ssm_conv_b=ssm_conv_b[l].reshape(1, SSM_XBC),
        ssm_a_log=ssm_a_log[l], ssm_dt_bias=ssm_dt_bias[l],
        ssm_d=jnp.repeat(ssm_d[l], SSM_HEAD_DIM).reshape(1, D_INNER),
        ssm_norm_w=ssm_norm_w[l].reshape(1, D_INNER),
        wa=w_branch_a[l].astype(BF16), wb=w_branch_b[l].astype(BF16),
        wc=w_branch_c[l].astype(BF16), wo=w_out[l].astype(BF16))


def _stream_layer(x, lp, mod_l, norm_w_l, final_norm_w, bd, rope_tabs, ctx, *,
                  batch, t_len, mod_base, mod_per_batch, final):
    p = _in_proj(x, norm_w_l, mod_l, lp["w_main"], lp["w_small_t"],
                 t_len=t_len, mod_base=mod_base, mod_per_batch=mod_per_batch)
    ck, cv, sg, ss = ctx if ctx is not None else (None, None, None, None)
    o_a, new_sg = _gdn(p, lp["gdn_conv_w"], lp["gdn_norm_w"], lp["gdn_a_log"], lp["gdn_dt_bias"], sg,
                       batch=batch, t_len=t_len)
    y_c, new_ss = _ssd(p, lp["ssm_conv_w"], lp["ssm_conv_b"], lp["ssm_d"], lp["ssm_a_log"],
                       lp["ssm_dt_bias"], ss, batch=batch, t_len=t_len)
    o_b, k_norm = _attention(p, lp["qnw"], lp["knw"], bd, rope_tabs, ck, cv, batch=batch, t_len=t_len)
    x_new = _out_proj(x, o_a, o_b, y_c, p["gates"], mod_l, lp["ssm_norm_w"], lp["wa"], lp["wb"],
                      lp["wc"], lp["wo"], final_norm_w, t_len=t_len, mod_base=mod_base,
                      mod_per_batch=mod_per_batch, final=final)
    return x_new, (k_norm, p["kv"], new_sg, new_ss)


def kernel(x_prompt, x_sample, cache_k, cache_v, state_gdn, state_ssm, c, c_ctx, norm_w, w_mod, b_mod, w_in, gdn_conv_w, gdn_a_log, gdn_dt_bias, gdn_norm_w, attn_q_norm, attn_k_norm, ssm_conv_w, ssm_conv_b, ssm_a_log, ssm_dt_bias, ssm_d, ssm_norm_w, w_branch_a, w_branch_b, w_branch_c, w_out, final_norm_w):
    batch, seq, d_model = x_prompt.shape
    dec_batch, dec_seq, _ = x_sample.shape
    depth = w_in.shape[0]
    past = cache_k.shape[2]
    assert d_model == D_MODEL and seq % CHUNK == 0 and dec_seq % CHUNK == 0

    rows = -(-(1 + dec_batch) // 8) * 8
    cond = jnp.zeros((rows, D_MODEL), F32).at[0].set(c_ctx).at[1:1 + dec_batch].set(c)
    mod = _modulation(cond, w_mod, b_mod).reshape(depth, rows, 1, 3 * D_MODEL)

    seg = jnp.arange(ATTN_WIDTH) // HEAD_DIM
    bd = jnp.where(seg[:, None] == seg[None, :], 1.0 / HEAD_DIM, 0.0).astype(BF16)
    rope_tabs = _rope_tables(dec_seq)
    fnw = final_norm_w.reshape(1, D_MODEL)

    xp = x_prompt.reshape(batch * seq, D_MODEL)
    xs = x_sample.reshape(dec_batch * dec_seq, D_MODEL)
    new_k, new_v, new_gdn, new_ssm = [], [], [], []
    for l in range(depth):
        lp = _layer_params(l, w_in, gdn_conv_w, gdn_a_log, gdn_dt_bias, gdn_norm_w, attn_q_norm,
                           attn_k_norm, ssm_conv_w, ssm_conv_b, ssm_a_log, ssm_dt_bias, ssm_d,
                           ssm_norm_w, w_branch_a, w_branch_b, w_branch_c, w_out)
        nw = norm_w[l].reshape(1, D_MODEL)
        final = l == depth - 1
        xp, (k_norm, kv, s_g, s_s) = _stream_layer(
            xp, lp, mod[l], nw, fnw, bd, None, None,
            batch=batch, t_len=seq, mod_base=0, mod_per_batch=False, final=final)
        new_k.append(k_norm.reshape(batch, seq, KV_HEADS, HEAD_DIM))
        new_v.append(kv[:, KV_WIDTH:].reshape(batch, seq, KV_HEADS, HEAD_DIM))
        new_gdn.append(s_g)
        new_ssm.append(s_s)
        ctx = (cache_k[:, l].reshape(dec_batch, past, KV_WIDTH),
               cache_v[:, l].reshape(dec_batch, past, KV_WIDTH),
               state_gdn[:, l], state_ssm[:, l])
        xs, _ = _stream_layer(
            xs, lp, mod[l], nw, fnw, bd, rope_tabs, ctx,
            batch=dec_batch, t_len=dec_seq, mod_base=1, mod_per_batch=True, final=final)
    return (xp.reshape(batch, seq, D_MODEL), xs.reshape(dec_batch, dec_seq, D_MODEL),
            jnp.stack(new_k, axis=1), jnp.stack(new_v, axis=1),
            jnp.stack(new_gdn, axis=1), jnp.stack(new_ssm, axis=1))
```

```python
import functools
import math

import jax
import jax.numpy as jnp
from jax import lax
from jax.experimental import pallas as pl
from jax.experimental.pallas import tpu as pltpu

F32 = jnp.float32
BF16 = jnp.bfloat16

D_MODEL = 1024
EPS = 1e-6
GDN_HEADS = 4
GDN_DK = 128
GDN_WIDTH = 512
N_HEADS = 8
KV_HEADS = 2
HEAD_DIM = 64
ATTN_WIDTH = 512
KV_WIDTH = 128
GRID_W = 64
ROPE_BASE = 10000.0
SSM_HEADS = 16
SSM_HEAD_DIM = 64
D_INNER = 1024
SSM_STATE = 128
SSM_XBC = 1536
CHUNK = 128
LANES = 128
SMALL_ROWS = 48
VMEM_LIMIT = 56 * 1024 * 1024

_SRC = dict(qkv=(0, 1536), az=(1536, 512), beta=(2048, 8), alpha=(2056, 8), bq=(2064, 512),
            kv=(2576, 256), bz=(2832, 512), xbc=(3344, 1536), cz=(4880, 1024), dt=(5904, 32),
            gates=(5936, 3072))
_MAIN = ("qkv", "az", "bq", "kv", "bz", "xbc", "cz", "gates")
_MAIN_OFF = {}
_o = 0
for _n in _MAIN:
    _MAIN_OFF[_n] = _o
    _o += _SRC[_n][1]
MAIN_WIDTH = _o


def _nt(a, b):
    return lax.dot_general(a, b, (((1,), (1,)), ((), ())), preferred_element_type=F32)


def _tn(a, b):
    return lax.dot_general(a, b, (((0,), (0,)), ((), ())), preferred_element_type=F32)


def _mm(a, b):
    return jnp.dot(a, b, preferred_element_type=F32)


def _silu(x):
    return x * jax.nn.sigmoid(x)


def _softplus(x):
    return jnp.maximum(x, 0.0) + jnp.log1p(jnp.exp(-jnp.abs(x)))


def _split3(x):
    hi = x.astype(BF16)
    r1 = x - hi.astype(F32)
    mid = r1.astype(BF16)
    lo = (r1 - mid.astype(F32)).astype(BF16)
    return hi, mid, lo


def _cumsum_rows(x, tri):
    hi, mid, lo = _split3(x)
    return _mm(hi, tri) + _mm(mid, tri) + _mm(lo, tri)


def _rows_to_cols(x, eye_b):
    hi, mid, lo = _split3(x)
    return _nt(eye_b, hi) + _nt(eye_b, mid) + _nt(eye_b, lo)


def _conv_tile(src_ref, r, rt, n_tiles, w, bias, off):
    t_len = src_ref.shape[0]
    cols = slice(off, off + LANES)
    start = pl.multiple_of(r * rt, rt)
    cur = src_ref[pl.ds(start, rt), cols].astype(F32)
    pstart = pl.multiple_of(jnp.maximum(start - 16, 0), 16)
    nstart = pl.multiple_of(jnp.minimum(start + rt, t_len - 16), 16)
    prev_row = src_ref[pl.ds(pstart, 16), cols][15:16, :].astype(F32)
    next_row = src_ref[pl.ds(nstart, 16), cols][0:1, :].astype(F32)
    prev_row = jnp.where(r > 0, prev_row, 0.0)
    next_row = jnp.where(r < n_tiles - 1, next_row, 0.0)
    ri = lax.broadcasted_iota(jnp.int32, (rt, LANES), 0)
    x_prev = jnp.where(ri == 0, prev_row, pltpu.roll(cur, 1, 0))
    x_next = jnp.where(ri == rt - 1, next_row, pltpu.roll(cur, rt - 1, 0))
    y = w[0:1, :] * x_prev + w[1:2, :] * cur + w[2:3, :] * x_next
    if bias is not None:
        y = y + bias
    return y


def _row_tile(t_len):
    return min(t_len, 256)


def _mod_kernel(c_ref, w_ref, b_ref, o_ref):
    s = _silu(c_ref[...])
    w = w_ref[...]
    s_hi = s.astype(BF16)
    s_lo = (s - s_hi.astype(F32)).astype(BF16)
    w_hi = w.astype(BF16)
    w_lo = (w - w_hi.astype(F32)).astype(BF16)
    o_ref[...] = _mm(s_hi, w_hi) + _mm(s_lo, w_hi) + _mm(s_hi, w_lo) + b_ref[...]


def _modulation(cond, w_mod, b_mod):
    depth = w_mod.shape[0]
    rows = cond.shape[0]
    tn = 1024
    return pl.pallas_call(
        _mod_kernel,
        grid=(depth, 3 * D_MODEL // tn),
        in_specs=[
            pl.BlockSpec((rows, D_MODEL), lambda l, j: (0, 0)),
            pl.BlockSpec((None, D_MODEL, tn), lambda l, j: (l, 0, j)),
            pl.BlockSpec((None, 1, tn), lambda l, j: (l, 0, j)),
        ],
        out_specs=pl.BlockSpec((None, rows, tn), lambda l, j: (l, 0, j)),
        out_shape=jax.ShapeDtypeStruct((depth, rows, 3 * D_MODEL), F32),
        name="modulation",
    )(cond, w_mod, b_mod.reshape(depth, 1, 3 * D_MODEL))


def _in_kernel(x_ref, nw_ref, mod_ref, w_ref, wst_ref,
               o_qkv, o_az, o_bq, o_kv, o_bz, o_xbc, o_cz, o_gates, o_small):
    x = x_ref[...]
    y = x * lax.rsqrt(jnp.mean(x * x, axis=-1, keepdims=True) + EPS) * nw_ref[...]
    shift = mod_ref[:, 0:D_MODEL]
    scale = mod_ref[:, D_MODEL:2 * D_MODEL]
    h = (y * (1.0 + scale) + shift).astype(BF16)
    outs = dict(qkv=o_qkv, az=o_az, bq=o_bq, kv=o_kv, bz=o_bz, xbc=o_xbc, cz=o_cz, gates=o_gates)
    for name in _MAIN:
        off, width = _MAIN_OFF[name], _SRC[name][1]
        outs[name][...] = _mm(h, w_ref[:, off:off + width]).astype(outs[name].dtype)
    o_small[...] = _nt(wst_ref[...], h)


def _in_proj(x, norm_w, mod, w_main, w_small_t, *, t_len, mod_base, mod_per_batch):
    n = x.shape[0]
    tm = 256
    assert n % tm == 0 and t_len % tm == 0

    def mod_idx(i):
        return (mod_base + (i * tm) // t_len if mod_per_batch else mod_base, 0, 0)

    widths = {k: _SRC[k][1] for k in _MAIN}
    dtypes = {k: (F32 if k == "kv" else BF16) for k in _MAIN}
    out_shape = [jax.ShapeDtypeStruct((n, widths[k]), dtypes[k]) for k in _MAIN]
    out_shape.append(jax.ShapeDtypeStruct((SMALL_ROWS, n), F32))
    out_specs = [pl.BlockSpec((tm, widths[k]), lambda i: (i, 0)) for k in _MAIN]
    out_specs.append(pl.BlockSpec((SMALL_ROWS, tm), lambda i: (0, i)))
    outs = pl.pallas_call(
        _in_kernel,
        grid=(n // tm,),
        in_specs=[
            pl.BlockSpec((tm, D_MODEL), lambda i: (i, 0)),
            pl.BlockSpec((1, D_MODEL), lambda i: (0, 0)),
            pl.BlockSpec((None, 1, 3 * D_MODEL), mod_idx),
            pl.BlockSpec((D_MODEL, MAIN_WIDTH), lambda i: (0, 0), pipeline_mode=pl.Buffered(1)),
            pl.BlockSpec((SMALL_ROWS, D_MODEL), lambda i: (0, 0)),
        ],
        out_specs=out_specs,
        out_shape=out_shape,
        compiler_params=pltpu.CompilerParams(
            dimension_semantics=("arbitrary",), vmem_limit_bytes=VMEM_LIMIT),
        name="in_proj",
    )(x, norm_w, mod, w_main, w_small_t)
    res = dict(zip(_MAIN, outs[:-1]))
    res["small_t"] = outs[-1]
    return res


INV_BASE = 8


def _tri_inverse(nms, sub, lane, eye_f):
    def same_block(s):
        sh = int(math.log2(s))
        return (sub >> sh) == (lane >> sh)

    base = same_block(INV_BASE)
    n8 = [jnp.where(base, nm, 0.0) for nm in nms]
    p = [n.astype(BF16) for n in n8]
    y = [eye_f - n for n in n8]
    p2 = [_mm(a, a).astype(BF16) for a in p]
    both = [_mm(a, jnp.concatenate([a, b.astype(BF16)], axis=1)) for a, b in zip(p2, y)]
    y = [b + c[:, CHUNK:] for b, c in zip(y, both)]
    p4 = [c[:, :CHUNK].astype(BF16) for c in both]
    y = [b + _mm(a, b.astype(BF16)) for a, b in zip(p4, y)]
    s = INV_BASE
    while s < CHUNK:
        sel = jnp.logical_and(same_block(2 * s), jnp.logical_not(same_block(s)))
        off = [jnp.where(sel, nm, 0.0).astype(BF16) for nm in nms]
        yb = [b.astype(BF16) for b in y]
        t = [_mm(b, o).astype(BF16) for b, o in zip(yb, off)]
        y = [b - _mm(a, c) for b, a, c in zip(y, t, yb)]
        s *= 2
    return y


def _gdn_kernel(*refs, t_len, has_ctx):
    if has_ctx:
        (q_ref, k_ref, v_ref, z_ref, gtf_ref, gtb_ref, wq_ref, wk_ref, wv_ref, nw_ref, alog_ref, bias_ref,
         s0_ref, o_ref, qs, ks, vs, oacc, st) = refs
        sout_ref = None
    else:
        (q_ref, k_ref, v_ref, z_ref, gtf_ref, gtb_ref, wq_ref, wk_ref, wv_ref, nw_ref, alog_ref, bias_ref,
         o_ref, sout_ref, qs, ks, vs, oacc, st) = refs
        s0_ref = None
    rt = _row_tile(t_len)
    n_tiles = t_len // rt
    n_chunks = t_len // CHUNK
    chains = [(d, h) for d in range(2) for h in range(GDN_HEADS)]

    def prep(r, carry):
        start = pl.multiple_of(r * rt, rt)
        for h in range(GDN_HEADS):
            off = h * LANES
            q = _silu(_conv_tile(q_ref, r, rt, n_tiles, wq_ref[:, off:off + LANES], None, off))
            k = _silu(_conv_tile(k_ref, r, rt, n_tiles, wk_ref[:, off:off + LANES], None, off))
            v = _silu(_conv_tile(v_ref, r, rt, n_tiles, wv_ref[:, off:off + LANES], None, off))
            q = q * lax.rsqrt(jnp.sum(q * q, axis=-1, keepdims=True) + EPS) * (GDN_DK ** -0.5)
            k = k * lax.rsqrt(jnp.sum(k * k, axis=-1, keepdims=True) + EPS)
            qs[pl.ds(start, rt), off:off + LANES] = q.astype(BF16)
            ks[pl.ds(start, rt), off:off + LANES] = k.astype(BF16)
            vs[pl.ds(start, rt), off:off + LANES] = v.astype(BF16)
        return carry

    lax.fori_loop(0, n_tiles, prep, 0)

    for ci, (d, h) in enumerate(chains):
        st[ci] = s0_ref[d, h] if has_ctx else jnp.zeros((GDN_DK, CHUNK), F32)

    sub = lax.broadcasted_iota(jnp.int32, (CHUNK, CHUNK), 0)
    lane = lax.broadcasted_iota(jnp.int32, (CHUNK, CHUNK), 1)
    eye = sub == lane
    eye_f = eye.astype(F32)
    incl = [lane <= sub, lane >= sub]
    strict = [lane < sub, lane > sub]
    eye_b = eye.astype(BF16)
    tri = [(sub <= lane).astype(BF16), (sub >= lane).astype(BF16)]
    grow = lax.broadcasted_iota(jnp.int32, (4 * GDN_HEADS, CHUNK), 0)
    is_beta = ((grow >> int(math.log2(GDN_HEADS))) & 1) == 0
    is_fwd = grow < 2 * GDN_HEADS
    neg_a = jnp.where(is_beta, 0.0, -jnp.exp(alog_ref[...]))
    beta_row = lambda d, h: 2 * d * GDN_HEADS + h
    decay_row = lambda d, h: (2 * d + 1) * GDN_HEADS + h

    def step(i, second):
        c0 = [pl.multiple_of(i * CHUNK, CHUNK), pl.multiple_of((n_chunks - 1 - i) * CHUNK, CHUNK)]
        raw = jnp.concatenate([gtf_ref[:, pl.ds(c0[0], CHUNK)], gtb_ref[:, pl.ds(c0[1], CHUNK)]], axis=0)
        g_all = neg_a * _softplus(raw + bias_ref[...])
        gc_all = jnp.where(is_fwd, _cumsum_rows(g_all, tri[0]), _cumsum_rows(g_all, tri[1]))
        comb = jnp.where(is_beta, jax.nn.sigmoid(raw), gc_all)
        comb_t = _rows_to_cols(comb, eye_b)
        cols = [slice(h * LANES, (h + 1) * LANES) for _, h in chains]
        qb = [qs[pl.ds(c0[d], CHUNK), cs] for (d, _), cs in zip(chains, cols)]
        kb = [ks[pl.ds(c0[d], CHUNK), cs] for (d, _), cs in zip(chains, cols)]
        vf = [vs[pl.ds(c0[d], CHUNK), cs].astype(F32) for (d, _), cs in zip(chains, cols)]
        qf = [a.astype(F32) for a in qb]
        kf = [a.astype(F32) for a in kb]
        gc = [comb[decay_row(d, h):decay_row(d, h) + 1, :] for d, h in chains]
        beta_col = [jnp.broadcast_to(comb_t[:, beta_row(d, h):beta_row(d, h) + 1], (CHUNK, CHUNK))
                    for d, h in chains]
        gc_col = [jnp.broadcast_to(comb_t[:, decay_row(d, h):decay_row(d, h) + 1], (CHUNK, CHUNK))
                  for d, h in chains]
        kk = [_nt(a, a) for a in kb]
        qk = [_nt(a, b) for a, b in zip(qb, kb)]
        dec = [jnp.where(incl[d], jnp.exp(jnp.where(incl[d], c - r, 0.0)), 0.0)
               for c, r, (d, _) in zip(gc_col, gc, chains)]
        nm = [jnp.where(strict[d], a * b * c, 0.0) for a, b, c, (d, _) in zip(kk, dec, beta_col, chains)]
        t_inv = _tri_inverse(nm, sub, lane, eye_f)
        eg = [jnp.exp(a) for a in gc_col]
        rhs = [jnp.concatenate([a * b, c * (b * e)], axis=1).astype(BF16)
               for a, b, c, e in zip(vf, beta_col, kf, eg)]
        sol = [_mm(a.astype(BF16), b) for a, b in zip(t_inv, rhs)]
        tot = [a[:, CHUNK - 1:CHUNK] if d == 0 else a[:, 0:1] for a, (d, _) in zip(gc, chains)]
        q_dec = [(a * e).astype(BF16) for a, e in zip(qf, eg)]
        k_dec = [(a * jnp.exp(t - c)).astype(BF16) for a, t, c in zip(kf, tot, gc_col)]
        intra = [(a * b).astype(BF16) for a, b in zip(qk, dec)]
        s_old = [st[ci] for ci in range(len(chains))]
        sb = [a.astype(BF16) for a in s_old]
        v_new = [a[:, :CHUNK] - _mm(a[:, CHUNK:].astype(BF16), b) for a, b in zip(sol, sb)]
        vb = [a.astype(BF16) for a in v_new]
        o = [_mm(a, b) + _mm(c, e) for a, b, c, e in zip(q_dec, sb, intra, vb)]
        s_new = [a * jnp.exp(t) + _tn(b, c) for a, t, b, c in zip(s_old, tot, k_dec, vb)]
        for ci, ((d, _), cs) in enumerate(zip(chains, cols)):
            st[ci] = s_new[ci]
            rows = pl.ds(c0[d], CHUNK)
            if not second:
                oacc[rows, cs] = o[ci]
            else:
                ot = oacc[rows, cs] + o[ci]
                ot = ot * lax.rsqrt(jnp.mean(ot * ot, axis=-1, keepdims=True) + EPS) * nw_ref[...]
                o_ref[rows, cs] = (ot * _silu(z_ref[rows, cs].astype(F32))).astype(o_ref.dtype)

    half = n_chunks // 2
    lax.fori_loop(0, half, lambda i, c: (step(i, False), c)[1], 0)
    lax.fori_loop(half, n_chunks, lambda i, c: (step(i, True), c)[1], 0)

    if sout_ref is not None:
        for ci, (d, h) in enumerate(chains):
            sout_ref[d, h] = st[ci]


def _gdn(p, conv_w, norm_w, a_log, dt_bias, s0, *, batch, t_len):
    n = batch * t_len
    has_ctx = s0 is not None
    assert (t_len // CHUNK) % 2 == 0
    big = dict(pipeline_mode=pl.Buffered(1)) if t_len * GDN_WIDTH * 2 >= (2 << 20) else {}
    seq = lambda j: pl.BlockSpec((t_len, GDN_WIDTH), lambda b, j=j: (b, j), **big)
    cw = lambda j: pl.BlockSpec((3, GDN_WIDTH), lambda b, j=j: (0, j))
    gate_rows = 2 * GDN_HEADS
    first = 2 * SSM_HEADS // gate_rows
    par = pl.BlockSpec((2 * gate_rows, 1), lambda b: (0, 0))
    state_spec = pl.BlockSpec((None, 2, GDN_HEADS, GDN_DK, CHUNK), lambda b: (b, 0, 0, 0, 0))
    in_specs = [seq(0), seq(1), seq(2), seq(0),
                pl.BlockSpec((gate_rows, t_len), lambda b: (first, b)),
                pl.BlockSpec((gate_rows, t_len), lambda b: (first + 1, b)),
                cw(0), cw(1), cw(2),
                pl.BlockSpec((1, LANES), lambda b: (0, 0)), par, par]
    args = [p["qkv"], p["qkv"], p["qkv"], p["az"], p["small_t"], p["small_t"], conv_w, conv_w, conv_w,
            norm_w, a_log, dt_bias]
    out_shape = [jax.ShapeDtypeStruct((n, GDN_WIDTH), BF16)]
    out_specs = [pl.BlockSpec((t_len, GDN_WIDTH), lambda b: (b, 0))]
    if has_ctx:
        in_specs.append(state_spec)
        args.append(s0)
    else:
        out_shape.append(jax.ShapeDtypeStruct((batch, 2, GDN_HEADS, GDN_DK, CHUNK), F32))
        out_specs.append(state_spec)
    scratch = [pltpu.VMEM((t_len, GDN_WIDTH), BF16) for _ in range(3)]
    scratch += [pltpu.VMEM((t_len, GDN_WIDTH), F32), pltpu.VMEM((2 * GDN_HEADS, GDN_DK, CHUNK), F32)]
    outs = pl.pallas_call(
        functools.partial(_gdn_kernel, t_len=t_len, has_ctx=has_ctx),
        grid=(batch,),
        in_specs=in_specs, out_specs=out_specs, out_shape=out_shape, scratch_shapes=scratch,
        compiler_params=pltpu.CompilerParams(
            dimension_semantics=("arbitrary",), vmem_limit_bytes=VMEM_LIMIT),
        name="gdn_ctx" if has_ctx else "gdn",
    )(*args)
    return outs[0], (None if has_ctx else outs[1])


SSD_GROUP_HEADS = 8
SSD_PAIRS = SSD_GROUP_HEADS // 2
SSD_GROUP_WIDTH = SSD_GROUP_HEADS * SSM_HEAD_DIM


def _ssd_kernel(*refs, t_len, has_ctx):
    if has_ctx:
        (x_ref, b_ref, c_ref, z_ref, dtf_ref, dtb_ref, wx_ref, wb_ref, wc_ref, bx_ref, bb_ref, bc_ref,
         dvec_ref, alog_ref, bias_ref, s0_ref, o_ref, xs, bs, cs, yacc, st) = refs
        sout_ref = None
    else:
        (x_ref, b_ref, c_ref, z_ref, dtf_ref, dtb_ref, wx_ref, wb_ref, wc_ref, bx_ref, bb_ref, bc_ref,
         dvec_ref, alog_ref, bias_ref, o_ref, sout_ref, xs, bs, cs, yacc, st) = refs
        s0_ref = None
    rt = _row_tile(t_len)
    n_tiles = t_len // rt
    n_chunks = t_len // CHUNK
    chains = [(d, q) for d in range(2) for q in range(SSD_PAIRS)]
    heads = [(d, q, j) for d, q in chains for j in range(2)]

    def prep(r, carry):
        start = pl.multiple_of(r * rt, rt)
        for q in range(SSD_PAIRS):
            off = q * LANES
            xs[pl.ds(start, rt), off:off + LANES] = _silu(_conv_tile(
                x_ref, r, rt, n_tiles, wx_ref[:, off:off + LANES], bx_ref[:, off:off + LANES], off)).astype(BF16)
        bs[pl.ds(start, rt), :] = _silu(_conv_tile(b_ref, r, rt, n_tiles, wb_ref[...], bb_ref[...], 0)).astype(BF16)
        cs[pl.ds(start, rt), :] = _silu(_conv_tile(c_ref, r, rt, n_tiles, wc_ref[...], bc_ref[...], 0)).astype(BF16)
        return carry

    lax.fori_loop(0, n_tiles, prep, 0)

    for ci, (d, q) in enumerate(chains):
        if has_ctx:
            st[ci] = jnp.concatenate([s0_ref[d, 2 * q], s0_ref[d, 2 * q + 1]], axis=0)
        else:
            st[ci] = jnp.zeros((2 * SSM_HEAD_DIM, SSM_STATE), F32)

    sub = lax.broadcasted_iota(jnp.int32, (CHUNK, CHUNK), 0)
    lane = lax.broadcasted_iota(jnp.int32, (CHUNK, CHUNK), 1)
    eye = sub == lane
    incl = [lane <= sub, lane >= sub]
    head0_lane = lane < SSM_HEAD_DIM
    head0_sub = sub < SSM_HEAD_DIM
    eye_b = eye.astype(BF16)
    tri = [(sub <= lane).astype(BF16), (sub >= lane).astype(BF16)]
    is_fwd = lax.broadcasted_iota(jnp.int32, (2 * SSD_GROUP_HEADS, CHUNK), 0) < SSD_GROUP_HEADS
    lane_r = lax.broadcasted_iota(jnp.int32, (2 * SSD_GROUP_HEADS, CHUNK), 1)
    neg_a = -jnp.exp(alog_ref[...])
    hrow = lambda d, q, j: d * SSD_GROUP_HEADS + 2 * q + j

    def step(i, second):
        c0 = [pl.multiple_of(i * CHUNK, CHUNK), pl.multiple_of((n_chunks - 1 - i) * CHUNK, CHUNK)]
        cols = [slice(q * LANES, (q + 1) * LANES) for _, q in chains]
        xb = [xs[pl.ds(c0[d], CHUNK), cs_] for (d, _), cs_ in zip(chains, cols)]
        xf = [a.astype(F32) for a in xb]
        bm = [bs[pl.ds(c0[d], CHUNK), :] for d in range(2)]
        cm = [cs[pl.ds(c0[d], CHUNK), :] for d in range(2)]
        cb = [_nt(cm[d], bm[d]) for d in range(2)]
        raw = jnp.concatenate([dtf_ref[:, pl.ds(c0[0], CHUNK)], dtb_ref[:, pl.ds(c0[1], CHUNK)]], axis=0)
        dt_all = _softplus(raw + bias_ref[...])
        da_all = dt_all * neg_a
        acum_all = jnp.where(is_fwd, _cumsum_rows(da_all, tri[0]), _cumsum_rows(da_all, tri[1]))
        tot_all = jnp.sum(jnp.where(lane_r == jnp.where(is_fwd, CHUNK - 1, 0), acum_all, 0.0),
                          axis=1, keepdims=True)
        w_all = dt_all * jnp.exp(tot_all - acum_all)
        cols_t = _rows_to_cols(acum_all, eye_b)
        cd_all = jnp.exp(tot_all)
        rows_ = [hrow(d, q, j) for d, q, j in heads]
        acum = [acum_all[r:r + 1, :] for r in rows_]
        dt = [dt_all[r:r + 1, :] for r in rows_]
        acum_b = [jnp.broadcast_to(cols_t[:, r:r + 1], (CHUNK, CHUNK)) for r in rows_]
        lmat = [jnp.where(incl[d], jnp.exp(jnp.where(incl[d], c - r, 0.0)), 0.0)
                for c, r, (d, _, _) in zip(acum_b, acum, heads)]
        m_mat = [(cb[d] * l * a).astype(BF16) for l, a, (d, _, _) in zip(lmat, dt, heads)]
        din_b = [jnp.exp(a) for a in acum_b]
        w_diag = [jnp.where(eye, w_all[r:r + 1, :], 0.0).astype(BF16) for r in rows_]
        cd = [cd_all[r:r + 1, :] for r in rows_]
        x0 = [jnp.where(head0_lane, a, jnp.zeros_like(a)) for a in xb]
        x1 = [jnp.where(head0_lane, jnp.zeros_like(a), a) for a in xb]
        y = [_mm(m_mat[2 * ci], x0[ci]) + _mm(m_mat[2 * ci + 1], x1[ci]) for ci in range(len(chains))]
        s_old = [st[ci] for ci in range(len(chains))]
        y = [a + _nt(cm[d], s.astype(BF16)) * jnp.where(head0_lane, din_b[2 * ci], din_b[2 * ci + 1])
             for ci, (a, s, (d, _)) in enumerate(zip(y, s_old, chains))]
        xw = [(_mm(w_diag[2 * ci], x0[ci]) + _mm(w_diag[2 * ci + 1], x1[ci])).astype(BF16)
              for ci in range(len(chains))]
        s_new = [s * jnp.where(head0_sub, cd[2 * ci], cd[2 * ci + 1]) + _tn(a, bm[d])
                 for ci, (s, a, (d, _)) in enumerate(zip(s_old, xw, chains))]
        for ci, ((d, _), cs_) in enumerate(zip(chains, cols)):
            st[ci] = s_new[ci]
            rows = pl.ds(c0[d], CHUNK)
            if not second:
                yacc[rows, cs_] = y[ci]
            else:
                yt = yacc[rows, cs_] + y[ci] + dvec_ref[:, cs_] * xf[ci]
                o_ref[rows, cs_] = (yt * _silu(z_ref[rows, cs_].astype(F32))).astype(o_ref.dtype)

    half = n_chunks // 2
    lax.fori_loop(0, half, lambda i, c: (step(i, False), c)[1], 0)
    lax.fori_loop(half, n_chunks, lambda i, c: (step(i, True), c)[1], 0)

    if sout_ref is not None:
        for ci, (d, q) in enumerate(chains):
            s = st[ci]
            sout_ref[d, 2 * q] = s[:SSM_HEAD_DIM]
            sout_ref[d, 2 * q + 1] = s[SSM_HEAD_DIM:]


def _ssd(p, conv_w, conv_b, d_vec, a_log, dt_bias, s0, *, batch, t_len):
    n = batch * t_len
    has_ctx = s0 is not None
    assert (t_len // CHUNK) % 2 == 0
    n_groups = SSM_HEADS // SSD_GROUP_HEADS
    x_blocks = D_INNER // LANES
    big = dict(pipeline_mode=pl.Buffered(1)) if t_len * SSD_GROUP_WIDTH * 2 >= (2 << 20) else {}
    par = pl.BlockSpec((None, 2 * SSD_GROUP_HEADS, 1), lambda b, g: (g, 0, 0))
    state_spec = pl.BlockSpec((None, 2, SSD_GROUP_HEADS, SSM_HEAD_DIM, SSM_STATE), lambda b, g: (b, 0, g, 0, 0))
    in_specs = [pl.BlockSpec((t_len, SSD_GROUP_WIDTH), lambda b, g: (b, g), **big),
                pl.BlockSpec((t_len, LANES), lambda b, g: (b, x_blocks + g)),
                pl.BlockSpec((t_len, LANES), lambda b, g: (b, x_blocks + n_groups + g)),
                pl.BlockSpec((t_len, SSD_GROUP_WIDTH), lambda b, g: (b, g), **big),
                pl.BlockSpec((SSD_GROUP_HEADS, t_len), lambda b, g: (g, b)),
                pl.BlockSpec((SSD_GROUP_HEADS, t_len), lambda b, g: (n_groups + g, b)),
                pl.BlockSpec((3, SSD_GROUP_WIDTH), lambda b, g: (0, g)),
                pl.BlockSpec((3, LANES), lambda b, g: (0, x_blocks + g)),
                pl.BlockSpec((3, LANES), lambda b, g: (0, x_blocks + n_groups + g)),
                pl.BlockSpec((1, SSD_GROUP_WIDTH), lambda b, g: (0, g)),
                pl.BlockSpec((1, LANES), lambda b, g: (0, x_blocks + g)),
                pl.BlockSpec((1, LANES), lambda b, g: (0, x_blocks + n_groups + g)),
                pl.BlockSpec((1, SSD_GROUP_WIDTH), lambda b, g: (0, g)), par, par]
    args = [p["xbc"], p["xbc"], p["xbc"], p["cz"], p["small_t"], p["small_t"], conv_w, conv_w, conv_w,
            conv_b, conv_b, conv_b, d_vec, a_log, dt_bias]
    out_shape = [jax.ShapeDtypeStruct((n, D_INNER), BF16)]
    out_specs = [pl.BlockSpec((t_len, SSD_GROUP_WIDTH), lambda b, g: (b, g))]
    if has_ctx:
        in_specs.append(state_spec)
        args.append(s0)
    else:
        out_shape.append(jax.ShapeDtypeStruct((batch, 2, SSM_HEADS, SSM_HEAD_DIM, SSM_STATE), F32))
        out_specs.append(state_spec)
    scratch = [pltpu.VMEM((t_len, SSD_GROUP_WIDTH), BF16), pltpu.VMEM((t_len, LANES), BF16),
               pltpu.VMEM((t_len, LANES), BF16), pltpu.VMEM((t_len, SSD_GROUP_WIDTH), F32),
               pltpu.VMEM((2 * SSD_PAIRS, 2 * SSM_HEAD_DIM, SSM_STATE), F32)]
    outs = pl.pallas_call(
        functools.partial(_ssd_kernel, t_len=t_len, has_ctx=has_ctx),
        grid=(batch, n_groups),
        in_specs=in_specs, out_specs=out_specs, out_shape=out_shape, scratch_shapes=scratch,
        compiler_params=pltpu.CompilerParams(
            dimension_semantics=("arbitrary", "arbitrary"), vmem_limit_bytes=VMEM_LIMIT),
        name="ssd_ctx" if has_ctx else "ssd",
    )(*args)
    return outs[0], (None if has_ctx else outs[1])


KEY_TILE = 256


def _seg_rmsnorm(x, bd):
    x2 = x * x
    hi = x2.astype(BF16)
    lo = (x2 - hi.astype(F32)).astype(BF16)
    ms = _mm(hi, bd) + _mm(lo, bd)
    return x * lax.rsqrt(ms + EPS)


def _rope128(x, cos, sin_signed):
    lane = lax.broadcasted_iota(jnp.int32, x.shape, 1)
    first = (lane % 32) < 16
    swapped = jnp.where(first, pltpu.roll(x, LANES - 16, 1), pltpu.roll(x, 16, 1))
    return x * cos + swapped * sin_signed


def _attn_kernel(*refs, t_len, tq, past, has_ctx):
    if has_ctx:
        (q_ref, kv_ref, z_ref, qnw_ref, knw_ref, bd_ref, cosq_ref, sinq_ref, cosk_ref, sink_ref,
         ck_ref, cv_ref, o_ref, k_s, v_s) = refs
        kn_ref = None
    else:
        (q_ref, kv_ref, z_ref, qnw_ref, knw_ref, bd_ref, o_ref, kn_ref, k_s, v_s) = refs
    qi = pl.program_id(1)
    rt = _row_tile(t_len)
    n_tiles = t_len // rt

    @pl.when(qi == 0)
    def _():
        def prep(r, carry):
            start = pl.multiple_of(r * rt, rt)
            kv = kv_ref[pl.ds(start, rt), :]
            kn = _seg_rmsnorm(kv[:, :KV_WIDTH], bd_ref[0:KV_WIDTH, 0:KV_WIDTH]) * knw_ref[...]
            if kn_ref is not None:
                kn_ref[pl.ds(start, rt), :] = kn
            if has_ctx:
                kn = _rope128(kn, cosk_ref[pl.ds(start, rt), :], sink_ref[pl.ds(start, rt), :])
            k_s[pl.ds(start, rt), :] = kn.astype(BF16)
            v_s[pl.ds(start, rt), :] = kv[:, KV_WIDTH:].astype(BF16)
            return carry

        lax.fori_loop(0, n_tiles, prep, 0)
        if has_ctx:
            k_s[t_len:t_len + past, :] = ck_ref[...].astype(BF16)
            v_s[t_len:t_len + past, :] = cv_ref[...].astype(BF16)

    q = _seg_rmsnorm(q_ref[...].astype(F32), bd_ref[...]) * qnw_ref[...]
    blocks = []
    for cbk in range(ATTN_WIDTH // LANES):
        blk = q[:, cbk * LANES:(cbk + 1) * LANES]
        if has_ctx:
            blk = _rope128(blk, cosq_ref[...], sinq_ref[...])
        blocks.append((blk * (HEAD_DIM ** -0.5)).astype(BF16))
    lane = lax.broadcasted_iota(jnp.int32, (tq, LANES), 1)
    low = lane < HEAD_DIM
    group = N_HEADS // KV_HEADS
    head_out = [None] * N_HEADS
    qgs = []
    for kvh in range(KV_HEADS):
        keep = low if kvh == 0 else jnp.logical_not(low)
        rows = []
        for g in range(group):
            head = kvh * group + g
            blk = blocks[head // 2]
            if head % 2 != kvh:
                blk = pltpu.roll(blk, HEAD_DIM, 1)
            rows.append(jnp.where(keep, blk, jnp.zeros_like(blk)))
        qgs.append(jnp.concatenate(rows, axis=0))

    n_kt = (t_len + past) // KEY_TILE
    keys = lambda j: slice(j * KEY_TILE, (j + 1) * KEY_TILE)
    score_tile = lambda kvh, j: _nt(qgs[kvh], k_s[keys(j), :])
    row_max = lambda tiles: jnp.max(functools.reduce(jnp.maximum, tiles), axis=-1, keepdims=True)
    s_tiles = [[score_tile(0, j) for j in range(n_kt)], []]
    m = [row_max(s_tiles[0]), None]
    p_tiles = [[], []]
    l_part = [None, None]

    def exp_tile(kvh, j):
        pe = jnp.exp(s_tiles[kvh][j] - m[kvh])
        l_part[kvh] = pe if l_part[kvh] is None else l_part[kvh] + pe
        p_tiles[kvh].append(pe.astype(BF16))

    acc = [None, None]

    def pv_tile(kvh, j):
        part = _mm(p_tiles[kvh][j], v_s[keys(j), :])
        acc[kvh] = part if acc[kvh] is None else acc[kvh] + part

    for j in range(n_kt):
        s_tiles[1].append(score_tile(1, j))
        exp_tile(0, j)
    m[1] = row_max(s_tiles[1])
    for j in range(n_kt):
        pv_tile(0, j)
        exp_tile(1, j)
    for j in range(n_kt):
        pv_tile(1, j)
    for kvh in range(KV_HEADS):
        o = acc[kvh] / jnp.sum(l_part[kvh], axis=-1, keepdims=True)
        for g in range(group):
            head = kvh * group + g
            og = o[g * tq:(g + 1) * tq, :]
            if head % 2 != kvh:
                og = pltpu.roll(og, HEAD_DIM, 1)
            head_out[head] = og
    outs = [jnp.where(low, head_out[2 * i], head_out[2 * i + 1]) for i in range(N_HEADS // 2)]
    o_all = jnp.concatenate(outs, axis=1)
    o_ref[...] = (o_all * _silu(z_ref[...].astype(F32))).astype(o_ref.dtype)


def _attention(p, qnw, knw, bd, rope_tabs, cache_k, cache_v, *, batch, t_len):
    n = batch * t_len
    has_ctx = cache_k is not None
    past = cache_k.shape[1] if has_ctx else 0
    tq = 128 if has_ctx else min(t_len, 256)
    nq = t_len // tq
    in_specs = [pl.BlockSpec((tq, ATTN_WIDTH), lambda b, i: (b * nq + i, 0)),
                pl.BlockSpec((t_len, 2 * KV_WIDTH), lambda b, i: (b, 0)),
                pl.BlockSpec((tq, ATTN_WIDTH), lambda b, i: (b * nq + i, 0)),
                pl.BlockSpec((1, ATTN_WIDTH), lambda b, i: (0, 0)),
                pl.BlockSpec((1, KV_WIDTH), lambda b, i: (0, 0)),
                pl.BlockSpec((ATTN_WIDTH, ATTN_WIDTH), lambda b, i: (0, 0))]
    args = [p["bq"], p["kv"], p["bz"], qnw, knw, bd]
    out_shape = [jax.ShapeDtypeStruct((n, ATTN_WIDTH), BF16)]
    out_specs = [pl.BlockSpec((tq, ATTN_WIDTH), lambda b, i: (b * nq + i, 0))]
    if has_ctx:
        cos, sin = rope_tabs
        in_specs += [pl.BlockSpec((tq, LANES), lambda b, i: (i, 0)),
                     pl.BlockSpec((tq, LANES), lambda b, i: (i, 0)),
                     pl.BlockSpec((t_len, LANES), lambda b, i: (0, 0)),
                     pl.BlockSpec((t_len, LANES), lambda b, i: (0, 0)),
                     pl.BlockSpec((None, past, KV_WIDTH), lambda b, i: (b, 0, 0)),
                     pl.BlockSpec((None, past, KV_WIDTH), lambda b, i: (b, 0, 0))]
        args += [cos, sin, cos, sin, cache_k, cache_v]
    else:
        out_shape.append(jax.ShapeDtypeStruct((n, KV_WIDTH), F32))
        out_specs.append(pl.BlockSpec((t_len, KV_WIDTH), lambda b, i: (b, 0)))
    scratch = [pltpu.VMEM((t_len + past, KV_WIDTH), BF16), pltpu.VMEM((t_len + past, KV_WIDTH), BF16)]
    outs = pl.pallas_call(
        functools.partial(_attn_kernel, t_len=t_len, tq=tq, past=past, has_ctx=has_ctx),
        grid=(batch, nq),
        in_specs=in_specs, out_specs=out_specs, out_shape=out_shape, scratch_shapes=scratch,
        compiler_params=pltpu.CompilerParams(
            dimension_semantics=("arbitrary", "arbitrary"), vmem_limit_bytes=VMEM_LIMIT),
        name="attn_ctx" if has_ctx else "attn",
    )(*args)
    return outs[0], (None if has_ctx else outs[1])


def _out_kernel(x_ref, oa_ref, ob_ref, yc_ref, g_ref, mod_ref, snw_ref, wa_ref, wb_ref, wc_ref, wo_ref,
                fnw_ref, o_ref, *, final):
    ba = _mm(oa_ref[...], wa_ref[...])
    bb = _mm(ob_ref[...], wb_ref[...])
    yc = yc_ref[...].astype(F32)
    yn = yc * lax.rsqrt(jnp.mean(yc * yc, axis=-1, keepdims=True) + EPS) * snw_ref[...]
    bc = _mm(yn.astype(BF16), wc_ref[...])
    g = g_ref[...].astype(F32)
    merged = (jax.nn.sigmoid(g[:, 0:D_MODEL]) * ba + jax.nn.sigmoid(g[:, D_MODEL:2 * D_MODEL]) * bb
              + jax.nn.sigmoid(g[:, 2 * D_MODEL:]) * bc)
    out = _mm(merged.astype(BF16), wo_ref[...])
    xn = x_ref[...] + mod_ref[:, 2 * D_MODEL:] * out
    if final:
        xn = xn * lax.rsqrt(jnp.mean(xn * xn, axis=-1, keepdims=True) + EPS) * fnw_ref[...]
    o_ref[...] = xn


def _out_proj(x, o_a, o_b, y_c, gates, mod, ssm_norm_w, wa, wb, wc, wo, final_norm_w, *,
              t_len, mod_base, mod_per_batch, final):
    n = x.shape[0]
    tm = 512 if t_len % 512 == 0 else 256
    assert n % tm == 0 and t_len % tm == 0

    def mod_idx(i):
        return (mod_base + (i * tm) // t_len if mod_per_batch else mod_base, 0, 0)

    tok = lambda w: pl.BlockSpec((tm, w), lambda i: (i, 0))
    full = lambda a, b: pl.BlockSpec((a, b), lambda i: (0, 0))
    return pl.pallas_call(
        functools.partial(_out_kernel, final=final),
        grid=(n // tm,),
        in_specs=[tok(D_MODEL), tok(GDN_WIDTH), tok(ATTN_WIDTH), tok(D_INNER), tok(3 * D_MODEL),
                  pl.BlockSpec((None, 1, 3 * D_MODEL), mod_idx), full(1, D_INNER),
                  full(GDN_WIDTH, D_MODEL), full(ATTN_WIDTH, D_MODEL), full(D_INNER, D_MODEL),
                  full(D_MODEL, D_MODEL), full(1, D_MODEL)],
        out_specs=tok(D_MODEL),
        out_shape=jax.ShapeDtypeStruct((n, D_MODEL), F32),
        compiler_params=pltpu.CompilerParams(
            dimension_semantics=("arbitrary",), vmem_limit_bytes=VMEM_LIMIT),
        name="out_proj",
    )(x, o_a, o_b, y_c, gates, mod, ssm_norm_w, wa, wb, wc, wo, final_norm_w)


def _rope_tables(t_len):
    rot = HEAD_DIM // 2
    pos = jnp.arange(t_len)
    freqs = ROPE_BASE ** (-jnp.arange(rot // 2, dtype=F32) / (rot // 2))
    ang_row = (pos // GRID_W).astype(F32)[:, None] * freqs
    ang_col = (pos % GRID_W).astype(F32)[:, None] * freqs
    ang = jnp.concatenate([ang_row, ang_row, ang_col, ang_col], axis=1)
    sign = jnp.tile(jnp.concatenate([-jnp.ones((rot // 2,), F32), jnp.ones((rot // 2,), F32)]), 2)
    cos = jnp.tile(jnp.cos(ang), (1, 2))
    sin = jnp.tile(jnp.sin(ang) * sign, (1, 2))
    return cos, sin


def _layer_params(l, w_in, gdn_conv_w, gdn_a_log, gdn_dt_bias, gdn_norm_w, attn_q_norm, attn_k_norm,
                  ssm_conv_w, ssm_conv_b, ssm_a_log, ssm_dt_bias, ssm_d, ssm_norm_w,
                  w_branch_a, w_branch_b, w_branch_c, w_out):
    w = w_in[l]
    sl = lambda name: w[:, _SRC[name][0]:_SRC[name][0] + _SRC[name][1]]
    w_main = jnp.concatenate([sl(k) for k in _MAIN], axis=1).astype(BF16)
    beta, alpha = sl("beta"), sl("alpha")
    hd = GDN_HEADS
    w_small_t = jnp.concatenate([sl("dt"), beta[:, :hd], alpha[:, :hd], beta[:, hd:], alpha[:, hd:]],
                                axis=1).T.astype(BF16)
    zeros = jnp.zeros((hd,), F32)

    def gdn_col(a):
        return jnp.concatenate([zeros, a[0], zeros, a[1]]).reshape(4 * hd, 1)

    def ssd_col(a):
        g = a.reshape(2, SSM_HEADS // SSD_GROUP_HEADS, SSD_GROUP_HEADS)
        return jnp.concatenate([g[0], g[1]], axis=1)[..., None]

    return dict(
        w_main=w_main, w_small_t=w_small_t,
        gdn_conv_w=gdn_conv_w[l], gdn_a_log=gdn_col(gdn_a_log[l]), gdn_dt_bias=gdn_col(gdn_dt_bias[l]),
        gdn_norm_w=gdn_norm_w[l].reshape(1, LANES),
        qnw=jnp.tile(attn_q_norm[l], N_HEADS).reshape(1, ATTN_WIDTH),
        knw=jnp.tile(attn_k_norm[l], KV_HEADS).reshape(1, KV_WIDTH),
        ssm_conv_w=ssm_conv_w[l], ssm_conv_b=ssm_conv_b[l].reshape(1, SSM_XBC),
        ssm_a_log=ssd_col(ssm_a_log[l]), ssm_dt_bias=ssd_col(ssm_dt_bias[l]),
        ssm_d=jnp.repeat(ssm_d[l], SSM_HEAD_DIM).reshape(1, D_INNER),
        ssm_norm_w=ssm_norm_w[l].reshape(1, D_INNER),
        wa=w_branch_a[l].astype(BF16), wb=w_branch_b[l].astype(BF16),
        wc=w_branch_c[l].astype(BF16), wo=w_out[l].astype(BF16))


def _stream_layer(x, lp, mod_l, norm_w_l, final_norm_w, bd, rope_tabs, ctx, *,
                  batch, t_len, mod_base, mod_per_batch, final):
    p = _in_proj(x, norm_w_l, mod_l, lp["w_main"], lp["w_small_t"],
                 t_len=t_len, mod_base=mod_base, mod_per_batch=mod_per_batch)
    ck, cv, sg, ss = ctx if ctx is not None else (None, None, None, None)
    o_a, new_sg = _gdn(p, lp["gdn_conv_w"], lp["gdn_norm_w"], lp["gdn_a_log"], lp["gdn_dt_bias"], sg,
                       batch=batch, t_len=t_len)
    y_c, new_ss = _ssd(p, lp["ssm_conv_w"], lp["ssm_conv_b"], lp["ssm_d"], lp["ssm_a_log"],
                       lp["ssm_dt_bias"], ss, batch=batch, t_len=t_len)
    o_b, k_norm = _attention(p, lp["qnw"], lp["knw"], bd, rope_tabs, ck, cv, batch=batch, t_len=t_len)
    x_new = _out_proj(x, o_a, o_b, y_c, p["gates"], mod_l, lp["ssm_norm_w"], lp["wa"], lp["wb"],
                      lp["wc"], lp["wo"], final_norm_w, t_len=t_len, mod_base=mod_base,
                      mod_per_batch=mod_per_batch, final=final)
    return x_new, (k_norm, p["kv"], new_sg, new_ss)


def kernel(x_prompt, x_sample, cache_k, cache_v, state_gdn, state_ssm, c, c_ctx, norm_w, w_mod, b_mod, w_in, gdn_conv_w, gdn_a_log, gdn_dt_bias, gdn_norm_w, attn_q_norm, attn_k_norm, ssm_conv_w, ssm_conv_b, ssm_a_log, ssm_dt_bias, ssm_d, ssm_norm_w, w_branch_a, w_branch_b, w_branch_c, w_out, final_norm_w):
    batch, seq, d_model = x_prompt.shape
    dec_batch, dec_seq, _ = x_sample.shape
    depth = w_in.shape[0]
    past = cache_k.shape[2]
    assert d_model == D_MODEL and seq % CHUNK == 0 and dec_seq % CHUNK == 0

    rows = -(-(1 + dec_batch) // 8) * 8
    cond = jnp.zeros((rows, D_MODEL), F32).at[0].set(c_ctx).at[1:1 + dec_batch].set(c)
    mod = _modulation(cond, w_mod, b_mod).reshape(depth, rows, 1, 3 * D_MODEL)

    seg = jnp.arange(ATTN_WIDTH) // HEAD_DIM
    bd = jnp.where(seg[:, None] == seg[None, :], 1.0 / HEAD_DIM, 0.0).astype(BF16)
    rope_tabs = _rope_tables(dec_seq)
    fnw = final_norm_w.reshape(1, D_MODEL)

    xp = x_prompt.reshape(batch * seq, D_MODEL)
    xs = x_sample.reshape(dec_batch * dec_seq, D_MODEL)
    new_k, new_v, new_gdn, new_ssm = [], [], [], []
    for l in range(depth):
        lp = _layer_params(l, w_in, gdn_conv_w, gdn_a_log, gdn_dt_bias, gdn_norm_w, attn_q_norm,
                           attn_k_norm, ssm_conv_w, ssm_conv_b, ssm_a_log, ssm_dt_bias, ssm_d,
                           ssm_norm_w, w_branch_a, w_branch_b, w_branch_c, w_out)
        nw = norm_w[l].reshape(1, D_MODEL)
        final = l == depth - 1
        xp, (k_norm, kv, s_g, s_s) = _stream_layer(
            xp, lp, mod[l], nw, fnw, bd, None, None,
            batch=batch, t_len=seq, mod_base=0, mod_per_batch=False, final=final)
        new_k.append(k_norm.reshape(batch, seq, KV_HEADS, HEAD_DIM))
        new_v.append(kv[:, KV_WIDTH:].reshape(batch, seq, KV_HEADS, HEAD_DIM))
        new_gdn.append(s_g)
        new_ssm.append(s_s)
        ctx = (cache_k[:, l].reshape(dec_batch, past, KV_WIDTH),
               cache_v[:, l].reshape(dec_batch, past, KV_WIDTH),
               state_gdn[:, l], state_ssm[:, l])
        xs, _ = _stream_layer(
            xs, lp, mod[l], nw, fnw, bd, rope_tabs, ctx,
            batch=dec_batch, t_len=dec_seq, mod_base=1, mod_per_batch=True, final=final)
    return (xp.reshape(batch, seq, D_MODEL), xs.reshape(dec_batch, dec_seq, D_MODEL),
            jnp.stack(new_k, axis=1), jnp.stack(new_v, axis=1),
            jnp.stack(new_gdn, axis=1), jnp.stack(new_ssm, axis=1))
```

```python
import functools
import math

import jax
import jax.numpy as jnp
from jax import lax
from jax.experimental import pallas as pl
from jax.experimental.pallas import tpu as pltpu

F32 = jnp.float32
BF16 = jnp.bfloat16

D_MODEL = 1024
EPS = 1e-6
GDN_HEADS = 4
GDN_DK = 128
GDN_WIDTH = 512
N_HEADS = 8
KV_HEADS = 2
HEAD_DIM = 64
ATTN_WIDTH = 512
KV_WIDTH = 128
GRID_W = 64
ROPE_BASE = 10000.0
SSM_HEADS = 16
SSM_HEAD_DIM = 64
D_INNER = 1024
SSM_STATE = 128
SSM_XBC = 1536
CHUNK = 128
LANES = 128
SMALL_ROWS = 48
VMEM_LIMIT = 56 * 1024 * 1024

_SRC = dict(qkv=(0, 1536), az=(1536, 512), beta=(2048, 8), alpha=(2056, 8), bq=(2064, 512),
            kv=(2576, 256), bz=(2832, 512), xbc=(3344, 1536), cz=(4880, 1024), dt=(5904, 32),
            gates=(5936, 3072))
_MAIN = ("qkv", "az", "bq", "kv", "bz", "xbc", "cz", "gates")
_MAIN_OFF = {}
_o = 0
for _n in _MAIN:
    _MAIN_OFF[_n] = _o
    _o += _SRC[_n][1]
MAIN_WIDTH = _o


def _nt(a, b):
    return lax.dot_general(a, b, (((1,), (1,)), ((), ())), preferred_element_type=F32)


def _tn(a, b):
    return lax.dot_general(a, b, (((0,), (0,)), ((), ())), preferred_element_type=F32)


def _mm(a, b):
    return jnp.dot(a, b, preferred_element_type=F32)


def _silu(x):
    return x * jax.nn.sigmoid(x)


def _softplus(x):
    return jnp.maximum(x, 0.0) + jnp.log1p(jnp.exp(-jnp.abs(x)))


def _split3(x):
    hi = x.astype(BF16)
    r1 = x - hi.astype(F32)
    mid = r1.astype(BF16)
    lo = (r1 - mid.astype(F32)).astype(BF16)
    return hi, mid, lo


def _cumsum_rows(x, tri):
    hi, mid, lo = _split3(x)
    return _mm(hi, tri) + _mm(mid, tri) + _mm(lo, tri)


def _rows_to_cols(x, eye_b):
    hi, mid, lo = _split3(x)
    return _nt(eye_b, hi) + _nt(eye_b, mid) + _nt(eye_b, lo)


def _conv_tile(src_ref, r, rt, n_tiles, w, bias, off):
    t_len = src_ref.shape[0]
    cols = slice(off, off + LANES)
    start = pl.multiple_of(r * rt, rt)
    cur = src_ref[pl.ds(start, rt), cols].astype(F32)
    pstart = pl.multiple_of(jnp.maximum(start - 16, 0), 16)
    nstart = pl.multiple_of(jnp.minimum(start + rt, t_len - 16), 16)
    prev_row = src_ref[pl.ds(pstart, 16), cols][15:16, :].astype(F32)
    next_row = src_ref[pl.ds(nstart, 16), cols][0:1, :].astype(F32)
    prev_row = jnp.where(r > 0, prev_row, 0.0)
    next_row = jnp.where(r < n_tiles - 1, next_row, 0.0)
    ri = lax.broadcasted_iota(jnp.int32, (rt, LANES), 0)
    x_prev = jnp.where(ri == 0, prev_row, pltpu.roll(cur, 1, 0))
    x_next = jnp.where(ri == rt - 1, next_row, pltpu.roll(cur, rt - 1, 0))
    y = w[0:1, :] * x_prev + w[1:2, :] * cur + w[2:3, :] * x_next
    if bias is not None:
        y = y + bias
    return y


def _row_tile(t_len):
    return min(t_len, 256)


def _mod_kernel(c_ref, w_ref, b_ref, o_ref):
    s = _silu(c_ref[...])
    w = w_ref[...]
    s_hi = s.astype(BF16)
    s_lo = (s - s_hi.astype(F32)).astype(BF16)
    w_hi = w.astype(BF16)
    w_lo = (w - w_hi.astype(F32)).astype(BF16)
    o_ref[...] = _mm(s_hi, w_hi) + _mm(s_lo, w_hi) + _mm(s_hi, w_lo) + b_ref[...]


def _modulation(cond, w_mod, b_mod):
    depth = w_mod.shape[0]
    rows = cond.shape[0]
    tn = 1024
    return pl.pallas_call(
        _mod_kernel,
        grid=(depth, 3 * D_MODEL // tn),
        in_specs=[
            pl.BlockSpec((rows, D_MODEL), lambda l, j: (0, 0)),
            pl.BlockSpec((None, D_MODEL, tn), lambda l, j: (l, 0, j)),
            pl.BlockSpec((None, 1, tn), lambda l, j: (l, 0, j)),
        ],
        out_specs=pl.BlockSpec((None, rows, tn), lambda l, j: (l, 0, j)),
        out_shape=jax.ShapeDtypeStruct((depth, rows, 3 * D_MODEL), F32),
        name="modulation",
    )(cond, w_mod, b_mod.reshape(depth, 1, 3 * D_MODEL))


def _in_kernel(x_ref, nw_ref, mod_ref, w_ref, wst_ref,
               o_qkv, o_az, o_bq, o_kv, o_bz, o_xbc, o_cz, o_gates, o_small):
    x = x_ref[...]
    y = x * lax.rsqrt(jnp.mean(x * x, axis=-1, keepdims=True) + EPS) * nw_ref[...]
    shift = mod_ref[:, 0:D_MODEL]
    scale = mod_ref[:, D_MODEL:2 * D_MODEL]
    h = (y * (1.0 + scale) + shift).astype(BF16)
    outs = dict(qkv=o_qkv, az=o_az, bq=o_bq, kv=o_kv, bz=o_bz, xbc=o_xbc, cz=o_cz, gates=o_gates)
    for name in _MAIN:
        off, width = _MAIN_OFF[name], _SRC[name][1]
        outs[name][...] = _mm(h, w_ref[:, off:off + width]).astype(outs[name].dtype)
    o_small[...] = _nt(wst_ref[...], h)


def _in_proj(x, norm_w, mod, w_main, w_small_t, *, t_len, mod_base, mod_per_batch):
    n = x.shape[0]
    tm = 256
    assert n % tm == 0 and t_len % tm == 0

    def mod_idx(i):
        return (mod_base + (i * tm) // t_len if mod_per_batch else mod_base, 0, 0)

    widths = {k: _SRC[k][1] for k in _MAIN}
    dtypes = {k: (F32 if k == "kv" else BF16) for k in _MAIN}
    out_shape = [jax.ShapeDtypeStruct((n, widths[k]), dtypes[k]) for k in _MAIN]
    out_shape.append(jax.ShapeDtypeStruct((SMALL_ROWS, n), F32))
    out_specs = [pl.BlockSpec((tm, widths[k]), lambda i: (i, 0)) for k in _MAIN]
    out_specs.append(pl.BlockSpec((SMALL_ROWS, tm), lambda i: (0, i)))
    outs = pl.pallas_call(
        _in_kernel,
        grid=(n // tm,),
        in_specs=[
            pl.BlockSpec((tm, D_MODEL), lambda i: (i, 0)),
            pl.BlockSpec((1, D_MODEL), lambda i: (0, 0)),
            pl.BlockSpec((None, 1, 3 * D_MODEL), mod_idx),
            pl.BlockSpec((D_MODEL, MAIN_WIDTH), lambda i: (0, 0), pipeline_mode=pl.Buffered(1)),
            pl.BlockSpec((SMALL_ROWS, D_MODEL), lambda i: (0, 0)),
        ],
        out_specs=out_specs,
        out_shape=out_shape,
        compiler_params=pltpu.CompilerParams(
            dimension_semantics=("arbitrary",), vmem_limit_bytes=VMEM_LIMIT),
        name="in_proj",
    )(x, norm_w, mod, w_main, w_small_t)
    res = dict(zip(_MAIN, outs[:-1]))
    res["small_t"] = outs[-1]
    return res


INV_BASE = 8


def _tri_inverse(nms, dirs, sub, lane, eye_f):
    def same_block(s):
        sh = int(math.log2(s))
        return (sub >> sh) == (lane >> sh)

    base = same_block(INV_BASE)
    n8 = [jnp.where(base, nm, 0.0) for nm in nms]
    p = [n.astype(BF16) for n in n8]
    y = [eye_f - n for n in n8]
    p2 = [_mm(a, a).astype(BF16) for a in p]
    both = [_mm(a, jnp.concatenate([a, b.astype(BF16)], axis=1)) for a, b in zip(p2, y)]
    y = [b + c[:, CHUNK:] for b, c in zip(y, both)]
    p4 = [c[:, :CHUNK].astype(BF16) for c in both]
    y = [b + _mm(a, b.astype(BF16)) for a, b in zip(p4, y)]
    s = INV_BASE
    while s < CHUNK:
        sel = jnp.logical_and(same_block(2 * s), jnp.logical_not(same_block(s)))
        off = [jnp.where(sel, nm, 0.0).astype(BF16) for nm in nms]
        yb = [b.astype(BF16) for b in y]
        starts = [[2 * s * b + (s if d == 0 else 0) for b in range(CHUNK // (2 * s))] for d in dirs]
        rows = [jnp.concatenate([b[a:a + s] for a in st], axis=0).astype(BF16) for b, st in zip(y, starts)]
        t = [_mm(r, o).astype(BF16) for r, o in zip(rows, off)]
        upd = [_mm(a, c) for a, c in zip(t, yb)]
        new_y = []
        for b, u, st in zip(y, upd, starts):
            parts = []
            for n, a in enumerate(st):
                keep = b[a - s:a] if a % (2 * s) else b[a + s:a + 2 * s]
                changed = b[a:a + s] - u[n * s:(n + 1) * s]
                parts += [keep, changed] if a % (2 * s) else [changed, keep]
            new_y.append(jnp.concatenate(parts, axis=0))
        y = new_y
        s *= 2
    return y


def _gdn_kernel(*refs, t_len, has_ctx):
    if has_ctx:
        (q_ref, k_ref, v_ref, z_ref, gtf_ref, gtb_ref, wq_ref, wk_ref, wv_ref, nw_ref, alog_ref, bias_ref,
         s0_ref, o_ref, qs, ks, vs, oacc, st) = refs
        sout_ref = None
    else:
        (q_ref, k_ref, v_ref, z_ref, gtf_ref, gtb_ref, wq_ref, wk_ref, wv_ref, nw_ref, alog_ref, bias_ref,
         o_ref, sout_ref, qs, ks, vs, oacc, st) = refs
        s0_ref = None
    rt = _row_tile(t_len)
    n_tiles = t_len // rt
    n_chunks = t_len // CHUNK
    chains = [(d, h) for d in range(2) for h in range(GDN_HEADS)]

    def prep(r, carry):
        start = pl.multiple_of(r * rt, rt)
        for h in range(GDN_HEADS):
            off = h * LANES
            q = _silu(_conv_tile(q_ref, r, rt, n_tiles, wq_ref[:, off:off + LANES], None, off))
            k = _silu(_conv_tile(k_ref, r, rt, n_tiles, wk_ref[:, off:off + LANES], None, off))
            v = _silu(_conv_tile(v_ref, r, rt, n_tiles, wv_ref[:, off:off + LANES], None, off))
            q = q * lax.rsqrt(jnp.sum(q * q, axis=-1, keepdims=True) + EPS) * (GDN_DK ** -0.5)
            k = k * lax.rsqrt(jnp.sum(k * k, axis=-1, keepdims=True) + EPS)
            qs[pl.ds(start, rt), off:off + LANES] = q.astype(BF16)
            ks[pl.ds(start, rt), off:off + LANES] = k.astype(BF16)
            vs[pl.ds(start, rt), off:off + LANES] = v.astype(BF16)
        return carry

    lax.fori_loop(0, n_tiles, prep, 0)

    for ci, (d, h) in enumerate(chains):
        st[ci] = s0_ref[d, h] if has_ctx else jnp.zeros((GDN_DK, CHUNK), F32)

    sub = lax.broadcasted_iota(jnp.int32, (CHUNK, CHUNK), 0)
    lane = lax.broadcasted_iota(jnp.int32, (CHUNK, CHUNK), 1)
    eye = sub == lane
    eye_f = eye.astype(F32)
    incl = [lane <= sub, lane >= sub]
    strict = [lane < sub, lane > sub]
    eye_b = eye.astype(BF16)
    tri = [(sub <= lane).astype(BF16), (sub >= lane).astype(BF16)]
    grow = lax.broadcasted_iota(jnp.int32, (4 * GDN_HEADS, CHUNK), 0)
    is_beta = ((grow >> int(math.log2(GDN_HEADS))) & 1) == 0
    is_fwd = grow < 2 * GDN_HEADS
    neg_a = jnp.where(is_beta, 0.0, -jnp.exp(alog_ref[...]))
    beta_row = lambda d, h: 2 * d * GDN_HEADS + h
    decay_row = lambda d, h: (2 * d + 1) * GDN_HEADS + h

    half = n_chunks // 2
    npar = 2 if half % 2 == 0 else 1

    def step(i, second):
        c0 = [[pl.multiple_of((i * npar + k) * CHUNK, CHUNK),
               pl.multiple_of((n_chunks - 1 - (i * npar + k)) * CHUNK, CHUNK)] for k in range(npar)]
        raw = [jnp.concatenate([gtf_ref[:, pl.ds(c[0], CHUNK)], gtb_ref[:, pl.ds(c[1], CHUNK)]], axis=0)
               for c in c0]
        g_all = [neg_a * _softplus(r + bias_ref[...]) for r in raw]
        gc_all = [jnp.where(is_fwd, _cumsum_rows(g, tri[0]), _cumsum_rows(g, tri[1])) for g in g_all]
        comb = [jnp.where(is_beta, jax.nn.sigmoid(r), g) for r, g in zip(raw, gc_all)]
        comb_t = [_rows_to_cols(c, eye_b) for c in comb]
        ch = [(k, d, h) for k in range(npar) for d, h in chains]
        cols = [slice(h * LANES, (h + 1) * LANES) for _, _, h in ch]
        qb = [qs[pl.ds(c0[k][d], CHUNK), cs] for (k, d, _), cs in zip(ch, cols)]
        kb = [ks[pl.ds(c0[k][d], CHUNK), cs] for (k, d, _), cs in zip(ch, cols)]
        vf = [vs[pl.ds(c0[k][d], CHUNK), cs].astype(F32) for (k, d, _), cs in zip(ch, cols)]
        qf = [a.astype(F32) for a in qb]
        kf = [a.astype(F32) for a in kb]
        gc = [comb[k][decay_row(d, h):decay_row(d, h) + 1, :] for k, d, h in ch]
        beta_col = [jnp.broadcast_to(comb_t[k][:, beta_row(d, h):beta_row(d, h) + 1], (CHUNK, CHUNK))
                    for k, d, h in ch]
        gc_col = [jnp.broadcast_to(comb_t[k][:, decay_row(d, h):decay_row(d, h) + 1], (CHUNK, CHUNK))
                  for k, d, h in ch]
        kk = [_nt(a, a) for a in kb]
        qk = [_nt(a, b) for a, b in zip(qb, kb)]
        dec = [jnp.where(incl[d], jnp.exp(jnp.where(incl[d], c - r, 0.0)), 0.0)
               for c, r, (_, d, _) in zip(gc_col, gc, ch)]
        nm = [jnp.where(strict[d], a * b * c, 0.0) for a, b, c, (_, d, _) in zip(kk, dec, beta_col, ch)]
        t_inv = _tri_inverse(nm, [d for _, d, _ in ch], sub, lane, eye_f)
        eg = [jnp.exp(a) for a in gc_col]
        rhs = [jnp.concatenate([a * b, c * (b * e)], axis=1).astype(BF16)
               for a, b, c, e in zip(vf, beta_col, kf, eg)]
        sol = [_mm(a.astype(BF16), b) for a, b in zip(t_inv, rhs)]
        tot = [a[:, CHUNK - 1:CHUNK] if d == 0 else a[:, 0:1] for a, (_, d, _) in zip(gc, ch)]
        q_dec = [(a * e).astype(BF16) for a, e in zip(qf, eg)]
        k_dec = [(a * jnp.exp(t - c)).astype(BF16) for a, t, c in zip(kf, tot, gc_col)]
        intra = [(a * b).astype(BF16) for a, b in zip(qk, dec)]
        state = [st[ci] for ci in range(len(chains))]
        nc = len(chains)
        for k in range(npar):
            sel = slice(k * nc, (k + 1) * nc)
            sb = [a.astype(BF16) for a in state]
            v_new = [a[:, :CHUNK] - _mm(a[:, CHUNK:].astype(BF16), b) for a, b in zip(sol[sel], sb)]
            vb = [a.astype(BF16) for a in v_new]
            o = [_mm(a, b) + _mm(c, e) for a, b, c, e in zip(q_dec[sel], sb, intra[sel], vb)]
            state = [a * jnp.exp(t) + _tn(b, c) for a, t, b, c in zip(state, tot[sel], k_dec[sel], vb)]
            for ci, (d, h) in enumerate(chains):
                rows = pl.ds(c0[k][d], CHUNK)
                cs = slice(h * LANES, (h + 1) * LANES)
                if not second:
                    oacc[rows, cs] = o[ci]
                else:
                    ot = oacc[rows, cs] + o[ci]
                    ot = ot * lax.rsqrt(jnp.mean(ot * ot, axis=-1, keepdims=True) + EPS) * nw_ref[...]
                    o_ref[rows, cs] = (ot * _silu(z_ref[rows, cs].astype(F32))).astype(o_ref.dtype)
        for ci in range(nc):
            st[ci] = state[ci]

    lax.fori_loop(0, half // npar, lambda i, c: (step(i, False), c)[1], 0)
    lax.fori_loop(half // npar, n_chunks // npar, lambda i, c: (step(i, True), c)[1], 0)

    if sout_ref is not None:
        for ci, (d, h) in enumerate(chains):
            sout_ref[d, h] = st[ci]


def _gdn(p, conv_w, norm_w, a_log, dt_bias, s0, *, batch, t_len):
    n = batch * t_len
    has_ctx = s0 is not None
    assert (t_len // CHUNK) % 2 == 0
    big = dict(pipeline_mode=pl.Buffered(1)) if t_len * GDN_WIDTH * 2 >= (2 << 20) else {}
    seq = lambda j: pl.BlockSpec((t_len, GDN_WIDTH), lambda b, j=j: (b, j), **big)
    cw = lambda j: pl.BlockSpec((3, GDN_WIDTH), lambda b, j=j: (0, j))
    gate_rows = 2 * GDN_HEADS
    first = 2 * SSM_HEADS // gate_rows
    par = pl.BlockSpec((2 * gate_rows, 1), lambda b: (0, 0))
    state_spec = pl.BlockSpec((None, 2, GDN_HEADS, GDN_DK, CHUNK), lambda b: (b, 0, 0, 0, 0))
    in_specs = [seq(0), seq(1), seq(2), seq(0),
                pl.BlockSpec((gate_rows, t_len), lambda b: (first, b)),
                pl.BlockSpec((gate_rows, t_len), lambda b: (first + 1, b)),
                cw(0), cw(1), cw(2),
                pl.BlockSpec((1, LANES), lambda b: (0, 0)), par, par]
    args = [p["qkv"], p["qkv"], p["qkv"], p["az"], p["small_t"], p["small_t"], conv_w, conv_w, conv_w,
            norm_w, a_log, dt_bias]
    out_shape = [jax.ShapeDtypeStruct((n, GDN_WIDTH), BF16)]
    out_specs = [pl.BlockSpec((t_len, GDN_WIDTH), lambda b: (b, 0))]
    if has_ctx:
        in_specs.append(state_spec)
        args.append(s0)
    else:
        out_shape.append(jax.ShapeDtypeStruct((batch, 2, GDN_HEADS, GDN_DK, CHUNK), F32))
        out_specs.append(state_spec)
    scratch = [pltpu.VMEM((t_len, GDN_WIDTH), BF16) for _ in range(3)]
    scratch += [pltpu.VMEM((t_len, GDN_WIDTH), F32), pltpu.VMEM((2 * GDN_HEADS, GDN_DK, CHUNK), F32)]
    outs = pl.pallas_call(
        functools.partial(_gdn_kernel, t_len=t_len, has_ctx=has_ctx),
        grid=(batch,),
        in_specs=in_specs, out_specs=out_specs, out_shape=out_shape, scratch_shapes=scratch,
        compiler_params=pltpu.CompilerParams(
            dimension_semantics=("arbitrary",), vmem_limit_bytes=VMEM_LIMIT),
        name="gdn_ctx" if has_ctx else "gdn",
    )(*args)
    return outs[0], (None if has_ctx else outs[1])


SSD_GROUP_HEADS = 8
SSD_PAIRS = SSD_GROUP_HEADS // 2
SSD_GROUP_WIDTH = SSD_GROUP_HEADS * SSM_HEAD_DIM


def _ssd_kernel(*refs, t_len, has_ctx):
    if has_ctx:
        (x_ref, b_ref, c_ref, z_ref, dtf_ref, dtb_ref, wx_ref, wb_ref, wc_ref, bx_ref, bb_ref, bc_ref,
         dvec_ref, alog_ref, bias_ref, s0_ref, o_ref, xs, bs, cs, yacc, st) = refs
        sout_ref = None
    else:
        (x_ref, b_ref, c_ref, z_ref, dtf_ref, dtb_ref, wx_ref, wb_ref, wc_ref, bx_ref, bb_ref, bc_ref,
         dvec_ref, alog_ref, bias_ref, o_ref, sout_ref, xs, bs, cs, yacc, st) = refs
        s0_ref = None
    rt = _row_tile(t_len)
    n_tiles = t_len // rt
    n_chunks = t_len // CHUNK
    chains = [(d, q) for d in range(2) for q in range(SSD_PAIRS)]

    def prep(r, carry):
        start = pl.multiple_of(r * rt, rt)
        for q in range(SSD_PAIRS):
            off = q * LANES
            xs[pl.ds(start, rt), off:off + LANES] = _silu(_conv_tile(
                x_ref, r, rt, n_tiles, wx_ref[:, off:off + LANES], bx_ref[:, off:off + LANES], off)).astype(BF16)
        bs[pl.ds(start, rt), :] = _silu(_conv_tile(b_ref, r, rt, n_tiles, wb_ref[...], bb_ref[...], 0)).astype(BF16)
        cs[pl.ds(start, rt), :] = _silu(_conv_tile(c_ref, r, rt, n_tiles, wc_ref[...], bc_ref[...], 0)).astype(BF16)
        return carry

    lax.fori_loop(0, n_tiles, prep, 0)

    for ci, (d, q) in enumerate(chains):
        if has_ctx:
            st[ci] = jnp.concatenate([s0_ref[d, 2 * q], s0_ref[d, 2 * q + 1]], axis=0)
        else:
            st[ci] = jnp.zeros((2 * SSM_HEAD_DIM, SSM_STATE), F32)

    sub = lax.broadcasted_iota(jnp.int32, (CHUNK, CHUNK), 0)
    lane = lax.broadcasted_iota(jnp.int32, (CHUNK, CHUNK), 1)
    eye = sub == lane
    incl = [lane <= sub, lane >= sub]
    head0_lane = lane < SSM_HEAD_DIM
    head0_sub = sub < SSM_HEAD_DIM
    eye_b = eye.astype(BF16)
    tri = [(sub <= lane).astype(BF16), (sub >= lane).astype(BF16)]
    is_fwd = lax.broadcasted_iota(jnp.int32, (2 * SSD_GROUP_HEADS, CHUNK), 0) < SSD_GROUP_HEADS
    lane_r = lax.broadcasted_iota(jnp.int32, (2 * SSD_GROUP_HEADS, CHUNK), 1)
    neg_a = -jnp.exp(alog_ref[...])
    hrow = lambda d, q, j: d * SSD_GROUP_HEADS + 2 * q + j

    half = n_chunks // 2
    npar = max(n for n in (1, 2, 4) if half % n == 0)

    def step(i, second):
        c0 = [[pl.multiple_of((i * npar + k) * CHUNK, CHUNK),
               pl.multiple_of((n_chunks - 1 - (i * npar + k)) * CHUNK, CHUNK)] for k in range(npar)]
        kd = [(k, d) for k in range(npar) for d in range(2)]
        bm = {(k, d): bs[pl.ds(c0[k][d], CHUNK), :] for k, d in kd}
        cm = {(k, d): cs[pl.ds(c0[k][d], CHUNK), :] for k, d in kd}
        cb = {key: _nt(cm[key], bm[key]) for key in kd}
        raw = [jnp.concatenate([dtf_ref[:, pl.ds(c[0], CHUNK)], dtb_ref[:, pl.ds(c[1], CHUNK)]], axis=0)
               for c in c0]
        dt_all = [_softplus(r + bias_ref[...]) for r in raw]
        da_all = [a * neg_a for a in dt_all]
        acum_all = [jnp.where(is_fwd, _cumsum_rows(a, tri[0]), _cumsum_rows(a, tri[1])) for a in da_all]
        tot_all = [jnp.sum(jnp.where(lane_r == jnp.where(is_fwd, CHUNK - 1, 0), a, 0.0), axis=1, keepdims=True)
                   for a in acum_all]
        w_all = [a * jnp.exp(t - c) for a, t, c in zip(dt_all, tot_all, acum_all)]
        cols_t = [_rows_to_cols(a, eye_b) for a in acum_all]
        cd_all = [jnp.exp(t) for t in tot_all]
        ch = [(k, d, q) for k in range(npar) for d, q in chains]
        hd = [(k, d, q, j) for k, d, q in ch for j in range(2)]
        cols = [slice(q * LANES, (q + 1) * LANES) for _, _, q in ch]
        xb = [xs[pl.ds(c0[k][d], CHUNK), cs_] for (k, d, _), cs_ in zip(ch, cols)]
        acum = [acum_all[k][hrow(d, q, j):hrow(d, q, j) + 1, :] for k, d, q, j in hd]
        dt = [dt_all[k][hrow(d, q, j):hrow(d, q, j) + 1, :] for k, d, q, j in hd]
        acum_b = [jnp.broadcast_to(cols_t[k][:, hrow(d, q, j):hrow(d, q, j) + 1], (CHUNK, CHUNK))
                  for k, d, q, j in hd]
        lmat = [jnp.where(incl[d], jnp.exp(jnp.where(incl[d], c - r, 0.0)), 0.0)
                for c, r, (_, d, _, _) in zip(acum_b, acum, hd)]
        m_mat = [(cb[(k, d)] * l * a).astype(BF16) for l, a, (k, d, _, _) in zip(lmat, dt, hd)]
        din_b = [jnp.exp(a) for a in acum_b]
        w_diag = [jnp.where(eye, w_all[k][hrow(d, q, j):hrow(d, q, j) + 1, :], 0.0).astype(BF16)
                  for k, d, q, j in hd]
        cd = [cd_all[k][hrow(d, q, j):hrow(d, q, j) + 1, :] for k, d, q, j in hd]
        x0 = [jnp.where(head0_lane, a, jnp.zeros_like(a)) for a in xb]
        x1 = [jnp.where(head0_lane, jnp.zeros_like(a), a) for a in xb]
        y = [_mm(m_mat[2 * n], x0[n]) + _mm(m_mat[2 * n + 1], x1[n]) for n in range(len(ch))]
        xw = [(_mm(w_diag[2 * n], x0[n]) + _mm(w_diag[2 * n + 1], x1[n])).astype(BF16) for n in range(len(ch))]
        s_in = [_tn(a, bm[(k, d)]) for a, (k, d, _) in zip(xw, ch)]
        state = [st[ci] for ci in range(len(chains))]
        nc = len(chains)
        for k in range(npar):
            for ci, (d, q) in enumerate(chains):
                n = k * nc + ci
                yk = y[n] + _nt(cm[(k, d)], state[ci].astype(BF16)) * jnp.where(
                    head0_lane, din_b[2 * n], din_b[2 * n + 1])
                state[ci] = state[ci] * jnp.where(head0_sub, cd[2 * n], cd[2 * n + 1]) + s_in[n]
                rows = pl.ds(c0[k][d], CHUNK)
                if not second:
                    yacc[rows, cols[n]] = yk
                else:
                    yt = yacc[rows, cols[n]] + yk + dvec_ref[:, cols[n]] * xb[n].astype(F32)
                    o_ref[rows, cols[n]] = (yt * _silu(z_ref[rows, cols[n]].astype(F32))).astype(o_ref.dtype)
        for ci in range(nc):
            st[ci] = state[ci]

    lax.fori_loop(0, half // npar, lambda i, c: (step(i, False), c)[1], 0)
    lax.fori_loop(half // npar, n_chunks // npar, lambda i, c: (step(i, True), c)[1], 0)

    if sout_ref is not None:
        for ci, (d, q) in enumerate(chains):
            s = st[ci]
            sout_ref[d, 2 * q] = s[:SSM_HEAD_DIM]
            sout_ref[d, 2 * q + 1] = s[SSM_HEAD_DIM:]


def _ssd(p, conv_w, conv_b, d_vec, a_log, dt_bias, s0, *, batch, t_len):
    n = batch * t_len
    has_ctx = s0 is not None
    assert (t_len // CHUNK) % 2 == 0
    n_groups = SSM_HEADS // SSD_GROUP_HEADS
    x_blocks = D_INNER // LANES
    big = dict(pipeline_mode=pl.Buffered(1)) if t_len * SSD_GROUP_WIDTH * 2 >= (2 << 20) else {}
    par = pl.BlockSpec((None, 2 * SSD_GROUP_HEADS, 1), lambda b, g: (g, 0, 0))
    state_spec = pl.BlockSpec((None, 2, SSD_GROUP_HEADS, SSM_HEAD_DIM, SSM_STATE), lambda b, g: (b, 0, g, 0, 0))
    in_specs = [pl.BlockSpec((t_len, SSD_GROUP_WIDTH), lambda b, g: (b, g), **big),
                pl.BlockSpec((t_len, LANES), lambda b, g: (b, x_blocks + g)),
                pl.BlockSpec((t_len, LANES), lambda b, g: (b, x_blocks + n_groups + g)),
                pl.BlockSpec((t_len, SSD_GROUP_WIDTH), lambda b, g: (b, g), **big),
                pl.BlockSpec((SSD_GROUP_HEADS, t_len), lambda b, g: (g, b)),
                pl.BlockSpec((SSD_GROUP_HEADS, t_len), lambda b, g: (n_groups + g, b)),
                pl.BlockSpec((3, SSD_GROUP_WIDTH), lambda b, g: (0, g)),
                pl.BlockSpec((3, LANES), lambda b, g: (0, x_blocks + g)),
                pl.BlockSpec((3, LANES), lambda b, g: (0, x_blocks + n_groups + g)),
                pl.BlockSpec((1, SSD_GROUP_WIDTH), lambda b, g: (0, g)),
                pl.BlockSpec((1, LANES), lambda b, g: (0, x_blocks + g)),
                pl.BlockSpec((1, LANES), lambda b, g: (0, x_blocks + n_groups + g)),
                pl.BlockSpec((1, SSD_GROUP_WIDTH), lambda b, g: (0, g)), par, par]
    args = [p["xbc"], p["xbc"], p["xbc"], p["cz"], p["small_t"], p["small_t"], conv_w, conv_w, conv_w,
            conv_b, conv_b, conv_b, d_vec, a_log, dt_bias]
    out_shape = [jax.ShapeDtypeStruct((n, D_INNER), BF16)]
    out_specs = [pl.BlockSpec((t_len, SSD_GROUP_WIDTH), lambda b, g: (b, g))]
    if has_ctx:
        in_specs.append(state_spec)
        args.append(s0)
    else:
        out_shape.append(jax.ShapeDtypeStruct((batch, 2, SSM_HEADS, SSM_HEAD_DIM, SSM_STATE), F32))
        out_specs.append(state_spec)
    scratch = [pltpu.VMEM((t_len, SSD_GROUP_WIDTH), BF16), pltpu.VMEM((t_len, LANES), BF16),
               pltpu.VMEM((t_len, LANES), BF16), pltpu.VMEM((t_len, SSD_GROUP_WIDTH), F32),
               pltpu.VMEM((2 * SSD_PAIRS, 2 * SSM_HEAD_DIM, SSM_STATE), F32)]
    outs = pl.pallas_call(
        functools.partial(_ssd_kernel, t_len=t_len, has_ctx=has_ctx),
        grid=(batch, n_groups),
        in_specs=in_specs, out_specs=out_specs, out_shape=out_shape, scratch_shapes=scratch,
        compiler_params=pltpu.CompilerParams(
            dimension_semantics=("arbitrary", "arbitrary"), vmem_limit_bytes=VMEM_LIMIT),
        name="ssd_ctx" if has_ctx else "ssd",
    )(*args)
    return outs[0], (None if has_ctx else outs[1])


KEY_TILE = 256


def _seg_rmsnorm(x, bd):
    x2 = x * x
    hi = x2.astype(BF16)
    lo = (x2 - hi.astype(F32)).astype(BF16)
    ms = _mm(hi, bd) + _mm(lo, bd)
    return x * lax.rsqrt(ms + EPS)


def _rope128(x, cos, sin_signed):
    lane = lax.broadcasted_iota(jnp.int32, x.shape, 1)
    first = (lane % 32) < 16
    swapped = jnp.where(first, pltpu.roll(x, LANES - 16, 1), pltpu.roll(x, 16, 1))
    return x * cos + swapped * sin_signed


def _attn_kernel(*refs, t_len, tq, past, has_ctx):
    if has_ctx:
        (q_ref, kv_ref, z_ref, qnw_ref, knw_ref, bd_ref, cosq_ref, sinq_ref, cosk_ref, sink_ref,
         ck_ref, cv_ref, o_ref, k_s, v_s) = refs
        kn_ref = None
    else:
        (q_ref, kv_ref, z_ref, qnw_ref, knw_ref, bd_ref, o_ref, kn_ref, k_s, v_s) = refs
    qi = pl.program_id(1)
    rt = _row_tile(t_len)
    n_tiles = t_len // rt

    def store_values(rows, v):
        low_v = lax.broadcasted_iota(jnp.int32, v.shape, 1) < HEAD_DIM
        v_s[0, rows, :] = jnp.where(low_v, v, 1.0).astype(BF16)
        v_s[1, rows, :] = jnp.where(low_v, 1.0, v).astype(BF16)

    @pl.when(qi == 0)
    def _():
        def prep(r, carry):
            start = pl.multiple_of(r * rt, rt)
            kv = kv_ref[pl.ds(start, rt), :]
            kn = _seg_rmsnorm(kv[:, :KV_WIDTH], bd_ref[0:KV_WIDTH, 0:KV_WIDTH]) * knw_ref[...]
            if kn_ref is not None:
                kn_ref[pl.ds(start, rt), :] = kn
            if has_ctx:
                kn = _rope128(kn, cosk_ref[pl.ds(start, rt), :], sink_ref[pl.ds(start, rt), :])
            k_s[pl.ds(start, rt), :] = kn.astype(BF16)
            store_values(pl.ds(start, rt), kv[:, KV_WIDTH:])
            return carry

        lax.fori_loop(0, n_tiles, prep, 0)
        if has_ctx:
            k_s[t_len:t_len + past, :] = ck_ref[...].astype(BF16)
            store_values(slice(t_len, t_len + past), cv_ref[...])

    q = _seg_rmsnorm(q_ref[...].astype(F32), bd_ref[...]) * qnw_ref[...]
    blocks = []
    for cbk in range(ATTN_WIDTH // LANES):
        blk = q[:, cbk * LANES:(cbk + 1) * LANES]
        if has_ctx:
            blk = _rope128(blk, cosq_ref[...], sinq_ref[...])
        blocks.append((blk * (HEAD_DIM ** -0.5)).astype(BF16))
    lane = lax.broadcasted_iota(jnp.int32, (tq, LANES), 1)
    low = lane < HEAD_DIM
    group = N_HEADS // KV_HEADS
    head_out = [None] * N_HEADS
    qgs = []
    for kvh in range(KV_HEADS):
        keep = low if kvh == 0 else jnp.logical_not(low)
        rows = []
        for g in range(group):
            head = kvh * group + g
            blk = blocks[head // 2]
            if head % 2 != kvh:
                blk = pltpu.roll(blk, HEAD_DIM, 1)
            rows.append(jnp.where(keep, blk, jnp.zeros_like(blk)))
        qgs.append(jnp.concatenate(rows, axis=0))

    n_kt = (t_len + past) // KEY_TILE
    keys = lambda j: slice(j * KEY_TILE, (j + 1) * KEY_TILE)
    score_tile = lambda kvh, j: _nt(qgs[kvh], k_s[keys(j), :])
    row_max = lambda tiles: jnp.max(functools.reduce(jnp.maximum, tiles), axis=-1, keepdims=True)
    s_tiles = [[score_tile(0, j) for j in range(n_kt)], []]
    m = [row_max(s_tiles[0]), None]
    p_tiles = [[], []]

    def exp_tile(kvh, j):
        p_tiles[kvh].append(jnp.exp((s_tiles[kvh][j] - m[kvh]).astype(BF16)))

    acc = [None, None]

    def pv_tile(kvh, j):
        part = _mm(p_tiles[kvh][j], v_s[kvh, keys(j), :])
        acc[kvh] = part if acc[kvh] is None else acc[kvh] + part

    for j in range(n_kt):
        s_tiles[1].append(score_tile(1, j))
        exp_tile(0, j)
    m[1] = row_max(s_tiles[1])
    for j in range(n_kt):
        pv_tile(0, j)
        exp_tile(1, j)
    for j in range(n_kt):
        pv_tile(1, j)
    for kvh in range(KV_HEADS):
        ones_lane = HEAD_DIM if kvh == 0 else 0
        o = acc[kvh] / acc[kvh][:, ones_lane:ones_lane + 1]
        for g in range(group):
            head = kvh * group + g
            og = o[g * tq:(g + 1) * tq, :]
            if head % 2 != kvh:
                og = pltpu.roll(og, HEAD_DIM, 1)
            head_out[head] = og
    outs = [jnp.where(low, head_out[2 * i], head_out[2 * i + 1]) for i in range(N_HEADS // 2)]
    o_all = jnp.concatenate(outs, axis=1)
    o_ref[...] = (o_all * _silu(z_ref[...].astype(F32))).astype(o_ref.dtype)


def _attention(p, qnw, knw, bd, rope_tabs, cache_k, cache_v, *, batch, t_len):
    n = batch * t_len
    has_ctx = cache_k is not None
    past = cache_k.shape[1] if has_ctx else 0
    tq = 128 if has_ctx else min(t_len, 256)
    nq = t_len // tq
    in_specs = [pl.BlockSpec((tq, ATTN_WIDTH), lambda b, i: (b * nq + i, 0)),
                pl.BlockSpec((t_len, 2 * KV_WIDTH), lambda b, i: (b, 0)),
                pl.BlockSpec((tq, ATTN_WIDTH), lambda b, i: (b * nq + i, 0)),
                pl.BlockSpec((1, ATTN_WIDTH), lambda b, i: (0, 0)),
                pl.BlockSpec((1, KV_WIDTH), lambda b, i: (0, 0)),
                pl.BlockSpec((ATTN_WIDTH, ATTN_WIDTH), lambda b, i: (0, 0))]
    args = [p["bq"], p["kv"], p["bz"], qnw, knw, bd]
    out_shape = [jax.ShapeDtypeStruct((n, ATTN_WIDTH), BF16)]
    out_specs = [pl.BlockSpec((tq, ATTN_WIDTH), lambda b, i: (b * nq + i, 0))]
    if has_ctx:
        cos, sin = rope_tabs
        in_specs += [pl.BlockSpec((tq, LANES), lambda b, i: (i, 0)),
                     pl.BlockSpec((tq, LANES), lambda b, i: (i, 0)),
                     pl.BlockSpec((t_len, LANES), lambda b, i: (0, 0)),
                     pl.BlockSpec((t_len, LANES), lambda b, i: (0, 0)),
                     pl.BlockSpec((None, past, KV_WIDTH), lambda b, i: (b, 0, 0)),
                     pl.BlockSpec((None, past, KV_WIDTH), lambda b, i: (b, 0, 0))]
        args += [cos, sin, cos, sin, cache_k, cache_v]
    else:
        out_shape.append(jax.ShapeDtypeStruct((n, KV_WIDTH), F32))
        out_specs.append(pl.BlockSpec((t_len, KV_WIDTH), lambda b, i: (b, 0)))
    scratch = [pltpu.VMEM((t_len + past, KV_WIDTH), BF16),
               pltpu.VMEM((KV_HEADS, t_len + past, KV_WIDTH), BF16)]
    outs = pl.pallas_call(
        functools.partial(_attn_kernel, t_len=t_len, tq=tq, past=past, has_ctx=has_ctx),
        grid=(batch, nq),
        in_specs=in_specs, out_specs=out_specs, out_shape=out_shape, scratch_shapes=scratch,
        compiler_params=pltpu.CompilerParams(
            dimension_semantics=("arbitrary", "arbitrary"), vmem_limit_bytes=VMEM_LIMIT),
        name="attn_ctx" if has_ctx else "attn",
    )(*args)
    return outs[0], (None if has_ctx else outs[1])


def _out_kernel(x_ref, oa_ref, ob_ref, yc_ref, g_ref, mod_ref, snw_ref, wa_ref, wb_ref, wc_ref, wo_ref,
                fnw_ref, o_ref, *, final):
    ba = _mm(oa_ref[...], wa_ref[...])
    bb = _mm(ob_ref[...], wb_ref[...])
    yc = yc_ref[...].astype(F32)
    yn = yc * lax.rsqrt(jnp.mean(yc * yc, axis=-1, keepdims=True) + EPS) * snw_ref[...]
    bc = _mm(yn.astype(BF16), wc_ref[...])
    g = g_ref[...].astype(F32)
    merged = (jax.nn.sigmoid(g[:, 0:D_MODEL]) * ba + jax.nn.sigmoid(g[:, D_MODEL:2 * D_MODEL]) * bb
              + jax.nn.sigmoid(g[:, 2 * D_MODEL:]) * bc)
    out = _mm(merged.astype(BF16), wo_ref[...])
    xn = x_ref[...] + mod_ref[:, 2 * D_MODEL:] * out
    if final:
        xn = xn * lax.rsqrt(jnp.mean(xn * xn, axis=-1, keepdims=True) + EPS) * fnw_ref[...]
    o_ref[...] = xn


def _out_proj(x, o_a, o_b, y_c, gates, mod, ssm_norm_w, wa, wb, wc, wo, final_norm_w, *,
              t_len, mod_base, mod_per_batch, final):
    n = x.shape[0]
    tm = 512 if t_len % 512 == 0 else 256
    assert n % tm == 0 and t_len % tm == 0

    def mod_idx(i):
        return (mod_base + (i * tm) // t_len if mod_per_batch else mod_base, 0, 0)

    tok = lambda w: pl.BlockSpec((tm, w), lambda i: (i, 0))
    full = lambda a, b: pl.BlockSpec((a, b), lambda i: (0, 0))
    return pl.pallas_call(
        functools.partial(_out_kernel, final=final),
        grid=(n // tm,),
        in_specs=[tok(D_MODEL), tok(GDN_WIDTH), tok(ATTN_WIDTH), tok(D_INNER), tok(3 * D_MODEL),
                  pl.BlockSpec((None, 1, 3 * D_MODEL), mod_idx), full(1, D_INNER),
                  full(GDN_WIDTH, D_MODEL), full(ATTN_WIDTH, D_MODEL), full(D_INNER, D_MODEL),
                  full(D_MODEL, D_MODEL), full(1, D_MODEL)],
        out_specs=tok(D_MODEL),
        out_shape=jax.ShapeDtypeStruct((n, D_MODEL), F32),
        compiler_params=pltpu.CompilerParams(
            dimension_semantics=("arbitrary",), vmem_limit_bytes=VMEM_LIMIT),
        name="out_proj",
    )(x, o_a, o_b, y_c, gates, mod, ssm_norm_w, wa, wb, wc, wo, final_norm_w)


def _rope_tables(t_len):
    rot = HEAD_DIM // 2
    pos = jnp.arange(t_len)
    freqs = ROPE_BASE ** (-jnp.arange(rot // 2, dtype=F32) / (rot // 2))
    ang_row = (pos // GRID_W).astype(F32)[:, None] * freqs
    ang_col = (pos % GRID_W).astype(F32)[:, None] * freqs
    ang = jnp.concatenate([ang_row, ang_row, ang_col, ang_col], axis=1)
    sign = jnp.tile(jnp.concatenate([-jnp.ones((rot // 2,), F32), jnp.ones((rot // 2,), F32)]), 2)
    cos = jnp.tile(jnp.cos(ang), (1, 2))
    sin = jnp.tile(jnp.sin(ang) * sign, (1, 2))
    return cos, sin


def _layer_params(l, w_in, gdn_conv_w, gdn_a_log, gdn_dt_bias, gdn_norm_w, attn_q_norm, attn_k_norm,
                  ssm_conv_w, ssm_conv_b, ssm_a_log, ssm_dt_bias, ssm_d, ssm_norm_w,
                  w_branch_a, w_branch_b, w_branch_c, w_out):
    w = w_in[l]
    sl = lambda name: w[:, _SRC[name][0]:_SRC[name][0] + _SRC[name][1]]
    w_main = jnp.concatenate([sl(k) for k in _MAIN], axis=1).astype(BF16)
    beta, alpha = sl("beta"), sl("alpha")
    hd = GDN_HEADS
    w_small_t = jnp.concatenate([sl("dt"), beta[:, :hd], alpha[:, :hd], beta[:, hd:], alpha[:, hd:]],
                                axis=1).T.astype(BF16)
    zeros = jnp.zeros((hd,), F32)

    def gdn_col(a):
        return jnp.concatenate([zeros, a[0], zeros, a[1]]).reshape(4 * hd, 1)

    def ssd_col(a):
        g = a.reshape(2, SSM_HEADS // SSD_GROUP_HEADS, SSD_GROUP_HEADS)
        return jnp.concatenate([g[0], g[1]], axis=1)[..., None]

    return dict(
        w_main=w_main, w_small_t=w_small_t,
        gdn_conv_w=gdn_conv_w[l], gdn_a_log=gdn_col(gdn_a_log[l]), gdn_dt_bias=gdn_col(gdn_dt_bias[l]),
        gdn_norm_w=gdn_norm_w[l].reshape(1, LANES),
        qnw=jnp.tile(attn_q_norm[l], N_HEADS).reshape(1, ATTN_WIDTH),
        knw=jnp.tile(attn_k_norm[l], KV_HEADS).reshape(1, KV_WIDTH),
        ssm_conv_w=ssm_conv_w[l], ssm_conv_b=ssm_conv_b[l].reshape(1, SSM_XBC),
        ssm_a_log=ssd_col(ssm_a_log[l]), ssm_dt_bias=ssd_col(ssm_dt_bias[l]),
        ssm_d=jnp.repeat(ssm_d[l], SSM_HEAD_DIM).reshape(1, D_INNER),
        ssm_norm_w=ssm_norm_w[l].reshape(1, D_INNER),
        wa=w_branch_a[l].astype(BF16), wb=w_branch_b[l].astype(BF16),
        wc=w_branch_c[l].astype(BF16), wo=w_out[l].astype(BF16))


def _stream_layer(x, lp, mod_l, norm_w_l, final_norm_w, bd, rope_tabs, ctx, *,
                  batch, t_len, mod_base, mod_per_batch, final):
    p = _in_proj(x, norm_w_l, mod_l, lp["w_main"], lp["w_small_t"],
                 t_len=t_len, mod_base=mod_base, mod_per_batch=mod_per_batch)
    ck, cv, sg, ss = ctx if ctx is not None else (None, None, None, None)
    o_a, new_sg = _gdn(p, lp["gdn_conv_w"], lp["gdn_norm_w"], lp["gdn_a_log"], lp["gdn_dt_bias"], sg,
                       batch=batch, t_len=t_len)
    y_c, new_ss = _ssd(p, lp["ssm_conv_w"], lp["ssm_conv_b"], lp["ssm_d"], lp["ssm_a_log"],
                       lp["ssm_dt_bias"], ss, batch=batch, t_len=t_len)
    o_b, k_norm = _attention(p, lp["qnw"], lp["knw"], bd, rope_tabs, ck, cv, batch=batch, t_len=t_len)
    x_new = _out_proj(x, o_a, o_b, y_c, p["gates"], mod_l, lp["ssm_norm_w"], lp["wa"], lp["wb"],
                      lp["wc"], lp["wo"], final_norm_w, t_len=t_len, mod_base=mod_base,
                      mod_per_batch=mod_per_batch, final=final)
    return x_new, (k_norm, p["kv"], new_sg, new_ss)


def kernel(x_prompt, x_sample, cache_k, cache_v, state_gdn, state_ssm, c, c_ctx, norm_w, w_mod, b_mod, w_in, gdn_conv_w, gdn_a_log, gdn_dt_bias, gdn_norm_w, attn_q_norm, attn_k_norm, ssm_conv_w, ssm_conv_b, ssm_a_log, ssm_dt_bias, ssm_d, ssm_norm_w, w_branch_a, w_branch_b, w_branch_c, w_out, final_norm_w):
    batch, seq, d_model = x_prompt.shape
    dec_batch, dec_seq, _ = x_sample.shape
    depth = w_in.shape[0]
    past = cache_k.shape[2]
    assert d_model == D_MODEL and seq % CHUNK == 0 and dec_seq % CHUNK == 0

    rows = -(-(1 + dec_batch) // 8) * 8
    cond = jnp.zeros((rows, D_MODEL), F32).at[0].set(c_ctx).at[1:1 + dec_batch].set(c)
    mod = _modulation(cond, w_mod, b_mod).reshape(depth, rows, 1, 3 * D_MODEL)

    seg = jnp.arange(ATTN_WIDTH) // HEAD_DIM
    bd = jnp.where(seg[:, None] == seg[None, :], 1.0 / HEAD_DIM, 0.0).astype(BF16)
    rope_tabs = _rope_tables(dec_seq)
    fnw = final_norm_w.reshape(1, D_MODEL)

    xp = x_prompt.reshape(batch * seq, D_MODEL)
    xs = x_sample.reshape(dec_batch * dec_seq, D_MODEL)
    new_k, new_v, new_gdn, new_ssm = [], [], [], []
    for l in range(depth):
        lp = _layer_params(l, w_in, gdn_conv_w, gdn_a_log, gdn_dt_bias, gdn_norm_w, attn_q_norm,
                           attn_k_norm, ssm_conv_w, ssm_conv_b, ssm_a_log, ssm_dt_bias, ssm_d,
                           ssm_norm_w, w_branch_a, w_branch_b, w_branch_c, w_out)
        nw = norm_w[l].reshape(1, D_MODEL)
        final = l == depth - 1
        xp, (k_norm, kv, s_g, s_s) = _stream_layer(
            xp, lp, mod[l], nw, fnw, bd, None, None,
            batch=batch, t_len=seq, mod_base=0, mod_per_batch=False, final=final)
        new_k.append(k_norm.reshape(batch, seq, KV_HEADS, HEAD_DIM))
        new_v.append(kv[:, KV_WIDTH:].reshape(batch, seq, KV_HEADS, HEAD_DIM))
        new_gdn.append(s_g)
        new_ssm.append(s_s)
        ctx = (cache_k[:, l].reshape(dec_batch, past, KV_WIDTH),
               cache_v[:, l].reshape(dec_batch, past, KV_WIDTH),
               state_gdn[:, l], state_ssm[:, l])
        xs, _ = _stream_layer(
            xs, lp, mod[l], nw, fnw, bd, rope_tabs, ctx,
            batch=dec_batch, t_len=dec_seq, mod_base=1, mod_per_batch=True, final=final)
    return (xp.reshape(batch, seq, D_MODEL), xs.reshape(dec_batch, dec_seq, D_MODEL),
            jnp.stack(new_k, axis=1), jnp.stack(new_v, axis=1),
            jnp.stack(new_gdn, axis=1), jnp.stack(new_ssm, axis=1))
```

```python
import functools
import math

import jax
import jax.numpy as jnp
from jax import lax
from jax.experimental import pallas as pl
from jax.experimental.pallas import tpu as pltpu

F32 = jnp.float32
BF16 = jnp.bfloat16

D_MODEL = 1024
EPS = 1e-6
GDN_HEADS = 4
GDN_DK = 128
GDN_WIDTH = 512
N_HEADS = 8
KV_HEADS = 2
HEAD_DIM = 64
ATTN_WIDTH = 512
KV_WIDTH = 128
GRID_W = 64
ROPE_BASE = 10000.0
SSM_HEADS = 16
SSM_HEAD_DIM = 64
D_INNER = 1024
SSM_STATE = 128
SSM_XBC = 1536
CHUNK = 128
LANES = 128
SMALL_ROWS = 48
VMEM_LIMIT = 56 * 1024 * 1024

_SRC = dict(qkv=(0, 1536), az=(1536, 512), beta=(2048, 8), alpha=(2056, 8), bq=(2064, 512),
            kv=(2576, 256), bz=(2832, 512), xbc=(3344, 1536), cz=(4880, 1024), dt=(5904, 32),
            gates=(5936, 3072))
_MAIN = ("qkv", "az", "bq", "kv", "bz", "xbc", "cz", "gates")
_MAIN_OFF = {}
_o = 0
for _n in _MAIN:
    _MAIN_OFF[_n] = _o
    _o += _SRC[_n][1]
MAIN_WIDTH = _o


def _nt(a, b):
    return lax.dot_general(a, b, (((1,), (1,)), ((), ())), preferred_element_type=F32)


def _tn(a, b):
    return lax.dot_general(a, b, (((0,), (0,)), ((), ())), preferred_element_type=F32)


def _mm(a, b):
    return jnp.dot(a, b, preferred_element_type=F32)


def _silu(x):
    return x * jax.nn.sigmoid(x)


def _softplus(x):
    return jnp.maximum(x, 0.0) + jnp.log1p(jnp.exp(-jnp.abs(x)))


def _split3(x):
    hi = x.astype(BF16)
    r1 = x - hi.astype(F32)
    mid = r1.astype(BF16)
    lo = (r1 - mid.astype(F32)).astype(BF16)
    return hi, mid, lo


def _cumsum_rows(x, tri):
    hi, mid, lo = _split3(x)
    return _mm(hi, tri) + _mm(mid, tri) + _mm(lo, tri)


def _rows_to_cols(x, eye_b):
    hi, mid, lo = _split3(x)
    return _nt(eye_b, hi) + _nt(eye_b, mid) + _nt(eye_b, lo)


def _conv_tile(src_ref, r, rt, n_tiles, w, bias, off):
    t_len = src_ref.shape[0]
    cols = slice(off, off + LANES)
    start = pl.multiple_of(r * rt, rt)
    cur = src_ref[pl.ds(start, rt), cols].astype(F32)
    pstart = pl.multiple_of(jnp.maximum(start - 16, 0), 16)
    nstart = pl.multiple_of(jnp.minimum(start + rt, t_len - 16), 16)
    prev_row = src_ref[pl.ds(pstart, 16), cols][15:16, :].astype(F32)
    next_row = src_ref[pl.ds(nstart, 16), cols][0:1, :].astype(F32)
    pos = lax.rem(r, n_tiles)
    prev_row = jnp.where(pos > 0, prev_row, 0.0)
    next_row = jnp.where(pos < n_tiles - 1, next_row, 0.0)
    ri = lax.broadcasted_iota(jnp.int32, (rt, LANES), 0)
    x_prev = jnp.where(ri == 0, prev_row, pltpu.roll(cur, 1, 0))
    x_next = jnp.where(ri == rt - 1, next_row, pltpu.roll(cur, rt - 1, 0))
    y = w[0:1, :] * x_prev + w[1:2, :] * cur + w[2:3, :] * x_next
    if bias is not None:
        y = y + bias
    return y


def _row_tile(t_len):
    return min(t_len, 256)


SCAN_STEP_ROWS = 4096
GDN_SLOTS = 2
SSD_SLOTS = 4


def _positions_per_body(half, max_slots):
    return max(n for n in (1, 2, 4) if half % n == 0 and n <= max_slots)


def _seqs_per_step(batch, t_len, max_slots):
    npar = _positions_per_body(t_len // CHUNK // 2, max_slots)
    return max(n for n in (1, 2, 4)
               if batch % n == 0 and n * npar <= max_slots and n * t_len <= SCAN_STEP_ROWS)


def _mod_kernel(c_ref, w_ref, b_ref, o_ref):
    s = _silu(c_ref[...])
    w = w_ref[...]
    s_hi = s.astype(BF16)
    s_lo = (s - s_hi.astype(F32)).astype(BF16)
    w_hi = w.astype(BF16)
    w_lo = (w - w_hi.astype(F32)).astype(BF16)
    o_ref[...] = _mm(s_hi, w_hi) + _mm(s_lo, w_hi) + _mm(s_hi, w_lo) + b_ref[...]


def _modulation(cond, w_mod, b_mod):
    depth = w_mod.shape[0]
    rows = cond.shape[0]
    tn = 1024
    return pl.pallas_call(
        _mod_kernel,
        grid=(depth, 3 * D_MODEL // tn),
        in_specs=[
            pl.BlockSpec((rows, D_MODEL), lambda l, j: (0, 0)),
            pl.BlockSpec((None, D_MODEL, tn), lambda l, j: (l, 0, j)),
            pl.BlockSpec((None, 1, tn), lambda l, j: (l, 0, j)),
        ],
        out_specs=pl.BlockSpec((None, rows, tn), lambda l, j: (l, 0, j)),
        out_shape=jax.ShapeDtypeStruct((depth, rows, 3 * D_MODEL), F32),
        name="modulation",
    )(cond, w_mod, b_mod.reshape(depth, 1, 3 * D_MODEL))


def _in_kernel(x_ref, nw_ref, mod_ref, w_ref, wst_ref,
               o_qkv, o_az, o_bq, o_kv, o_bz, o_xbc, o_cz, o_gates, o_small):
    x = x_ref[...]
    y = x * lax.rsqrt(jnp.mean(x * x, axis=-1, keepdims=True) + EPS) * nw_ref[...]
    shift = mod_ref[:, 0:D_MODEL]
    scale = mod_ref[:, D_MODEL:2 * D_MODEL]
    h = (y * (1.0 + scale) + shift).astype(BF16)
    outs = dict(qkv=o_qkv, az=o_az, bq=o_bq, kv=o_kv, bz=o_bz, xbc=o_xbc, cz=o_cz, gates=o_gates)
    for name in _MAIN:
        off, width = _MAIN_OFF[name], _SRC[name][1]
        outs[name][...] = _mm(h, w_ref[:, off:off + width]).astype(outs[name].dtype)
    o_small[...] = _nt(wst_ref[...], h)


def _in_proj(x, norm_w, mod, w_main, w_small_t, *, t_len, mod_base, mod_per_batch):
    n = x.shape[0]
    tm = 256
    assert n % tm == 0 and t_len % tm == 0

    def mod_idx(i):
        return (mod_base + (i * tm) // t_len if mod_per_batch else mod_base, 0, 0)

    widths = {k: _SRC[k][1] for k in _MAIN}
    dtypes = {k: (F32 if k == "kv" else BF16) for k in _MAIN}
    out_shape = [jax.ShapeDtypeStruct((n, widths[k]), dtypes[k]) for k in _MAIN]
    out_shape.append(jax.ShapeDtypeStruct((SMALL_ROWS, n), F32))
    out_specs = [pl.BlockSpec((tm, widths[k]), lambda i: (i, 0)) for k in _MAIN]
    out_specs.append(pl.BlockSpec((SMALL_ROWS, tm), lambda i: (0, i)))
    outs = pl.pallas_call(
        _in_kernel,
        grid=(n // tm,),
        in_specs=[
            pl.BlockSpec((tm, D_MODEL), lambda i: (i, 0)),
            pl.BlockSpec((1, D_MODEL), lambda i: (0, 0)),
            pl.BlockSpec((None, 1, 3 * D_MODEL), mod_idx),
            pl.BlockSpec((D_MODEL, MAIN_WIDTH), lambda i: (0, 0), pipeline_mode=pl.Buffered(1)),
            pl.BlockSpec((SMALL_ROWS, D_MODEL), lambda i: (0, 0)),
        ],
        out_specs=out_specs,
        out_shape=out_shape,
        compiler_params=pltpu.CompilerParams(
            dimension_semantics=("arbitrary",), vmem_limit_bytes=VMEM_LIMIT),
        name="in_proj",
    )(x, norm_w, mod, w_main, w_small_t)
    res = dict(zip(_MAIN, outs[:-1]))
    res["small_t"] = outs[-1]
    return res


INV_BASE = 8


def _tri_inverse(nms, dirs, sub, lane, eye_f):
    def same_block(s):
        sh = int(math.log2(s))
        return (sub >> sh) == (lane >> sh)

    base = same_block(INV_BASE)
    n8 = [jnp.where(base, nm, 0.0) for nm in nms]
    p = [n.astype(BF16) for n in n8]
    y = [eye_f - n for n in n8]
    p2 = [_mm(a, a).astype(BF16) for a in p]
    both = [_mm(a, jnp.concatenate([a, b.astype(BF16)], axis=1)) for a, b in zip(p2, y)]
    y = [b + c[:, CHUNK:] for b, c in zip(y, both)]
    p4 = [c[:, :CHUNK].astype(BF16) for c in both]
    y = [b + _mm(a, b.astype(BF16)) for a, b in zip(p4, y)]
    s = INV_BASE
    while s < CHUNK:
        sel = jnp.logical_and(same_block(2 * s), jnp.logical_not(same_block(s)))
        off = [jnp.where(sel, nm, 0.0).astype(BF16) for nm in nms]
        yb = [b.astype(BF16) for b in y]
        starts = [[2 * s * b + (s if d == 0 else 0) for b in range(CHUNK // (2 * s))] for d in dirs]
        rows = [jnp.concatenate([b[a:a + s] for a in st], axis=0).astype(BF16) for b, st in zip(y, starts)]
        t = [_mm(r, o).astype(BF16) for r, o in zip(rows, off)]
        upd = [_mm(a, c) for a, c in zip(t, yb)]
        new_y = []
        for b, u, st in zip(y, upd, starts):
            parts = []
            for n, a in enumerate(st):
                keep = b[a - s:a] if a % (2 * s) else b[a + s:a + 2 * s]
                changed = b[a:a + s] - u[n * s:(n + 1) * s]
                parts += [keep, changed] if a % (2 * s) else [changed, keep]
            new_y.append(jnp.concatenate(parts, axis=0))
        y = new_y
        s *= 2
    return y


def _gdn_kernel(*refs, t_len, nseq, has_ctx, n_prev):
    (q_ref, k_ref, v_ref, z_ref, gtf_ref, gtb_ref, wq_ref, wk_ref, wv_ref, nw_ref, alog_ref, bias_ref) = refs[:12]
    rest = refs[12:]
    s0_ref = prev_ref = sout_ref = None
    if has_ctx:
        s0_ref, o_ref, qs, ks, vs, oacc, st = rest
    else:
        if n_prev:
            prev_ref, rest = rest[0], rest[1:]
        o_ref, sout_ref, qs, ks, vs, oacc, st = rest
    rt = _row_tile(t_len)
    n_tiles = t_len // rt
    n_chunks = t_len // CHUNK
    chains = [(d, h) for d in range(2) for h in range(GDN_HEADS)]

    def prep(r, carry):
        start = pl.multiple_of(r * rt, rt)
        for h in range(GDN_HEADS):
            off = h * LANES
            q = _silu(_conv_tile(q_ref, r, rt, n_tiles, wq_ref[:, off:off + LANES], None, off))
            k = _silu(_conv_tile(k_ref, r, rt, n_tiles, wk_ref[:, off:off + LANES], None, off))
            v = _silu(_conv_tile(v_ref, r, rt, n_tiles, wv_ref[:, off:off + LANES], None, off))
            q = q * lax.rsqrt(jnp.sum(q * q, axis=-1, keepdims=True) + EPS) * (GDN_DK ** -0.5)
            k = k * lax.rsqrt(jnp.sum(k * k, axis=-1, keepdims=True) + EPS)
            qs[pl.ds(start, rt), off:off + LANES] = q.astype(BF16)
            ks[pl.ds(start, rt), off:off + LANES] = k.astype(BF16)
            vs[pl.ds(start, rt), off:off + LANES] = v.astype(BF16)
        return carry

    lax.fori_loop(0, nseq * n_tiles, prep, 0)

    nc = len(chains)
    for sq in range(nseq):
        for ci, (d, h) in enumerate(chains):
            st[sq * nc + ci] = s0_ref[sq, d, h] if has_ctx else jnp.zeros((GDN_DK, CHUNK), F32)

    sub = lax.broadcasted_iota(jnp.int32, (CHUNK, CHUNK), 0)
    lane = lax.broadcasted_iota(jnp.int32, (CHUNK, CHUNK), 1)
    eye = sub == lane
    eye_f = eye.astype(F32)
    incl = [lane <= sub, lane >= sub]
    strict = [lane < sub, lane > sub]
    eye_b = eye.astype(BF16)
    tri = [(sub <= lane).astype(BF16), (sub >= lane).astype(BF16)]
    grow = lax.broadcasted_iota(jnp.int32, (4 * GDN_HEADS, CHUNK), 0)
    is_beta = ((grow >> int(math.log2(GDN_HEADS))) & 1) == 0
    is_fwd = grow < 2 * GDN_HEADS
    neg_a = jnp.where(is_beta, 0.0, -jnp.exp(alog_ref[...]))
    beta_row = lambda d, h: 2 * d * GDN_HEADS + h
    decay_row = lambda d, h: (2 * d + 1) * GDN_HEADS + h

    half = n_chunks // 2
    npar = _positions_per_body(half, GDN_SLOTS)

    def step(i, second):
        slots = [(sq, k) for sq in range(nseq) for k in range(npar)]
        c0 = [[pl.multiple_of(sq * t_len + (i * npar + k) * CHUNK, CHUNK),
               pl.multiple_of(sq * t_len + (n_chunks - 1 - (i * npar + k)) * CHUNK, CHUNK)] for sq, k in slots]
        raw = [jnp.concatenate([gtf_ref[:, pl.ds(c[0], CHUNK)], gtb_ref[:, pl.ds(c[1], CHUNK)]], axis=0)
               for c in c0]
        g_all = [neg_a * _softplus(r + bias_ref[...]) for r in raw]
        gc_all = [jnp.where(is_fwd, _cumsum_rows(g, tri[0]), _cumsum_rows(g, tri[1])) for g in g_all]
        comb = [jnp.where(is_beta, jax.nn.sigmoid(r), g) for r, g in zip(raw, gc_all)]
        comb_t = [_rows_to_cols(c, eye_b) for c in comb]
        ch = [(k, d, h) for k in range(len(slots)) for d, h in chains]
        cols = [slice(h * LANES, (h + 1) * LANES) for _, _, h in ch]
        qb = [qs[pl.ds(c0[k][d], CHUNK), cs] for (k, d, _), cs in zip(ch, cols)]
        kb = [ks[pl.ds(c0[k][d], CHUNK), cs] for (k, d, _), cs in zip(ch, cols)]
        vf = [vs[pl.ds(c0[k][d], CHUNK), cs].astype(F32) for (k, d, _), cs in zip(ch, cols)]
        qf = [a.astype(F32) for a in qb]
        kf = [a.astype(F32) for a in kb]
        gc = [comb[k][decay_row(d, h):decay_row(d, h) + 1, :] for k, d, h in ch]
        beta_col = [jnp.broadcast_to(comb_t[k][:, beta_row(d, h):beta_row(d, h) + 1], (CHUNK, CHUNK))
                    for k, d, h in ch]
        gc_col = [jnp.broadcast_to(comb_t[k][:, decay_row(d, h):decay_row(d, h) + 1], (CHUNK, CHUNK))
                  for k, d, h in ch]
        kk = [_nt(a, a) for a in kb]
        qk = [_nt(a, b) for a, b in zip(qb, kb)]
        dec = [jnp.where(incl[d], jnp.exp(jnp.where(incl[d], c - r, 0.0)), 0.0)
               for c, r, (_, d, _) in zip(gc_col, gc, ch)]
        nm = [jnp.where(strict[d], a * b * c, 0.0) for a, b, c, (_, d, _) in zip(kk, dec, beta_col, ch)]
        t_inv = _tri_inverse(nm, [d for _, d, _ in ch], sub, lane, eye_f)
        eg = [jnp.exp(a) for a in gc_col]
        rhs = [jnp.concatenate([a * b, c * (b * e)], axis=1).astype(BF16)
               for a, b, c, e in zip(vf, beta_col, kf, eg)]
        sol = [_mm(a.astype(BF16), b) for a, b in zip(t_inv, rhs)]
        tot = [a[:, CHUNK - 1:CHUNK] if d == 0 else a[:, 0:1] for a, (_, d, _) in zip(gc, ch)]
        q_dec = [(a * e).astype(BF16) for a, e in zip(qf, eg)]
        k_dec = [(a * jnp.exp(t - c)).astype(BF16) for a, t, c in zip(kf, tot, gc_col)]
        intra = [(a * b).astype(BF16) for a, b in zip(qk, dec)]
        state = [st[n] for n in range(nseq * nc)]
        for k in range(npar):
            idx = [(sq * npar + k) * nc + ci for sq in range(nseq) for ci in range(nc)]
            sidx = [sq * nc + ci for sq in range(nseq) for ci in range(nc)]
            sb = [state[n].astype(BF16) for n in sidx]
            v_new = [sol[m][:, :CHUNK] - _mm(sol[m][:, CHUNK:].astype(BF16), b) for m, b in zip(idx, sb)]
            vb = [a.astype(BF16) for a in v_new]
            o = [_mm(q_dec[m], b) + _mm(intra[m], e) for m, b, e in zip(idx, sb, vb)]
            for m, n, e, ov in zip(idx, sidx, vb, o):
                state[n] = state[n] * jnp.exp(tot[m]) + _tn(k_dec[m], e)
                slot, d, h = ch[m]
                rows = pl.ds(c0[slot][d], CHUNK)
                cs = slice(h * LANES, (h + 1) * LANES)
                if not second:
                    oacc[rows, cs] = ov
                else:
                    ot = oacc[rows, cs] + ov
                    ot = ot * lax.rsqrt(jnp.mean(ot * ot, axis=-1, keepdims=True) + EPS) * nw_ref[...]
                    o_ref[rows, cs] = (ot * _silu(z_ref[rows, cs].astype(F32))).astype(o_ref.dtype)
        for n in range(nseq * nc):
            st[n] = state[n]

    lax.fori_loop(0, half // npar, lambda i, c: (step(i, False), c)[1], 0)
    lax.fori_loop(half // npar, n_chunks // npar, lambda i, c: (step(i, True), c)[1], 0)

    if sout_ref is not None:
        if n_prev:
            sout_ref[:, 0:n_prev] = prev_ref[...]
        for sq in range(nseq):
            for ci, (d, h) in enumerate(chains):
                sout_ref[sq, n_prev, d, h] = st[sq * nc + ci]


def _stacked_state_specs(prev, nseq, tail, index_map):
    n_prev = 0 if prev is None else prev.shape[1]
    in_spec = None if prev is None else pl.BlockSpec((nseq, n_prev) + tail, index_map)
    return n_prev, in_spec, pl.BlockSpec((nseq, n_prev + 1) + tail, index_map)


def _gdn(p, conv_w, norm_w, a_log, dt_bias, s0, prev, *, batch, t_len):
    n = batch * t_len
    has_ctx = s0 is not None
    assert (t_len // CHUNK) % 2 == 0
    nseq = _seqs_per_step(batch, t_len, GDN_SLOTS)
    rows = nseq * t_len
    big = dict(pipeline_mode=pl.Buffered(1)) if rows * GDN_WIDTH * 2 >= (2 << 20) else {}
    seq = lambda j: pl.BlockSpec((rows, GDN_WIDTH), lambda b, j=j: (b, j), **big)
    cw = lambda j: pl.BlockSpec((3, GDN_WIDTH), lambda b, j=j: (0, j))
    gate_rows = 2 * GDN_HEADS
    first = 2 * SSM_HEADS // gate_rows
    par = pl.BlockSpec((2 * gate_rows, 1), lambda b: (0, 0))
    state_spec = pl.BlockSpec((nseq, 2, GDN_HEADS, GDN_DK, CHUNK), lambda b: (b, 0, 0, 0, 0))
    in_specs = [seq(0), seq(1), seq(2), seq(0),
                pl.BlockSpec((gate_rows, rows), lambda b: (first, b)),
                pl.BlockSpec((gate_rows, rows), lambda b: (first + 1, b)),
                cw(0), cw(1), cw(2),
                pl.BlockSpec((1, LANES), lambda b: (0, 0)), par, par]
    args = [p["qkv"], p["qkv"], p["qkv"], p["az"], p["small_t"], p["small_t"], conv_w, conv_w, conv_w,
            norm_w, a_log, dt_bias]
    out_shape = [jax.ShapeDtypeStruct((n, GDN_WIDTH), BF16)]
    out_specs = [pl.BlockSpec((rows, GDN_WIDTH), lambda b: (b, 0))]
    n_prev = 0
    if has_ctx:
        in_specs.append(state_spec)
        args.append(s0)
    else:
        tail = (2, GDN_HEADS, GDN_DK, CHUNK)
        n_prev, prev_spec, out_spec = _stacked_state_specs(prev, nseq, tail, lambda b: (b, 0, 0, 0, 0, 0))
        if n_prev:
            in_specs.append(prev_spec)
            args.append(prev)
        out_shape.append(jax.ShapeDtypeStruct((batch, n_prev + 1) + tail, F32))
        out_specs.append(out_spec)
    scratch = [pltpu.VMEM((rows, GDN_WIDTH), BF16) for _ in range(3)]
    scratch += [pltpu.VMEM((rows, GDN_WIDTH), F32), pltpu.VMEM((nseq * 2 * GDN_HEADS, GDN_DK, CHUNK), F32)]
    outs = pl.pallas_call(
        functools.partial(_gdn_kernel, t_len=t_len, nseq=nseq, has_ctx=has_ctx, n_prev=n_prev),
        grid=(batch // nseq,),
        in_specs=in_specs, out_specs=out_specs, out_shape=out_shape, scratch_shapes=scratch,
        compiler_params=pltpu.CompilerParams(
            dimension_semantics=("arbitrary",), vmem_limit_bytes=VMEM_LIMIT),
        name="gdn_ctx" if has_ctx else "gdn",
    )(*args)
    return outs[0], (None if has_ctx else outs[1])


SSD_GROUP_HEADS = 8
SSD_PAIRS = SSD_GROUP_HEADS // 2
SSD_GROUP_WIDTH = SSD_GROUP_HEADS * SSM_HEAD_DIM


def _ssd_kernel(*refs, t_len, nseq, has_ctx, n_prev):
    (x_ref, b_ref, c_ref, z_ref, dtf_ref, dtb_ref, wx_ref, wb_ref, wc_ref, bx_ref, bb_ref, bc_ref,
     dvec_ref, alog_ref, bias_ref) = refs[:15]
    rest = refs[15:]
    s0_ref = prev_ref = sout_ref = None
    if has_ctx:
        s0_ref, o_ref, xs, bs, cs, yacc, st = rest
    else:
        if n_prev:
            prev_ref, rest = rest[0], rest[1:]
        o_ref, sout_ref, xs, bs, cs, yacc, st = rest
    rt = _row_tile(t_len)
    n_tiles = t_len // rt
    n_chunks = t_len // CHUNK
    chains = [(d, q) for d in range(2) for q in range(SSD_PAIRS)]

    def prep(r, carry):
        start = pl.multiple_of(r * rt, rt)
        for q in range(SSD_PAIRS):
            off = q * LANES
            xs[pl.ds(start, rt), off:off + LANES] = _silu(_conv_tile(
                x_ref, r, rt, n_tiles, wx_ref[:, off:off + LANES], bx_ref[:, off:off + LANES], off)).astype(BF16)
        bs[pl.ds(start, rt), :] = _silu(_conv_tile(b_ref, r, rt, n_tiles, wb_ref[...], bb_ref[...], 0)).astype(BF16)
        cs[pl.ds(start, rt), :] = _silu(_conv_tile(c_ref, r, rt, n_tiles, wc_ref[...], bc_ref[...], 0)).astype(BF16)
        return carry

    lax.fori_loop(0, nseq * n_tiles, prep, 0)

    nc = len(chains)
    for sq in range(nseq):
        for ci, (d, q) in enumerate(chains):
            if has_ctx:
                st[sq * nc + ci] = jnp.concatenate([s0_ref[sq, d, 2 * q], s0_ref[sq, d, 2 * q + 1]], axis=0)
            else:
                st[sq * nc + ci] = jnp.zeros((2 * SSM_HEAD_DIM, SSM_STATE), F32)

    sub = lax.broadcasted_iota(jnp.int32, (CHUNK, CHUNK), 0)
    lane = lax.broadcasted_iota(jnp.int32, (CHUNK, CHUNK), 1)
    eye = sub == lane
    incl = [lane <= sub, lane >= sub]
    head0_lane = lane < SSM_HEAD_DIM
    head0_sub = sub < SSM_HEAD_DIM
    eye_b = eye.astype(BF16)
    tri = [(sub <= lane).astype(BF16), (sub >= lane).astype(BF16)]
    is_fwd = lax.broadcasted_iota(jnp.int32, (2 * SSD_GROUP_HEADS, CHUNK), 0) < SSD_GROUP_HEADS
    lane_r = lax.broadcasted_iota(jnp.int32, (2 * SSD_GROUP_HEADS, CHUNK), 1)
    neg_a = -jnp.exp(alog_ref[...])
    hrow = lambda d, q, j: d * SSD_GROUP_HEADS + 2 * q + j

    half = n_chunks // 2
    npar = _positions_per_body(half, SSD_SLOTS)

    def step(i, second):
        slots = [(sq, k) for sq in range(nseq) for k in range(npar)]
        c0 = [[pl.multiple_of(sq * t_len + (i * npar + k) * CHUNK, CHUNK),
               pl.multiple_of(sq * t_len + (n_chunks - 1 - (i * npar + k)) * CHUNK, CHUNK)] for sq, k in slots]
        kd = [(k, d) for k in range(len(slots)) for d in range(2)]
        bm = {(k, d): bs[pl.ds(c0[k][d], CHUNK), :] for k, d in kd}
        cm = {(k, d): cs[pl.ds(c0[k][d], CHUNK), :] for k, d in kd}
        cb = {key: _nt(cm[key], bm[key]) for key in kd}
        raw = [jnp.concatenate([dtf_ref[:, pl.ds(c[0], CHUNK)], dtb_ref[:, pl.ds(c[1], CHUNK)]], axis=0)
               for c in c0]
        dt_all = [_softplus(r + bias_ref[...]) for r in raw]
        da_all = [a * neg_a for a in dt_all]
        acum_all = [jnp.where(is_fwd, _cumsum_rows(a, tri[0]), _cumsum_rows(a, tri[1])) for a in da_all]
        tot_all = [jnp.sum(jnp.where(lane_r == jnp.where(is_fwd, CHUNK - 1, 0), a, 0.0), axis=1, keepdims=True)
                   for a in acum_all]
        w_all = [a * jnp.exp(t - c) for a, t, c in zip(dt_all, tot_all, acum_all)]
        cols_t = [_rows_to_cols(a, eye_b) for a in acum_all]
        cd_all = [jnp.exp(t) for t in tot_all]
        ch = [(k, d, q) for k in range(len(slots)) for d, q in chains]
        hd = [(k, d, q, j) for k, d, q in ch for j in range(2)]
        cols = [slice(q * LANES, (q + 1) * LANES) for _, _, q in ch]
        xb = [xs[pl.ds(c0[k][d], CHUNK), cs_] for (k, d, _), cs_ in zip(ch, cols)]
        acum = [acum_all[k][hrow(d, q, j):hrow(d, q, j) + 1, :] for k, d, q, j in hd]
        dt = [dt_all[k][hrow(d, q, j):hrow(d, q, j) + 1, :] for k, d, q, j in hd]
        acum_b = [jnp.broadcast_to(cols_t[k][:, hrow(d, q, j):hrow(d, q, j) + 1], (CHUNK, CHUNK))
                  for k, d, q, j in hd]
        lmat = [jnp.where(incl[d], jnp.exp(jnp.where(incl[d], c - r, 0.0)), 0.0)
                for c, r, (_, d, _, _) in zip(acum_b, acum, hd)]
        m_mat = [(cb[(k, d)] * l * a).astype(BF16) for l, a, (k, d, _, _) in zip(lmat, dt, hd)]
        din_b = [jnp.exp(a) for a in acum_b]
        w_diag = [jnp.where(eye, w_all[k][hrow(d, q, j):hrow(d, q, j) + 1, :], 0.0).astype(BF16)
                  for k, d, q, j in hd]
        cd = [cd_all[k][hrow(d, q, j):hrow(d, q, j) + 1, :] for k, d, q, j in hd]
        x0 = [jnp.where(head0_lane, a, jnp.zeros_like(a)) for a in xb]
        x1 = [jnp.where(head0_lane, jnp.zeros_like(a), a) for a in xb]
        y = [_mm(m_mat[2 * n], x0[n]) + _mm(m_mat[2 * n + 1], x1[n]) for n in range(len(ch))]
        xw = [(_mm(w_diag[2 * n], x0[n]) + _mm(w_diag[2 * n + 1], x1[n])).astype(BF16) for n in range(len(ch))]
        s_in = [_tn(a, bm[(k, d)]) for a, (k, d, _) in zip(xw, ch)]
        state = [st[n] for n in range(nseq * nc)]
        for k in range(npar):
            for sq in range(nseq):
                slot = sq * npar + k
                for ci, (d, q) in enumerate(chains):
                    n = slot * nc + ci
                    si = sq * nc + ci
                    yk = y[n] + _nt(cm[(slot, d)], state[si].astype(BF16)) * jnp.where(
                        head0_lane, din_b[2 * n], din_b[2 * n + 1])
                    state[si] = state[si] * jnp.where(head0_sub, cd[2 * n], cd[2 * n + 1]) + s_in[n]
                    rows = pl.ds(c0[slot][d], CHUNK)
                    if not second:
                        yacc[rows, cols[n]] = yk
                    else:
                        yt = yacc[rows, cols[n]] + yk + dvec_ref[:, cols[n]] * xb[n].astype(F32)
                        o_ref[rows, cols[n]] = (yt * _silu(z_ref[rows, cols[n]].astype(F32))).astype(o_ref.dtype)
        for n in range(nseq * nc):
            st[n] = state[n]

    lax.fori_loop(0, half // npar, lambda i, c: (step(i, False), c)[1], 0)
    lax.fori_loop(half // npar, n_chunks // npar, lambda i, c: (step(i, True), c)[1], 0)

    if sout_ref is not None:
        if n_prev:
            sout_ref[:, 0:n_prev] = prev_ref[...]
        for sq in range(nseq):
            for ci, (d, q) in enumerate(chains):
                s = st[sq * nc + ci]
                sout_ref[sq, n_prev, d, 2 * q] = s[:SSM_HEAD_DIM]
                sout_ref[sq, n_prev, d, 2 * q + 1] = s[SSM_HEAD_DIM:]


def _ssd(p, conv_w, conv_b, d_vec, a_log, dt_bias, s0, prev, *, batch, t_len):
    n = batch * t_len
    has_ctx = s0 is not None
    assert (t_len // CHUNK) % 2 == 0
    n_groups = SSM_HEADS // SSD_GROUP_HEADS
    nseq = _seqs_per_step(batch, t_len, SSD_SLOTS)
    rows = nseq * t_len
    x_blocks = D_INNER // LANES
    big = dict(pipeline_mode=pl.Buffered(1)) if rows * SSD_GROUP_WIDTH * 2 >= (2 << 20) else {}
    par = pl.BlockSpec((None, 2 * SSD_GROUP_HEADS, 1), lambda b, g: (g, 0, 0))
    state_spec = pl.BlockSpec((nseq, 2, SSD_GROUP_HEADS, SSM_HEAD_DIM, SSM_STATE), lambda b, g: (b, 0, g, 0, 0))
    in_specs = [pl.BlockSpec((rows, SSD_GROUP_WIDTH), lambda b, g: (b, g), **big),
                pl.BlockSpec((rows, LANES), lambda b, g: (b, x_blocks + g)),
                pl.BlockSpec((rows, LANES), lambda b, g: (b, x_blocks + n_groups + g)),
                pl.BlockSpec((rows, SSD_GROUP_WIDTH), lambda b, g: (b, g), **big),
                pl.BlockSpec((SSD_GROUP_HEADS, rows), lambda b, g: (g, b)),
                pl.BlockSpec((SSD_GROUP_HEADS, rows), lambda b, g: (n_groups + g, b)),
                pl.BlockSpec((3, SSD_GROUP_WIDTH), lambda b, g: (0, g)),
                pl.BlockSpec((3, LANES), lambda b, g: (0, x_blocks + g)),
                pl.BlockSpec((3, LANES), lambda b, g: (0, x_blocks + n_groups + g)),
                pl.BlockSpec((1, SSD_GROUP_WIDTH), lambda b, g: (0, g)),
                pl.BlockSpec((1, LANES), lambda b, g: (0, x_blocks + g)),
                pl.BlockSpec((1, LANES), lambda b, g: (0, x_blocks + n_groups + g)),
                pl.BlockSpec((1, SSD_GROUP_WIDTH), lambda b, g: (0, g)), par, par]
    args = [p["xbc"], p["xbc"], p["xbc"], p["cz"], p["small_t"], p["small_t"], conv_w, conv_w, conv_w,
            conv_b, conv_b, conv_b, d_vec, a_log, dt_bias]
    out_shape = [jax.ShapeDtypeStruct((n, D_INNER), BF16)]
    out_specs = [pl.BlockSpec((rows, SSD_GROUP_WIDTH), lambda b, g: (b, g))]
    n_prev = 0
    if has_ctx:
        in_specs.append(state_spec)
        args.append(s0)
    else:
        n_prev, prev_spec, out_spec = _stacked_state_specs(
            prev, nseq, (2, SSD_GROUP_HEADS, SSM_HEAD_DIM, SSM_STATE), lambda b, g: (b, 0, 0, g, 0, 0))
        if n_prev:
            in_specs.append(prev_spec)
            args.append(prev)
        out_shape.append(jax.ShapeDtypeStruct((batch, n_prev + 1, 2, SSM_HEADS, SSM_HEAD_DIM, SSM_STATE), F32))
        out_specs.append(out_spec)
    scratch = [pltpu.VMEM((rows, SSD_GROUP_WIDTH), BF16), pltpu.VMEM((rows, LANES), BF16),
               pltpu.VMEM((rows, LANES), BF16), pltpu.VMEM((rows, SSD_GROUP_WIDTH), F32),
               pltpu.VMEM((nseq * 2 * SSD_PAIRS, 2 * SSM_HEAD_DIM, SSM_STATE), F32)]
    outs = pl.pallas_call(
        functools.partial(_ssd_kernel, t_len=t_len, nseq=nseq, has_ctx=has_ctx, n_prev=n_prev),
        grid=(batch // nseq, n_groups),
        in_specs=in_specs, out_specs=out_specs, out_shape=out_shape, scratch_shapes=scratch,
        compiler_params=pltpu.CompilerParams(
            dimension_semantics=("arbitrary", "arbitrary"), vmem_limit_bytes=VMEM_LIMIT),
        name="ssd_ctx" if has_ctx else "ssd",
    )(*args)
    return outs[0], (None if has_ctx else outs[1])


KEY_TILE = 256


def _seg_rmsnorm(x, bd):
    x2 = x * x
    hi = x2.astype(BF16)
    lo = (x2 - hi.astype(F32)).astype(BF16)
    ms = _mm(hi, bd) + _mm(lo, bd)
    return x * lax.rsqrt(ms + EPS)


def _rope128(x, cos, sin_signed):
    lane = lax.broadcasted_iota(jnp.int32, x.shape, 1)
    first = (lane % 32) < 16
    swapped = jnp.where(first, pltpu.roll(x, LANES - 16, 1), pltpu.roll(x, 16, 1))
    return x * cos + swapped * sin_signed


def _attn_kernel(*refs, t_len, tq, past, has_ctx, n_prev):
    q_ref, kv_ref, z_ref, qnw_ref, knw_ref, bd_ref = refs[:6]
    rest = refs[6:]
    kn_ref = vn_ref = pk_ref = pv_ref = None
    if has_ctx:
        cosq_ref, sinq_ref, cosk_ref, sink_ref, ck_ref, cv_ref, o_ref, k_s, v_s = rest
    else:
        if n_prev:
            pk_ref, pv_ref, rest = rest[0], rest[1], rest[2:]
        o_ref, kn_ref, vn_ref, k_s, v_s = rest
    qi = pl.program_id(1)
    rt = _row_tile(t_len)
    n_tiles = t_len // rt

    def store_values(rows, v):
        low_v = lax.broadcasted_iota(jnp.int32, v.shape, 1) < HEAD_DIM
        v_s[0, rows, :] = jnp.where(low_v, v, 1.0).astype(BF16)
        v_s[1, rows, :] = jnp.where(low_v, 1.0, v).astype(BF16)

    @pl.when(qi == 0)
    def _():
        def prep(r, carry):
            start = pl.multiple_of(r * rt, rt)
            kv = kv_ref[pl.ds(start, rt), :]
            kn = _seg_rmsnorm(kv[:, :KV_WIDTH], bd_ref[0:KV_WIDTH, 0:KV_WIDTH]) * knw_ref[...]
            if kn_ref is not None:
                kn_ref[n_prev, pl.ds(start, rt), :] = kn
                vn_ref[n_prev, pl.ds(start, rt), :] = kv[:, KV_WIDTH:]
            if has_ctx:
                kn = _rope128(kn, cosk_ref[pl.ds(start, rt), :], sink_ref[pl.ds(start, rt), :])
            k_s[pl.ds(start, rt), :] = kn.astype(BF16)
            store_values(pl.ds(start, rt), kv[:, KV_WIDTH:])
            return carry

        lax.fori_loop(0, n_tiles, prep, 0)
        if n_prev:
            kn_ref[0:n_prev] = pk_ref[...]
            vn_ref[0:n_prev] = pv_ref[...]
        if has_ctx:
            k_s[t_len:t_len + past, :] = ck_ref[...].astype(BF16)
            store_values(slice(t_len, t_len + past), cv_ref[...])

    q = _seg_rmsnorm(q_ref[...].astype(F32), bd_ref[...]) * qnw_ref[...]
    blocks = []
    for cbk in range(ATTN_WIDTH // LANES):
        blk = q[:, cbk * LANES:(cbk + 1) * LANES]
        if has_ctx:
            blk = _rope128(blk, cosq_ref[...], sinq_ref[...])
        blocks.append((blk * (HEAD_DIM ** -0.5)).astype(BF16))
    lane = lax.broadcasted_iota(jnp.int32, (tq, LANES), 1)
    low = lane < HEAD_DIM
    group = N_HEADS // KV_HEADS
    head_out = [None] * N_HEADS
    qgs = []
    for kvh in range(KV_HEADS):
        keep = low if kvh == 0 else jnp.logical_not(low)
        rows = []
        for g in range(group):
            head = kvh * group + g
            blk = blocks[head // 2]
            if head % 2 != kvh:
                blk = pltpu.roll(blk, HEAD_DIM, 1)
            rows.append(jnp.where(keep, blk, jnp.zeros_like(blk)))
        qgs.append(jnp.concatenate(rows, axis=0))

    n_kt = (t_len + past) // KEY_TILE
    keys = lambda j: slice(j * KEY_TILE, (j + 1) * KEY_TILE)
    score_tile = lambda kvh, j: _nt(qgs[kvh], k_s[keys(j), :])
    row_max = lambda tiles: jnp.max(functools.reduce(jnp.maximum, tiles), axis=-1, keepdims=True)
    s_tiles = [[score_tile(0, j) for j in range(n_kt)], []]
    m = [row_max(s_tiles[0]), None]
    p_tiles = [[], []]

    def exp_tile(kvh, j):
        p_tiles[kvh].append(jnp.exp((s_tiles[kvh][j] - m[kvh]).astype(BF16)))

    acc = [None, None]

    def pv_tile(kvh, j):
        part = _mm(p_tiles[kvh][j], v_s[kvh, keys(j), :])
        acc[kvh] = part if acc[kvh] is None else acc[kvh] + part

    for j in range(n_kt):
        s_tiles[1].append(score_tile(1, j))
        exp_tile(0, j)
    m[1] = row_max(s_tiles[1])
    for j in range(n_kt):
        pv_tile(0, j)
        exp_tile(1, j)
    for j in range(n_kt):
        pv_tile(1, j)
    for kvh in range(KV_HEADS):
        ones_lane = HEAD_DIM if kvh == 0 else 0
        o = acc[kvh] / acc[kvh][:, ones_lane:ones_lane + 1]
        for g in range(group):
            head = kvh * group + g
            og = o[g * tq:(g + 1) * tq, :]
            if head % 2 != kvh:
                og = pltpu.roll(og, HEAD_DIM, 1)
            head_out[head] = og
    outs = [jnp.where(low, head_out[2 * i], head_out[2 * i + 1]) for i in range(N_HEADS // 2)]
    o_all = jnp.concatenate(outs, axis=1)
    o_ref[...] = (o_all * _silu(z_ref[...].astype(F32))).astype(o_ref.dtype)


def _attention(p, qnw, knw, bd, rope_tabs, cache_k, cache_v, prev_kv, *, batch, t_len):
    n = batch * t_len
    has_ctx = cache_k is not None
    past = cache_k.shape[1] if has_ctx else 0
    n_prev = 0
    tq = 128 if has_ctx else min(t_len, 256)
    nq = t_len // tq
    in_specs = [pl.BlockSpec((tq, ATTN_WIDTH), lambda b, i: (b * nq + i, 0)),
                pl.BlockSpec((t_len, 2 * KV_WIDTH), lambda b, i: (b, 0)),
                pl.BlockSpec((tq, ATTN_WIDTH), lambda b, i: (b * nq + i, 0)),
                pl.BlockSpec((1, ATTN_WIDTH), lambda b, i: (0, 0)),
                pl.BlockSpec((1, KV_WIDTH), lambda b, i: (0, 0)),
                pl.BlockSpec((ATTN_WIDTH, ATTN_WIDTH), lambda b, i: (0, 0))]
    args = [p["bq"], p["kv"], p["bz"], qnw, knw, bd]
    out_shape = [jax.ShapeDtypeStruct((n, ATTN_WIDTH), BF16)]
    out_specs = [pl.BlockSpec((tq, ATTN_WIDTH), lambda b, i: (b * nq + i, 0))]
    if has_ctx:
        cos, sin = rope_tabs
        in_specs += [pl.BlockSpec((tq, LANES), lambda b, i: (i, 0)),
                     pl.BlockSpec((tq, LANES), lambda b, i: (i, 0)),
                     pl.BlockSpec((t_len, LANES), lambda b, i: (0, 0)),
                     pl.BlockSpec((t_len, LANES), lambda b, i: (0, 0)),
                     pl.BlockSpec((None, past, KV_WIDTH), lambda b, i: (b, 0, 0)),
                     pl.BlockSpec((None, past, KV_WIDTH), lambda b, i: (b, 0, 0))]
        args += [cos, sin, cos, sin, cache_k, cache_v]
    else:
        n_prev = 0 if prev_kv is None else prev_kv[0].shape[1]
        if n_prev:
            in_specs += [pl.BlockSpec((None, n_prev, t_len, KV_WIDTH), lambda b, i: (b, 0, 0, 0))] * 2
            args += list(prev_kv)
        out_shape += [jax.ShapeDtypeStruct((batch, n_prev + 1, t_len, KV_WIDTH), F32)] * 2
        out_specs += [pl.BlockSpec((None, n_prev + 1, t_len, KV_WIDTH), lambda b, i: (b, 0, 0, 0))] * 2
    scratch = [pltpu.VMEM((t_len + past, KV_WIDTH), BF16),
               pltpu.VMEM((KV_HEADS, t_len + past, KV_WIDTH), BF16)]
    outs = pl.pallas_call(
        functools.partial(_attn_kernel, t_len=t_len, tq=tq, past=past, has_ctx=has_ctx, n_prev=n_prev),
        grid=(batch, nq),
        in_specs=in_specs, out_specs=out_specs, out_shape=out_shape, scratch_shapes=scratch,
        compiler_params=pltpu.CompilerParams(
            dimension_semantics=("arbitrary", "arbitrary"), vmem_limit_bytes=VMEM_LIMIT),
        name="attn_ctx" if has_ctx else "attn",
    )(*args)
    return outs[0], (None if has_ctx else (outs[1], outs[2]))


def _out_kernel(x_ref, oa_ref, ob_ref, yc_ref, g_ref, mod_ref, snw_ref, wa_ref, wb_ref, wc_ref, wo_ref,
                fnw_ref, o_ref, *, final):
    ba = _mm(oa_ref[...], wa_ref[...])
    bb = _mm(ob_ref[...], wb_ref[...])
    yc = yc_ref[...].astype(F32)
    yn = yc * lax.rsqrt(jnp.mean(yc * yc, axis=-1, keepdims=True) + EPS) * snw_ref[...]
    bc = _mm(yn.astype(BF16), wc_ref[...])
    g = g_ref[...].astype(F32)
    merged = (jax.nn.sigmoid(g[:, 0:D_MODEL]) * ba + jax.nn.sigmoid(g[:, D_MODEL:2 * D_MODEL]) * bb
              + jax.nn.sigmoid(g[:, 2 * D_MODEL:]) * bc)
    out = _mm(merged.astype(BF16), wo_ref[...])
    xn = x_ref[...] + mod_ref[:, 2 * D_MODEL:] * out
    if final:
        xn = xn * lax.rsqrt(jnp.mean(xn * xn, axis=-1, keepdims=True) + EPS) * fnw_ref[...]
    o_ref[...] = xn


def _out_proj(x, o_a, o_b, y_c, gates, mod, ssm_norm_w, wa, wb, wc, wo, final_norm_w, *,
              t_len, mod_base, mod_per_batch, final):
    n = x.shape[0]
    tm = 512 if t_len % 512 == 0 else 256
    assert n % tm == 0 and t_len % tm == 0

    def mod_idx(i):
        return (mod_base + (i * tm) // t_len if mod_per_batch else mod_base, 0, 0)

    tok = lambda w: pl.BlockSpec((tm, w), lambda i: (i, 0))
    full = lambda a, b: pl.BlockSpec((a, b), lambda i: (0, 0))
    return pl.pallas_call(
        functools.partial(_out_kernel, final=final),
        grid=(n // tm,),
        in_specs=[tok(D_MODEL), tok(GDN_WIDTH), tok(ATTN_WIDTH), tok(D_INNER), tok(3 * D_MODEL),
                  pl.BlockSpec((None, 1, 3 * D_MODEL), mod_idx), full(1, D_INNER),
                  full(GDN_WIDTH, D_MODEL), full(ATTN_WIDTH, D_MODEL), full(D_INNER, D_MODEL),
                  full(D_MODEL, D_MODEL), full(1, D_MODEL)],
        out_specs=tok(D_MODEL),
        out_shape=jax.ShapeDtypeStruct((n, D_MODEL), F32),
        compiler_params=pltpu.CompilerParams(
            dimension_semantics=("arbitrary",), vmem_limit_bytes=VMEM_LIMIT),
        name="out_proj",
    )(x, o_a, o_b, y_c, gates, mod, ssm_norm_w, wa, wb, wc, wo, final_norm_w)


def _rope_tables(t_len):
    rot = HEAD_DIM // 2
    pos = jnp.arange(t_len)
    freqs = ROPE_BASE ** (-jnp.arange(rot // 2, dtype=F32) / (rot // 2))
    ang_row = (pos // GRID_W).astype(F32)[:, None] * freqs
    ang_col = (pos % GRID_W).astype(F32)[:, None] * freqs
    ang = jnp.concatenate([ang_row, ang_row, ang_col, ang_col], axis=1)
    sign = jnp.tile(jnp.concatenate([-jnp.ones((rot // 2,), F32), jnp.ones((rot // 2,), F32)]), 2)
    cos = jnp.tile(jnp.cos(ang), (1, 2))
    sin = jnp.tile(jnp.sin(ang) * sign, (1, 2))
    return cos, sin


def _layer_params(l, w_in, gdn_conv_w, gdn_a_log, gdn_dt_bias, gdn_norm_w, attn_q_norm, attn_k_norm,
                  ssm_conv_w, ssm_conv_b, ssm_a_log, ssm_dt_bias, ssm_d, ssm_norm_w,
                  w_branch_a, w_branch_b, w_branch_c, w_out):
    w = w_in[l]
    sl = lambda name: w[:, _SRC[name][0]:_SRC[name][0] + _SRC[name][1]]
    w_main = jnp.concatenate([sl(k) for k in _MAIN], axis=1).astype(BF16)
    beta, alpha = sl("beta"), sl("alpha")
    hd = GDN_HEADS
    w_small_t = jnp.concatenate([sl("dt"), beta[:, :hd], alpha[:, :hd], beta[:, hd:], alpha[:, hd:]],
                                axis=1).T.astype(BF16)
    zeros = jnp.zeros((hd,), F32)

    def gdn_col(a):
        return jnp.concatenate([zeros, a[0], zeros, a[1]]).reshape(4 * hd, 1)

    def ssd_col(a):
        g = a.reshape(2, SSM_HEADS // SSD_GROUP_HEADS, SSD_GROUP_HEADS)
        return jnp.concatenate([g[0], g[1]], axis=1)[..., None]

    return dict(
        w_main=w_main, w_small_t=w_small_t,
        gdn_conv_w=gdn_conv_w[l], gdn_a_log=gdn_col(gdn_a_log[l]), gdn_dt_bias=gdn_col(gdn_dt_bias[l]),
        gdn_norm_w=gdn_norm_w[l].reshape(1, LANES),
        qnw=jnp.tile(attn_q_norm[l], N_HEADS).reshape(1, ATTN_WIDTH),
        knw=jnp.tile(attn_k_norm[l], KV_HEADS).reshape(1, KV_WIDTH),
        ssm_conv_w=ssm_conv_w[l], ssm_conv_b=ssm_conv_b[l].reshape(1, SSM_XBC),
        ssm_a_log=ssd_col(ssm_a_log[l]), ssm_dt_bias=ssd_col(ssm_dt_bias[l]),
        ssm_d=jnp.repeat(ssm_d[l], SSM_HEAD_DIM).reshape(1, D_INNER),
        ssm_norm_w=ssm_norm_w[l].reshape(1, D_INNER),
        wa=w_branch_a[l].astype(BF16), wb=w_branch_b[l].astype(BF16),
        wc=w_branch_c[l].astype(BF16), wo=w_out[l].astype(BF16))


def _stream_layer(x, lp, mod_l, norm_w_l, final_norm_w, bd, rope_tabs, ctx, prev, *,
                  batch, t_len, mod_base, mod_per_batch, final):
    p = _in_proj(x, norm_w_l, mod_l, lp["w_main"], lp["w_small_t"],
                 t_len=t_len, mod_base=mod_base, mod_per_batch=mod_per_batch)
    ck, cv, sg, ss = ctx if ctx is not None else (None, None, None, None)
    prev_kv, prev_g, prev_s = prev if prev is not None else (None, None, None)
    o_a, new_sg = _gdn(p, lp["gdn_conv_w"], lp["gdn_norm_w"], lp["gdn_a_log"], lp["gdn_dt_bias"], sg, prev_g,
                       batch=batch, t_len=t_len)
    y_c, new_ss = _ssd(p, lp["ssm_conv_w"], lp["ssm_conv_b"], lp["ssm_d"], lp["ssm_a_log"],
                       lp["ssm_dt_bias"], ss, prev_s, batch=batch, t_len=t_len)
    o_b, new_kv = _attention(p, lp["qnw"], lp["knw"], bd, rope_tabs, ck, cv, prev_kv, batch=batch, t_len=t_len)
    x_new = _out_proj(x, o_a, o_b, y_c, p["gates"], mod_l, lp["ssm_norm_w"], lp["wa"], lp["wb"],
                      lp["wc"], lp["wo"], final_norm_w, t_len=t_len, mod_base=mod_base,
                      mod_per_batch=mod_per_batch, final=final)
    return x_new, (new_kv, new_sg, new_ss)


def kernel(x_prompt, x_sample, cache_k, cache_v, state_gdn, state_ssm, c, c_ctx, norm_w, w_mod, b_mod, w_in, gdn_conv_w, gdn_a_log, gdn_dt_bias, gdn_norm_w, attn_q_norm, attn_k_norm, ssm_conv_w, ssm_conv_b, ssm_a_log, ssm_dt_bias, ssm_d, ssm_norm_w, w_branch_a, w_branch_b, w_branch_c, w_out, final_norm_w):
    batch, seq, d_model = x_prompt.shape
    dec_batch, dec_seq, _ = x_sample.shape
    depth = w_in.shape[0]
    past = cache_k.shape[2]
    assert d_model == D_MODEL and seq % CHUNK == 0 and dec_seq % CHUNK == 0

    rows = -(-(1 + dec_batch) // 8) * 8
    cond = jnp.zeros((rows, D_MODEL), F32).at[0].set(c_ctx).at[1:1 + dec_batch].set(c)
    mod = _modulation(cond, w_mod, b_mod).reshape(depth, rows, 1, 3 * D_MODEL)

    seg = jnp.arange(ATTN_WIDTH) // HEAD_DIM
    bd = jnp.where(seg[:, None] == seg[None, :], 1.0 / HEAD_DIM, 0.0).astype(BF16)
    rope_tabs = _rope_tables(dec_seq)
    fnw = final_norm_w.reshape(1, D_MODEL)

    xp = x_prompt.reshape(batch * seq, D_MODEL)
    xs = x_sample.reshape(dec_batch * dec_seq, D_MODEL)
    new_cache = None
    for l in range(depth):
        lp = _layer_params(l, w_in, gdn_conv_w, gdn_a_log, gdn_dt_bias, gdn_norm_w, attn_q_norm,
                           attn_k_norm, ssm_conv_w, ssm_conv_b, ssm_a_log, ssm_dt_bias, ssm_d,
                           ssm_norm_w, w_branch_a, w_branch_b, w_branch_c, w_out)
        nw = norm_w[l].reshape(1, D_MODEL)
        final = l == depth - 1
        xp, new_cache = _stream_layer(
            xp, lp, mod[l], nw, fnw, bd, None, None, new_cache,
            batch=batch, t_len=seq, mod_base=0, mod_per_batch=False, final=final)
        ctx = (cache_k[:, l].reshape(dec_batch, past, KV_WIDTH),
               cache_v[:, l].reshape(dec_batch, past, KV_WIDTH),
               state_gdn[:, l], state_ssm[:, l])
        xs, _ = _stream_layer(
            xs, lp, mod[l], nw, fnw, bd, rope_tabs, ctx, None,
            batch=dec_batch, t_len=dec_seq, mod_base=1, mod_per_batch=True, final=final)
    (new_k, new_v), new_gdn, new_ssm = new_cache
    return (xp.reshape(batch, seq, D_MODEL), xs.reshape(dec_batch, dec_seq, D_MODEL),
            new_k.reshape(batch, depth, seq, KV_HEADS, HEAD_DIM),
            new_v.reshape(batch, depth, seq, KV_HEADS, HEAD_DIM), new_gdn, new_ssm)
```

```python
import functools
import math

import jax
import jax.numpy as jnp
from jax import lax
from jax.experimental import pallas as pl
from jax.experimental.pallas import tpu as pltpu

F32 = jnp.float32
BF16 = jnp.bfloat16

D_MODEL = 1024
EPS = 1e-6
GDN_HEADS = 4
GDN_DK = 128
GDN_WIDTH = 512
N_HEADS = 8
KV_HEADS = 2
HEAD_DIM = 64
ATTN_WIDTH = 512
KV_WIDTH = 128
GRID_W = 64
ROPE_BASE = 10000.0
SSM_HEADS = 16
SSM_HEAD_DIM = 64
D_INNER = 1024
SSM_STATE = 128
SSM_XBC = 1536
CHUNK = 128
LANES = 128
SMALL_ROWS = 48
VMEM_LIMIT = 56 * 1024 * 1024

_SRC = dict(qkv=(0, 1536), az=(1536, 512), beta=(2048, 8), alpha=(2056, 8), bq=(2064, 512),
            kv=(2576, 256), bz=(2832, 512), xbc=(3344, 1536), cz=(4880, 1024), dt=(5904, 32),
            gates=(5936, 3072))
_MAIN = ("qkv", "az", "bq", "kv", "bz", "xbc", "cz", "gates")
_MAIN_OFF = {}
_o = 0
for _n in _MAIN:
    _MAIN_OFF[_n] = _o
    _o += _SRC[_n][1]
MAIN_WIDTH = _o


def _nt(a, b):
    return lax.dot_general(a, b, (((1,), (1,)), ((), ())), preferred_element_type=F32)


def _tn(a, b):
    return lax.dot_general(a, b, (((0,), (0,)), ((), ())), preferred_element_type=F32)


def _mm(a, b):
    return jnp.dot(a, b, preferred_element_type=F32)


def _sigmoid(x):
    return 0.5 * jnp.tanh(0.5 * x) + 0.5


def _silu(x):
    h = 0.5 * x
    return h * jnp.tanh(h) + h


def _softplus(x):
    return jnp.maximum(x, 0.0) + jnp.log1p(jnp.exp(-jnp.abs(x)))


def _split3(x):
    hi = x.astype(BF16)
    r1 = x - hi.astype(F32)
    mid = r1.astype(BF16)
    lo = (r1 - mid.astype(F32)).astype(BF16)
    return hi, mid, lo


def _cumsum_rows(x, tri):
    hi, mid, lo = _split3(x)
    return _mm(hi, tri) + _mm(mid, tri) + _mm(lo, tri)


def _rows_to_cols(x, eye_b):
    hi, mid, lo = _split3(x)
    return _nt(eye_b, hi) + _nt(eye_b, mid) + _nt(eye_b, lo)


def _conv_tile(src_ref, r, rt, n_tiles, w, bias, off):
    t_len = src_ref.shape[0]
    cols = slice(off, off + LANES)
    start = pl.multiple_of(r * rt, rt)
    cur = src_ref[pl.ds(start, rt), cols].astype(F32)
    pstart = pl.multiple_of(jnp.maximum(start - 16, 0), 16)
    nstart = pl.multiple_of(jnp.minimum(start + rt, t_len - 16), 16)
    prev_row = src_ref[pl.ds(pstart, 16), cols][15:16, :].astype(F32)
    next_row = src_ref[pl.ds(nstart, 16), cols][0:1, :].astype(F32)
    pos = lax.rem(r, n_tiles)
    prev_row = jnp.where(pos > 0, prev_row, 0.0)
    next_row = jnp.where(pos < n_tiles - 1, next_row, 0.0)
    ri = lax.broadcasted_iota(jnp.int32, (rt, LANES), 0)
    x_prev = jnp.where(ri == 0, prev_row, pltpu.roll(cur, 1, 0))
    x_next = jnp.where(ri == rt - 1, next_row, pltpu.roll(cur, rt - 1, 0))
    y = w[0:1, :] * x_prev + w[1:2, :] * cur + w[2:3, :] * x_next
    if bias is not None:
        y = y + bias
    return y


def _row_tile(t_len):
    return min(t_len, 256)


SCAN_STEP_ROWS = 4096
GDN_SLOTS = 2
SSD_SLOTS = 4


def _positions_per_body(half, max_slots):
    return max(n for n in (1, 2, 4) if half % n == 0 and n <= max_slots)


def _seqs_per_step(batch, t_len, max_slots):
    npar = _positions_per_body(t_len // CHUNK // 2, max_slots)
    return max(n for n in (1, 2, 4)
               if batch % n == 0 and n * npar <= max_slots and n * t_len <= SCAN_STEP_ROWS)


def _mod_kernel(c_ref, w_ref, b_ref, o_ref):
    s = _silu(c_ref[...])
    w = w_ref[...]
    s_hi = s.astype(BF16)
    s_lo = (s - s_hi.astype(F32)).astype(BF16)
    w_hi = w.astype(BF16)
    w_lo = (w - w_hi.astype(F32)).astype(BF16)
    o_ref[...] = _mm(s_hi, w_hi) + _mm(s_lo, w_hi) + _mm(s_hi, w_lo) + b_ref[...]


def _modulation(cond, w_mod, b_mod):
    depth = w_mod.shape[0]
    rows = cond.shape[0]
    tn = 1024
    return pl.pallas_call(
        _mod_kernel,
        grid=(depth, 3 * D_MODEL // tn),
        in_specs=[
            pl.BlockSpec((rows, D_MODEL), lambda l, j: (0, 0)),
            pl.BlockSpec((None, D_MODEL, tn), lambda l, j: (l, 0, j)),
            pl.BlockSpec((None, 1, tn), lambda l, j: (l, 0, j)),
        ],
        out_specs=pl.BlockSpec((None, rows, tn), lambda l, j: (l, 0, j)),
        out_shape=jax.ShapeDtypeStruct((depth, rows, 3 * D_MODEL), F32),
        name="modulation",
    )(cond, w_mod, b_mod.reshape(depth, 1, 3 * D_MODEL))


def _in_kernel(x_ref, nw_ref, mod_ref, w_ref, wst_ref,
               o_qkv, o_az, o_bq, o_kv, o_bz, o_xbc, o_cz, o_gates, o_small):
    x = x_ref[...]
    y = x * lax.rsqrt(jnp.mean(x * x, axis=-1, keepdims=True) + EPS) * nw_ref[...]
    shift = mod_ref[:, 0:D_MODEL]
    scale = mod_ref[:, D_MODEL:2 * D_MODEL]
    h = (y * (1.0 + scale) + shift).astype(BF16)
    outs = dict(qkv=o_qkv, az=o_az, bq=o_bq, kv=o_kv, bz=o_bz, xbc=o_xbc, cz=o_cz, gates=o_gates)
    for name in _MAIN:
        off, width = _MAIN_OFF[name], _SRC[name][1]
        outs[name][...] = _mm(h, w_ref[:, off:off + width]).astype(outs[name].dtype)
    o_small[...] = _nt(wst_ref[...], h)


def _in_proj(x, norm_w, mod, w_main, w_small_t, *, t_len, mod_base, mod_per_batch):
    n = x.shape[0]
    tm = 256
    assert n % tm == 0 and t_len % tm == 0

    def mod_idx(i):
        return (mod_base + (i * tm) // t_len if mod_per_batch else mod_base, 0, 0)

    widths = {k: _SRC[k][1] for k in _MAIN}
    dtypes = {k: (F32 if k == "kv" else BF16) for k in _MAIN}
    out_shape = [jax.ShapeDtypeStruct((n, widths[k]), dtypes[k]) for k in _MAIN]
    out_shape.append(jax.ShapeDtypeStruct((SMALL_ROWS, n), F32))
    out_specs = [pl.BlockSpec((tm, widths[k]), lambda i: (i, 0)) for k in _MAIN]
    out_specs.append(pl.BlockSpec((SMALL_ROWS, tm), lambda i: (0, i)))
    outs = pl.pallas_call(
        _in_kernel,
        grid=(n // tm,),
        in_specs=[
            pl.BlockSpec((tm, D_MODEL), lambda i: (i, 0)),
            pl.BlockSpec((1, D_MODEL), lambda i: (0, 0)),
            pl.BlockSpec((None, 1, 3 * D_MODEL), mod_idx),
            pl.BlockSpec((D_MODEL, MAIN_WIDTH), lambda i: (0, 0), pipeline_mode=pl.Buffered(1)),
            pl.BlockSpec((SMALL_ROWS, D_MODEL), lambda i: (0, 0)),
        ],
        out_specs=out_specs,
        out_shape=out_shape,
        compiler_params=pltpu.CompilerParams(
            dimension_semantics=("arbitrary",), vmem_limit_bytes=VMEM_LIMIT),
        name="in_proj",
    )(x, norm_w, mod, w_main, w_small_t)
    res = dict(zip(_MAIN, outs[:-1]))
    res["small_t"] = outs[-1]
    return res


INV_BASE = 8


def _tri_inverse(nms, dirs, sub, lane, eye_f):
    def same_block(s):
        sh = int(math.log2(s))
        return (sub >> sh) == (lane >> sh)

    base = same_block(INV_BASE)
    n8 = [jnp.where(base, nm, 0.0) for nm in nms]
    p = [n.astype(BF16) for n in n8]
    y = [eye_f - n for n in n8]
    p2 = [_mm(a, a).astype(BF16) for a in p]
    both = [_mm(a, jnp.concatenate([a, b.astype(BF16)], axis=1)) for a, b in zip(p2, y)]
    y = [b + c[:, CHUNK:] for b, c in zip(y, both)]
    p4 = [c[:, :CHUNK].astype(BF16) for c in both]
    y = [b + _mm(a, b.astype(BF16)) for a, b in zip(p4, y)]
    s = INV_BASE
    while s < CHUNK:
        sel = jnp.logical_and(same_block(2 * s), jnp.logical_not(same_block(s)))
        off = [jnp.where(sel, nm, 0.0).astype(BF16) for nm in nms]
        yb = [b.astype(BF16) for b in y]
        starts = [[2 * s * b + (s if d == 0 else 0) for b in range(CHUNK // (2 * s))] for d in dirs]
        rows = [jnp.concatenate([b[a:a + s] for a in st], axis=0).astype(BF16) for b, st in zip(y, starts)]
        t = [_mm(r, o).astype(BF16) for r, o in zip(rows, off)]
        upd = [_mm(a, c) for a, c in zip(t, yb)]
        new_y = []
        for b, u, st in zip(y, upd, starts):
            parts = []
            for n, a in enumerate(st):
                keep = b[a - s:a] if a % (2 * s) else b[a + s:a + 2 * s]
                changed = b[a:a + s] - u[n * s:(n + 1) * s]
                parts += [keep, changed] if a % (2 * s) else [changed, keep]
            new_y.append(jnp.concatenate(parts, axis=0))
        y = new_y
        s *= 2
    return y


def _gdn_kernel(*refs, t_len, nseq, has_ctx, n_prev):
    (q_ref, k_ref, v_ref, z_ref, gtf_ref, gtb_ref, wq_ref, wk_ref, wv_ref, nw_ref, alog_ref, bias_ref) = refs[:12]
    rest = refs[12:]
    s0_ref = prev_ref = sout_ref = None
    if has_ctx:
        s0_ref, o_ref, qs, ks, vs, oacc, st = rest
    else:
        if n_prev:
            prev_ref, rest = rest[0], rest[1:]
        o_ref, sout_ref, qs, ks, vs, oacc, st = rest
    rt = _row_tile(t_len)
    n_tiles = t_len // rt
    n_chunks = t_len // CHUNK
    chains = [(d, h) for d in range(2) for h in range(GDN_HEADS)]

    def prep(r, carry):
        start = pl.multiple_of(r * rt, rt)
        for h in range(GDN_HEADS):
            off = h * LANES
            q = _silu(_conv_tile(q_ref, r, rt, n_tiles, wq_ref[:, off:off + LANES], None, off))
            k = _silu(_conv_tile(k_ref, r, rt, n_tiles, wk_ref[:, off:off + LANES], None, off))
            v = _silu(_conv_tile(v_ref, r, rt, n_tiles, wv_ref[:, off:off + LANES], None, off))
            q = q * lax.rsqrt(jnp.sum(q * q, axis=-1, keepdims=True) + EPS) * (GDN_DK ** -0.5)
            k = k * lax.rsqrt(jnp.sum(k * k, axis=-1, keepdims=True) + EPS)
            qs[pl.ds(start, rt), off:off + LANES] = q.astype(BF16)
            ks[pl.ds(start, rt), off:off + LANES] = k.astype(BF16)
            vs[pl.ds(start, rt), off:off + LANES] = v.astype(BF16)
        return carry

    lax.fori_loop(0, nseq * n_tiles, prep, 0)

    nc = len(chains)
    for sq in range(nseq):
        for ci, (d, h) in enumerate(chains):
            st[sq * nc + ci] = s0_ref[sq, d, h] if has_ctx else jnp.zeros((GDN_DK, CHUNK), F32)

    sub = lax.broadcasted_iota(jnp.int32, (CHUNK, CHUNK), 0)
    lane = lax.broadcasted_iota(jnp.int32, (CHUNK, CHUNK), 1)
    eye = sub == lane
    eye_f = eye.astype(F32)
    incl = [lane <= sub, lane >= sub]
    strict = [lane < sub, lane > sub]
    eye_b = eye.astype(BF16)
    tri = [(sub <= lane).astype(BF16), (sub >= lane).astype(BF16)]
    grow = lax.broadcasted_iota(jnp.int32, (4 * GDN_HEADS, CHUNK), 0)
    is_beta = ((grow >> int(math.log2(GDN_HEADS))) & 1) == 0
    is_fwd = grow < 2 * GDN_HEADS
    neg_a = jnp.where(is_beta, 0.0, -jnp.exp(alog_ref[...]))
    beta_row = lambda d, h: 2 * d * GDN_HEADS + h
    decay_row = lambda d, h: (2 * d + 1) * GDN_HEADS + h

    half = n_chunks // 2
    npar = _positions_per_body(half, GDN_SLOTS)

    def step(i, second):
        slots = [(sq, k) for sq in range(nseq) for k in range(npar)]
        c0 = [[pl.multiple_of(sq * t_len + (i * npar + k) * CHUNK, CHUNK),
               pl.multiple_of(sq * t_len + (n_chunks - 1 - (i * npar + k)) * CHUNK, CHUNK)] for sq, k in slots]
        raw = [jnp.concatenate([gtf_ref[:, pl.ds(c[0], CHUNK)], gtb_ref[:, pl.ds(c[1], CHUNK)]], axis=0)
               for c in c0]
        g_all = [neg_a * _softplus(r + bias_ref[...]) for r in raw]
        gc_all = [jnp.where(is_fwd, _cumsum_rows(g, tri[0]), _cumsum_rows(g, tri[1])) for g in g_all]
        comb = [jnp.where(is_beta, _sigmoid(r), g) for r, g in zip(raw, gc_all)]
        comb_t = [_rows_to_cols(c, eye_b) for c in comb]
        ch = [(k, d, h) for k in range(len(slots)) for d, h in chains]
        cols = [slice(h * LANES, (h + 1) * LANES) for _, _, h in ch]
        qb = [qs[pl.ds(c0[k][d], CHUNK), cs] for (k, d, _), cs in zip(ch, cols)]
        kb = [ks[pl.ds(c0[k][d], CHUNK), cs] for (k, d, _), cs in zip(ch, cols)]
        vf = [vs[pl.ds(c0[k][d], CHUNK), cs].astype(F32) for (k, d, _), cs in zip(ch, cols)]
        qf = [a.astype(F32) for a in qb]
        kf = [a.astype(F32) for a in kb]
        gc = [comb[k][decay_row(d, h):decay_row(d, h) + 1, :] for k, d, h in ch]
        beta_col = [jnp.broadcast_to(comb_t[k][:, beta_row(d, h):beta_row(d, h) + 1], (CHUNK, CHUNK))
                    for k, d, h in ch]
        gc_col = [jnp.broadcast_to(comb_t[k][:, decay_row(d, h):decay_row(d, h) + 1], (CHUNK, CHUNK))
                  for k, d, h in ch]
        kk = [_nt(a, a) for a in kb]
        qk = [_nt(a, b) for a, b in zip(qb, kb)]
        dec = [jnp.where(incl[d], jnp.exp(jnp.where(incl[d], c - r, 0.0)), 0.0)
               for c, r, (_, d, _) in zip(gc_col, gc, ch)]
        nm = [jnp.where(strict[d], a * b * c, 0.0) for a, b, c, (_, d, _) in zip(kk, dec, beta_col, ch)]
        t_inv = _tri_inverse(nm, [d for _, d, _ in ch], sub, lane, eye_f)
        eg = [jnp.exp(a) for a in gc_col]
        rhs = [jnp.concatenate([a * b, c * (b * e)], axis=1).astype(BF16)
               for a, b, c, e in zip(vf, beta_col, kf, eg)]
        sol = [_mm(a.astype(BF16), b) for a, b in zip(t_inv, rhs)]
        tot = [a[:, CHUNK - 1:CHUNK] if d == 0 else a[:, 0:1] for a, (_, d, _) in zip(gc, ch)]
        q_dec = [(a * e).astype(BF16) for a, e in zip(qf, eg)]
        k_dec = [(a * jnp.exp(t - c)).astype(BF16) for a, t, c in zip(kf, tot, gc_col)]
        intra = [(a * b).astype(BF16) for a, b in zip(qk, dec)]
        state = [st[n] for n in range(nseq * nc)]
        for k in range(npar):
            idx = [(sq * npar + k) * nc + ci for sq in range(nseq) for ci in range(nc)]
            sidx = [sq * nc + ci for sq in range(nseq) for ci in range(nc)]
            sb = [state[n].astype(BF16) for n in sidx]
            v_new = [sol[m][:, :CHUNK] - _mm(sol[m][:, CHUNK:].astype(BF16), b) for m, b in zip(idx, sb)]
            vb = [a.astype(BF16) for a in v_new]
            o = [_mm(jnp.concatenate([q_dec[m], intra[m]], axis=1), jnp.concatenate([b, e], axis=0))
                 for m, b, e in zip(idx, sb, vb)]
            for m, n, e, ov in zip(idx, sidx, vb, o):
                state[n] = state[n] * jnp.exp(tot[m]) + _tn(k_dec[m], e)
                slot, d, h = ch[m]
                rows = pl.ds(c0[slot][d], CHUNK)
                cs = slice(h * LANES, (h + 1) * LANES)
                if not second:
                    oacc[rows, cs] = ov
                else:
                    ot = oacc[rows, cs] + ov
                    ot = ot * lax.rsqrt(jnp.mean(ot * ot, axis=-1, keepdims=True) + EPS) * nw_ref[...]
                    o_ref[rows, cs] = (ot * _silu(z_ref[rows, cs].astype(F32))).astype(o_ref.dtype)
        for n in range(nseq * nc):
            st[n] = state[n]

    lax.fori_loop(0, half // npar, lambda i, c: (step(i, False), c)[1], 0)
    lax.fori_loop(half // npar, n_chunks // npar, lambda i, c: (step(i, True), c)[1], 0)

    if sout_ref is not None:
        if n_prev:
            sout_ref[:, 0:n_prev] = prev_ref[...]
        for sq in range(nseq):
            for ci, (d, h) in enumerate(chains):
                sout_ref[sq, n_prev, d, h] = st[sq * nc + ci]


def _stacked_state_specs(prev, nseq, tail, index_map):
    n_prev = 0 if prev is None else prev.shape[1]
    in_spec = None if prev is None else pl.BlockSpec((nseq, n_prev) + tail, index_map)
    return n_prev, in_spec, pl.BlockSpec((nseq, n_prev + 1) + tail, index_map)


def _gdn(p, conv_w, norm_w, a_log, dt_bias, s0, prev, *, batch, t_len):
    n = batch * t_len
    has_ctx = s0 is not None
    assert (t_len // CHUNK) % 2 == 0
    nseq = _seqs_per_step(batch, t_len, GDN_SLOTS)
    rows = nseq * t_len
    big = dict(pipeline_mode=pl.Buffered(1)) if rows * GDN_WIDTH * 2 >= (2 << 20) else {}
    seq = lambda j: pl.BlockSpec((rows, GDN_WIDTH), lambda b, j=j: (b, j), **big)
    cw = lambda j: pl.BlockSpec((3, GDN_WIDTH), lambda b, j=j: (0, j))
    gate_rows = 2 * GDN_HEADS
    first = 2 * SSM_HEADS // gate_rows
    par = pl.BlockSpec((2 * gate_rows, 1), lambda b: (0, 0))
    state_spec = pl.BlockSpec((nseq, 2, GDN_HEADS, GDN_DK, CHUNK), lambda b: (b, 0, 0, 0, 0))
    in_specs = [seq(0), seq(1), seq(2), seq(0),
                pl.BlockSpec((gate_rows, rows), lambda b: (first, b)),
                pl.BlockSpec((gate_rows, rows), lambda b: (first + 1, b)),
                cw(0), cw(1), cw(2),
                pl.BlockSpec((1, LANES), lambda b: (0, 0)), par, par]
    args = [p["qkv"], p["qkv"], p["qkv"], p["az"], p["small_t"], p["small_t"], conv_w, conv_w, conv_w,
            norm_w, a_log, dt_bias]
    out_shape = [jax.ShapeDtypeStruct((n, GDN_WIDTH), BF16)]
    out_specs = [pl.BlockSpec((rows, GDN_WIDTH), lambda b: (b, 0))]
    n_prev = 0
    if has_ctx:
        in_specs.append(state_spec)
        args.append(s0)
    else:
        tail = (2, GDN_HEADS, GDN_DK, CHUNK)
        n_prev, prev_spec, out_spec = _stacked_state_specs(prev, nseq, tail, lambda b: (b, 0, 0, 0, 0, 0))
        if n_prev:
            in_specs.append(prev_spec)
            args.append(prev)
        out_shape.append(jax.ShapeDtypeStruct((batch, n_prev + 1) + tail, F32))
        out_specs.append(out_spec)
    scratch = [pltpu.VMEM((rows, GDN_WIDTH), BF16) for _ in range(3)]
    scratch += [pltpu.VMEM((rows, GDN_WIDTH), F32), pltpu.VMEM((nseq * 2 * GDN_HEADS, GDN_DK, CHUNK), F32)]
    outs = pl.pallas_call(
        functools.partial(_gdn_kernel, t_len=t_len, nseq=nseq, has_ctx=has_ctx, n_prev=n_prev),
        grid=(batch // nseq,),
        in_specs=in_specs, out_specs=out_specs, out_shape=out_shape, scratch_shapes=scratch,
        compiler_params=pltpu.CompilerParams(
            dimension_semantics=("arbitrary",), vmem_limit_bytes=VMEM_LIMIT),
        name="gdn_ctx" if has_ctx else "gdn",
    )(*args)
    return outs[0], (None if has_ctx else outs[1])


SSD_GROUP_HEADS = 8
SSD_PAIRS = SSD_GROUP_HEADS // 2
SSD_GROUP_WIDTH = SSD_GROUP_HEADS * SSM_HEAD_DIM


def _ssd_kernel(*refs, t_len, nseq, has_ctx, n_prev):
    (x_ref, b_ref, c_ref, z_ref, dtf_ref, dtb_ref, wx_ref, wb_ref, wc_ref, bx_ref, bb_ref, bc_ref,
     dvec_ref, alog_ref, bias_ref) = refs[:15]
    rest = refs[15:]
    s0_ref = prev_ref = sout_ref = None
    if has_ctx:
        s0_ref, o_ref, xs, bs, cs, yacc, st = rest
    else:
        if n_prev:
            prev_ref, rest = rest[0], rest[1:]
        o_ref, sout_ref, xs, bs, cs, yacc, st = rest
    rt = _row_tile(t_len)
    n_tiles = t_len // rt
    n_chunks = t_len // CHUNK
    chains = [(d, q) for d in range(2) for q in range(SSD_PAIRS)]

    def prep(r, carry):
        start = pl.multiple_of(r * rt, rt)
        for q in range(SSD_PAIRS):
            off = q * LANES
            xs[pl.ds(start, rt), off:off + LANES] = _silu(_conv_tile(
                x_ref, r, rt, n_tiles, wx_ref[:, off:off + LANES], bx_ref[:, off:off + LANES], off)).astype(BF16)
        bs[pl.ds(start, rt), :] = _silu(_conv_tile(b_ref, r, rt, n_tiles, wb_ref[...], bb_ref[...], 0)).astype(BF16)
        cs[pl.ds(start, rt), :] = _silu(_conv_tile(c_ref, r, rt, n_tiles, wc_ref[...], bc_ref[...], 0)).astype(BF16)
        return carry

    lax.fori_loop(0, nseq * n_tiles, prep, 0)

    nc = len(chains)
    for sq in range(nseq):
        for ci, (d, q) in enumerate(chains):
            if has_ctx:
                st[sq * nc + ci] = jnp.concatenate([s0_ref[sq, d, 2 * q], s0_ref[sq, d, 2 * q + 1]], axis=0)
            else:
                st[sq * nc + ci] = jnp.zeros((2 * SSM_HEAD_DIM, SSM_STATE), F32)

    sub = lax.broadcasted_iota(jnp.int32, (CHUNK, CHUNK), 0)
    lane = lax.broadcasted_iota(jnp.int32, (CHUNK, CHUNK), 1)
    eye = sub == lane
    incl = [lane <= sub, lane >= sub]
    head0_lane = lane < SSM_HEAD_DIM
    head0_sub = sub < SSM_HEAD_DIM
    eye_b = eye.astype(BF16)
    tri = [(sub <= lane).astype(BF16), (sub >= lane).astype(BF16)]
    is_fwd = lax.broadcasted_iota(jnp.int32, (2 * SSD_GROUP_HEADS, CHUNK), 0) < SSD_GROUP_HEADS
    lane_r = lax.broadcasted_iota(jnp.int32, (2 * SSD_GROUP_HEADS, CHUNK), 1)
    neg_a = -jnp.exp(alog_ref[...])
    hrow = lambda d, q, j: d * SSD_GROUP_HEADS + 2 * q + j

    half = n_chunks // 2
    npar = _positions_per_body(half, SSD_SLOTS)

    def step(i, second):
        slots = [(sq, k) for sq in range(nseq) for k in range(npar)]
        c0 = [[pl.multiple_of(sq * t_len + (i * npar + k) * CHUNK, CHUNK),
               pl.multiple_of(sq * t_len + (n_chunks - 1 - (i * npar + k)) * CHUNK, CHUNK)] for sq, k in slots]
        kd = [(k, d) for k in range(len(slots)) for d in range(2)]
        bm = {(k, d): bs[pl.ds(c0[k][d], CHUNK), :] for k, d in kd}
        cm = {(k, d): cs[pl.ds(c0[k][d], CHUNK), :] for k, d in kd}
        cb = {key: _nt(cm[key], bm[key]) for key in kd}
        raw = [jnp.concatenate([dtf_ref[:, pl.ds(c[0], CHUNK)], dtb_ref[:, pl.ds(c[1], CHUNK)]], axis=0)
               for c in c0]
        dt_all = [_softplus(r + bias_ref[...]) for r in raw]
        da_all = [a * neg_a for a in dt_all]
        acum_all = [jnp.where(is_fwd, _cumsum_rows(a, tri[0]), _cumsum_rows(a, tri[1])) for a in da_all]
        tot_all = [jnp.sum(jnp.where(lane_r == jnp.where(is_fwd, CHUNK - 1, 0), a, 0.0), axis=1, keepdims=True)
                   for a in acum_all]
        w_all = [a * jnp.exp(t - c) for a, t, c in zip(dt_all, tot_all, acum_all)]
        cols_t = [_rows_to_cols(a, eye_b) for a in acum_all]
        cd_all = [jnp.exp(t) for t in tot_all]
        ch = [(k, d, q) for k in range(len(slots)) for d, q in chains]
        hd = [(k, d, q, j) for k, d, q in ch for j in range(2)]
        cols = [slice(q * LANES, (q + 1) * LANES) for _, _, q in ch]
        xb = [xs[pl.ds(c0[k][d], CHUNK), cs_] for (k, d, _), cs_ in zip(ch, cols)]
        shifted_all = [a - jnp.log(t) for a, t in zip(acum_all, dt_all)]
        shifted = [shifted_all[k][hrow(d, q, j):hrow(d, q, j) + 1, :] for k, d, q, j in hd]
        acum_b = [jnp.broadcast_to(cols_t[k][:, hrow(d, q, j):hrow(d, q, j) + 1], (CHUNK, CHUNK))
                  for k, d, q, j in hd]
        m_mat = [(cb[(k, d)] * jnp.where(incl[d], jnp.exp(c - r), 0.0)).astype(BF16)
                 for c, r, (k, d, _, _) in zip(acum_b, shifted, hd)]
        din_b = [jnp.exp(a) for a in acum_b]
        w_diag = [jnp.where(eye, w_all[k][hrow(d, q, j):hrow(d, q, j) + 1, :], 0.0).astype(BF16)
                  for k, d, q, j in hd]
        cd = [cd_all[k][hrow(d, q, j):hrow(d, q, j) + 1, :] for k, d, q, j in hd]
        x0 = [jnp.where(head0_lane, a, jnp.zeros_like(a)) for a in xb]
        x1 = [jnp.where(head0_lane, jnp.zeros_like(a), a) for a in xb]
        y = [_mm(m_mat[2 * n], x0[n]) + _mm(m_mat[2 * n + 1], x1[n]) for n in range(len(ch))]
        xw = [(_mm(w_diag[2 * n], x0[n]) + _mm(w_diag[2 * n + 1], x1[n])).astype(BF16) for n in range(len(ch))]
        s_in = [_tn(a, bm[(k, d)]) for a, (k, d, _) in zip(xw, ch)]
        state = [st[n] for n in range(nseq * nc)]
        for k in range(npar):
            for sq in range(nseq):
                slot = sq * npar + k
                for ci, (d, q) in enumerate(chains):
                    n = slot * nc + ci
                    si = sq * nc + ci
                    yk = y[n] + _nt(cm[(slot, d)], state[si].astype(BF16)) * jnp.where(
                        head0_lane, din_b[2 * n], din_b[2 * n + 1])
                    state[si] = state[si] * jnp.where(head0_sub, cd[2 * n], cd[2 * n + 1]) + s_in[n]
                    rows = pl.ds(c0[slot][d], CHUNK)
                    if not second:
                        yacc[rows, cols[n]] = yk
                    else:
                        yt = yacc[rows, cols[n]] + yk + dvec_ref[:, cols[n]] * xb[n].astype(F32)
                        o_ref[rows, cols[n]] = (yt * _silu(z_ref[rows, cols[n]].astype(F32))).astype(o_ref.dtype)
        for n in range(nseq * nc):
            st[n] = state[n]

    lax.fori_loop(0, half // npar, lambda i, c: (step(i, False), c)[1], 0)
    lax.fori_loop(half // npar, n_chunks // npar, lambda i, c: (step(i, True), c)[1], 0)

    if sout_ref is not None:
        if n_prev:
            sout_ref[:, 0:n_prev] = prev_ref[...]
        for sq in range(nseq):
            for ci, (d, q) in enumerate(chains):
                s = st[sq * nc + ci]
                sout_ref[sq, n_prev, d, 2 * q] = s[:SSM_HEAD_DIM]
                sout_ref[sq, n_prev, d, 2 * q + 1] = s[SSM_HEAD_DIM:]


def _ssd(p, conv_w, conv_b, d_vec, a_log, dt_bias, s0, prev, *, batch, t_len):
    n = batch * t_len
    has_ctx = s0 is not None
    assert (t_len // CHUNK) % 2 == 0
    n_groups = SSM_HEADS // SSD_GROUP_HEADS
    nseq = _seqs_per_step(batch, t_len, SSD_SLOTS)
    rows = nseq * t_len
    x_blocks = D_INNER // LANES
    big = dict(pipeline_mode=pl.Buffered(1)) if rows * SSD_GROUP_WIDTH * 2 >= (2 << 20) else {}
    par = pl.BlockSpec((None, 2 * SSD_GROUP_HEADS, 1), lambda b, g: (g, 0, 0))
    state_spec = pl.BlockSpec((nseq, 2, SSD_GROUP_HEADS, SSM_HEAD_DIM, SSM_STATE), lambda b, g: (b, 0, g, 0, 0))
    in_specs = [pl.BlockSpec((rows, SSD_GROUP_WIDTH), lambda b, g: (b, g), **big),
                pl.BlockSpec((rows, LANES), lambda b, g: (b, x_blocks + g)),
                pl.BlockSpec((rows, LANES), lambda b, g: (b, x_blocks + n_groups + g)),
                pl.BlockSpec((rows, SSD_GROUP_WIDTH), lambda b, g: (b, g), **big),
                pl.BlockSpec((SSD_GROUP_HEADS, rows), lambda b, g: (g, b)),
                pl.BlockSpec((SSD_GROUP_HEADS, rows), lambda b, g: (n_groups + g, b)),
                pl.BlockSpec((3, SSD_GROUP_WIDTH), lambda b, g: (0, g)),
                pl.BlockSpec((3, LANES), lambda b, g: (0, x_blocks + g)),
                pl.BlockSpec((3, LANES), lambda b, g: (0, x_blocks + n_groups + g)),
                pl.BlockSpec((1, SSD_GROUP_WIDTH), lambda b, g: (0, g)),
                pl.BlockSpec((1, LANES), lambda b, g: (0, x_blocks + g)),
                pl.BlockSpec((1, LANES), lambda b, g: (0, x_blocks + n_groups + g)),
                pl.BlockSpec((1, SSD_GROUP_WIDTH), lambda b, g: (0, g)), par, par]
    args = [p["xbc"], p["xbc"], p["xbc"], p["cz"], p["small_t"], p["small_t"], conv_w, conv_w, conv_w,
            conv_b, conv_b, conv_b, d_vec, a_log, dt_bias]
    out_shape = [jax.ShapeDtypeStruct((n, D_INNER), BF16)]
    out_specs = [pl.BlockSpec((rows, SSD_GROUP_WIDTH), lambda b, g: (b, g))]
    n_prev = 0
    if has_ctx:
        in_specs.append(state_spec)
        args.append(s0)
    else:
        n_prev, prev_spec, out_spec = _stacked_state_specs(
            prev, nseq, (2, SSD_GROUP_HEADS, SSM_HEAD_DIM, SSM_STATE), lambda b, g: (b, 0, 0, g, 0, 0))
        if n_prev:
            in_specs.append(prev_spec)
            args.append(prev)
        out_shape.append(jax.ShapeDtypeStruct((batch, n_prev + 1, 2, SSM_HEADS, SSM_HEAD_DIM, SSM_STATE), F32))
        out_specs.append(out_spec)
    scratch = [pltpu.VMEM((rows, SSD_GROUP_WIDTH), BF16), pltpu.VMEM((rows, LANES), BF16),
               pltpu.VMEM((rows, LANES), BF16), pltpu.VMEM((rows, SSD_GROUP_WIDTH), F32),
               pltpu.VMEM((nseq * 2 * SSD_PAIRS, 2 * SSM_HEAD_DIM, SSM_STATE), F32)]
    outs = pl.pallas_call(
        functools.partial(_ssd_kernel, t_len=t_len, nseq=nseq, has_ctx=has_ctx, n_prev=n_prev),
        grid=(batch // nseq, n_groups),
        in_specs=in_specs, out_specs=out_specs, out_shape=out_shape, scratch_shapes=scratch,
        compiler_params=pltpu.CompilerParams(
            dimension_semantics=("arbitrary", "arbitrary"), vmem_limit_bytes=VMEM_LIMIT),
        name="ssd_ctx" if has_ctx else "ssd",
    )(*args)
    return outs[0], (None if has_ctx else outs[1])


KEY_TILE = 256


def _seg_rmsnorm(x, bd):
    x2 = x * x
    hi = x2.astype(BF16)
    lo = (x2 - hi.astype(F32)).astype(BF16)
    ms = _mm(hi, bd) + _mm(lo, bd)
    return x * lax.rsqrt(ms + EPS)


def _rope128(x, cos, sin_signed):
    lane = lax.broadcasted_iota(jnp.int32, x.shape, 1)
    first = (lane % 32) < 16
    swapped = jnp.where(first, pltpu.roll(x, LANES - 16, 1), pltpu.roll(x, 16, 1))
    return x * cos + swapped * sin_signed


def _attn_kernel(*refs, t_len, tq, past, has_ctx, n_prev):
    q_ref, kv_ref, z_ref, qnw_ref, knw_ref, bd_ref = refs[:6]
    rest = refs[6:]
    kn_ref = vn_ref = pk_ref = pv_ref = None
    if has_ctx:
        cosq_ref, sinq_ref, cosk_ref, sink_ref, ck_ref, cv_ref, o_ref, k_s, v_s = rest
    else:
        if n_prev:
            pk_ref, pv_ref, rest = rest[0], rest[1], rest[2:]
        o_ref, kn_ref, vn_ref, k_s, v_s = rest
    qi = pl.program_id(1)
    rt = _row_tile(t_len)
    n_tiles = t_len // rt

    def store_values(rows, v):
        low_v = lax.broadcasted_iota(jnp.int32, v.shape, 1) < HEAD_DIM
        v_s[0, rows, :] = jnp.where(low_v, v, 1.0).astype(BF16)
        v_s[1, rows, :] = jnp.where(low_v, 1.0, v).astype(BF16)

    @pl.when(qi == 0)
    def _():
        def prep(r, carry):
            start = pl.multiple_of(r * rt, rt)
            kv = kv_ref[pl.ds(start, rt), :]
            kn = _seg_rmsnorm(kv[:, :KV_WIDTH], bd_ref[0:KV_WIDTH, 0:KV_WIDTH]) * knw_ref[...]
            if kn_ref is not None:
                kn_ref[n_prev, pl.ds(start, rt), :] = kn
                vn_ref[n_prev, pl.ds(start, rt), :] = kv[:, KV_WIDTH:]
            if has_ctx:
                kn = _rope128(kn, cosk_ref[pl.ds(start, rt), :], sink_ref[pl.ds(start, rt), :])
            k_s[pl.ds(start, rt), :] = kn.astype(BF16)
            store_values(pl.ds(start, rt), kv[:, KV_WIDTH:])
            return carry

        lax.fori_loop(0, n_tiles, prep, 0)
        if n_prev:
            kn_ref[0:n_prev] = pk_ref[...]
            vn_ref[0:n_prev] = pv_ref[...]
        if has_ctx:
            k_s[t_len:t_len + past, :] = ck_ref[...].astype(BF16)
            store_values(slice(t_len, t_len + past), cv_ref[...])

    q = _seg_rmsnorm(q_ref[...].astype(F32), bd_ref[...]) * qnw_ref[...]
    blocks = []
    for cbk in range(ATTN_WIDTH // LANES):
        blk = q[:, cbk * LANES:(cbk + 1) * LANES]
        if has_ctx:
            blk = _rope128(blk, cosq_ref[...], sinq_ref[...])
        blocks.append((blk * (HEAD_DIM ** -0.5)).astype(BF16))
    lane = lax.broadcasted_iota(jnp.int32, (tq, LANES), 1)
    low = lane < HEAD_DIM
    group = N_HEADS // KV_HEADS
    head_out = [None] * N_HEADS
    qgs = []
    for kvh in range(KV_HEADS):
        keep = low if kvh == 0 else jnp.logical_not(low)
        rows = []
        for g in range(group):
            head = kvh * group + g
            blk = blocks[head // 2]
            if head % 2 != kvh:
                blk = pltpu.roll(blk, HEAD_DIM, 1)
            rows.append(jnp.where(keep, blk, jnp.zeros_like(blk)))
        qgs.append(jnp.concatenate(rows, axis=0))

    n_kt = (t_len + past) // KEY_TILE
    keys = lambda j: slice(j * KEY_TILE, (j + 1) * KEY_TILE)
    score_tile = lambda kvh, j: _nt(qgs[kvh], k_s[keys(j), :])
    row_max = lambda tiles: jnp.max(functools.reduce(jnp.maximum, tiles), axis=-1, keepdims=True)
    s_tiles = [[score_tile(0, j) for j in range(n_kt)], []]
    m = [row_max(s_tiles[0]), None]
    p_tiles = [[], []]

    def exp_tile(kvh, j):
        p_tiles[kvh].append(jnp.exp((s_tiles[kvh][j] - m[kvh]).astype(BF16)))

    acc = [None, None]

    def pv_tile(kvh, j):
        part = _mm(p_tiles[kvh][j], v_s[kvh, keys(j), :])
        acc[kvh] = part if acc[kvh] is None else acc[kvh] + part

    for j in range(n_kt):
        s_tiles[1].append(score_tile(1, j))
        exp_tile(0, j)
    m[1] = row_max(s_tiles[1])
    for j in range(n_kt):
        pv_tile(0, j)
        exp_tile(1, j)
    for j in range(n_kt):
        pv_tile(1, j)
    for kvh in range(KV_HEADS):
        ones_lane = HEAD_DIM if kvh == 0 else 0
        o = acc[kvh] / acc[kvh][:, ones_lane:ones_lane + 1]
        for g in range(group):
            head = kvh * group + g
            og = o[g * tq:(g + 1) * tq, :]
            if head % 2 != kvh:
                og = pltpu.roll(og, HEAD_DIM, 1)
            head_out[head] = og
    outs = [jnp.where(low, head_out[2 * i], head_out[2 * i + 1]) for i in range(N_HEADS // 2)]
    o_all = jnp.concatenate(outs, axis=1)
    o_ref[...] = (o_all * _silu(z_ref[...].astype(F32))).astype(o_ref.dtype)


def _attention(p, qnw, knw, bd, rope_tabs, cache_k, cache_v, prev_kv, *, batch, t_len):
    n = batch * t_len
    has_ctx = cache_k is not None
    past = cache_k.shape[1] if has_ctx else 0
    n_prev = 0
    tq = min(t_len, 256)
    nq = t_len // tq
    once = dict(pipeline_mode=pl.Buffered(1)) if t_len * 2 * KV_WIDTH * 4 >= (2 << 20) else {}
    in_specs = [pl.BlockSpec((tq, ATTN_WIDTH), lambda b, i: (b * nq + i, 0)),
                pl.BlockSpec((t_len, 2 * KV_WIDTH), lambda b, i: (b, 0), **once),
                pl.BlockSpec((tq, ATTN_WIDTH), lambda b, i: (b * nq + i, 0)),
                pl.BlockSpec((1, ATTN_WIDTH), lambda b, i: (0, 0)),
                pl.BlockSpec((1, KV_WIDTH), lambda b, i: (0, 0)),
                pl.BlockSpec((ATTN_WIDTH, ATTN_WIDTH), lambda b, i: (0, 0))]
    args = [p["bq"], p["kv"], p["bz"], qnw, knw, bd]
    out_shape = [jax.ShapeDtypeStruct((n, ATTN_WIDTH), BF16)]
    out_specs = [pl.BlockSpec((tq, ATTN_WIDTH), lambda b, i: (b * nq + i, 0))]
    if has_ctx:
        cos, sin = rope_tabs
        in_specs += [pl.BlockSpec((tq, LANES), lambda b, i: (i, 0)),
                     pl.BlockSpec((tq, LANES), lambda b, i: (i, 0)),
                     pl.BlockSpec((t_len, LANES), lambda b, i: (0, 0), **once),
                     pl.BlockSpec((t_len, LANES), lambda b, i: (0, 0), **once),
                     pl.BlockSpec((None, past, KV_WIDTH), lambda b, i: (b, 0, 0)),
                     pl.BlockSpec((None, past, KV_WIDTH), lambda b, i: (b, 0, 0))]
        args += [cos, sin, cos, sin, cache_k, cache_v]
    else:
        n_prev = 0 if prev_kv is None else prev_kv[0].shape[1]
        if n_prev:
            in_specs += [pl.BlockSpec((None, n_prev, t_len, KV_WIDTH), lambda b, i: (b, 0, 0, 0))] * 2
            args += list(prev_kv)
        out_shape += [jax.ShapeDtypeStruct((batch, n_prev + 1, t_len, KV_WIDTH), F32)] * 2
        out_specs += [pl.BlockSpec((None, n_prev + 1, t_len, KV_WIDTH), lambda b, i: (b, 0, 0, 0))] * 2
    scratch = [pltpu.VMEM((t_len + past, KV_WIDTH), BF16),
               pltpu.VMEM((KV_HEADS, t_len + past, KV_WIDTH), BF16)]
    outs = pl.pallas_call(
        functools.partial(_attn_kernel, t_len=t_len, tq=tq, past=past, has_ctx=has_ctx, n_prev=n_prev),
        grid=(batch, nq),
        in_specs=in_specs, out_specs=out_specs, out_shape=out_shape, scratch_shapes=scratch,
        compiler_params=pltpu.CompilerParams(
            dimension_semantics=("arbitrary", "arbitrary"), vmem_limit_bytes=VMEM_LIMIT),
        name="attn_ctx" if has_ctx else "attn",
    )(*args)
    return outs[0], (None if has_ctx else (outs[1], outs[2]))


def _out_kernel(x_ref, oa_ref, ob_ref, yc_ref, g_ref, mod_ref, snw_ref, wa_ref, wb_ref, wc_ref, wo_ref,
                fnw_ref, o_ref, *, final):
    ba = _mm(oa_ref[...], wa_ref[...])
    bb = _mm(ob_ref[...], wb_ref[...])
    yc = yc_ref[...].astype(F32)
    yn = yc * lax.rsqrt(jnp.mean(yc * yc, axis=-1, keepdims=True) + EPS) * snw_ref[...]
    bc = _mm(yn.astype(BF16), wc_ref[...])
    g = g_ref[...].astype(F32)
    merged = (_sigmoid(g[:, 0:D_MODEL]) * ba + _sigmoid(g[:, D_MODEL:2 * D_MODEL]) * bb
              + _sigmoid(g[:, 2 * D_MODEL:]) * bc)
    out = _mm(merged.astype(BF16), wo_ref[...])
    xn = x_ref[...] + mod_ref[:, 2 * D_MODEL:] * out
    if final:
        xn = xn * lax.rsqrt(jnp.mean(xn * xn, axis=-1, keepdims=True) + EPS) * fnw_ref[...]
    o_ref[...] = xn


def _out_proj(x, o_a, o_b, y_c, gates, mod, ssm_norm_w, wa, wb, wc, wo, final_norm_w, *,
              t_len, mod_base, mod_per_batch, final):
    n = x.shape[0]
    tm = 512 if t_len % 512 == 0 else 256
    assert n % tm == 0 and t_len % tm == 0

    def mod_idx(i):
        return (mod_base + (i * tm) // t_len if mod_per_batch else mod_base, 0, 0)

    tok = lambda w: pl.BlockSpec((tm, w), lambda i: (i, 0))
    full = lambda a, b: pl.BlockSpec((a, b), lambda i: (0, 0))
    return pl.pallas_call(
        functools.partial(_out_kernel, final=final),
        grid=(n // tm,),
        in_specs=[tok(D_MODEL), tok(GDN_WIDTH), tok(ATTN_WIDTH), tok(D_INNER), tok(3 * D_MODEL),
                  pl.BlockSpec((None, 1, 3 * D_MODEL), mod_idx), full(1, D_INNER),
                  full(GDN_WIDTH, D_MODEL), full(ATTN_WIDTH, D_MODEL), full(D_INNER, D_MODEL),
                  full(D_MODEL, D_MODEL), full(1, D_MODEL)],
        out_specs=tok(D_MODEL),
        out_shape=jax.ShapeDtypeStruct((n, D_MODEL), F32),
        compiler_params=pltpu.CompilerParams(
            dimension_semantics=("arbitrary",), vmem_limit_bytes=VMEM_LIMIT),
        name="out_proj",
    )(x, o_a, o_b, y_c, gates, mod, ssm_norm_w, wa, wb, wc, wo, final_norm_w)


def _rope_tables(t_len):
    rot = HEAD_DIM // 2
    pos = jnp.arange(t_len)
    freqs = ROPE_BASE ** (-jnp.arange(rot // 2, dtype=F32) / (rot // 2))
    ang_row = (pos // GRID_W).astype(F32)[:, None] * freqs
    ang_col = (pos % GRID_W).astype(F32)[:, None] * freqs
    ang = jnp.concatenate([ang_row, ang_row, ang_col, ang_col], axis=1)
    sign = jnp.tile(jnp.concatenate([-jnp.ones((rot // 2,), F32), jnp.ones((rot // 2,), F32)]), 2)
    cos = jnp.tile(jnp.cos(ang), (1, 2))
    sin = jnp.tile(jnp.sin(ang) * sign, (1, 2))
    return cos, sin


def _layer_params(l, w_in, gdn_conv_w, gdn_a_log, gdn_dt_bias, gdn_norm_w, attn_q_norm, attn_k_norm,
                  ssm_conv_w, ssm_conv_b, ssm_a_log, ssm_dt_bias, ssm_d, ssm_norm_w,
                  w_branch_a, w_branch_b, w_branch_c, w_out):
    w = w_in[l]
    sl = lambda name: w[:, _SRC[name][0]:_SRC[name][0] + _SRC[name][1]]
    w_main = jnp.concatenate([sl(k) for k in _MAIN], axis=1)
    beta, alpha = sl("beta"), sl("alpha")
    hd = GDN_HEADS
    w_small_t = jnp.concatenate([sl("dt"), beta[:, :hd], alpha[:, :hd], beta[:, hd:], alpha[:, hd:]], axis=1).T
    zeros = jnp.zeros((hd,), F32)

    def gdn_col(a):
        return jnp.concatenate([zeros, a[0], zeros, a[1]]).reshape(4 * hd, 1)

    def ssd_col(a):
        g = a.reshape(2, SSM_HEADS // SSD_GROUP_HEADS, SSD_GROUP_HEADS)
        return jnp.concatenate([g[0], g[1]], axis=1)[..., None]

    return dict(
        w_main=w_main, w_small_t=w_small_t,
        gdn_conv_w=gdn_conv_w[l], gdn_a_log=gdn_col(gdn_a_log[l]), gdn_dt_bias=gdn_col(gdn_dt_bias[l]),
        gdn_norm_w=gdn_norm_w[l].reshape(1, LANES),
        qnw=jnp.tile(attn_q_norm[l], N_HEADS).reshape(1, ATTN_WIDTH),
        knw=jnp.tile(attn_k_norm[l], KV_HEADS).reshape(1, KV_WIDTH),
        ssm_conv_w=ssm_conv_w[l], ssm_conv_b=ssm_conv_b[l].reshape(1, SSM_XBC),
        ssm_a_log=ssd_col(ssm_a_log[l]), ssm_dt_bias=ssd_col(ssm_dt_bias[l]),
        ssm_d=jnp.repeat(ssm_d[l], SSM_HEAD_DIM).reshape(1, D_INNER),
        ssm_norm_w=ssm_norm_w[l].reshape(1, D_INNER),
        wa=w_branch_a[l].astype(BF16), wb=w_branch_b[l].astype(BF16),
        wc=w_branch_c[l].astype(BF16), wo=w_out[l].astype(BF16))


def _stream_layer(x, lp, mod_l, norm_w_l, final_norm_w, bd, rope_tabs, ctx, prev, *,
                  batch, t_len, mod_base, mod_per_batch, final):
    p = _in_proj(x, norm_w_l, mod_l, lp["w_main"], lp["w_small_t"],
                 t_len=t_len, mod_base=mod_base, mod_per_batch=mod_per_batch)
    ck, cv, sg, ss = ctx if ctx is not None else (None, None, None, None)
    prev_kv, prev_g, prev_s = prev if prev is not None else (None, None, None)
    o_a, new_sg = _gdn(p, lp["gdn_conv_w"], lp["gdn_norm_w"], lp["gdn_a_log"], lp["gdn_dt_bias"], sg, prev_g,
                       batch=batch, t_len=t_len)
    y_c, new_ss = _ssd(p, lp["ssm_conv_w"], lp["ssm_conv_b"], lp["ssm_d"], lp["ssm_a_log"],
                       lp["ssm_dt_bias"], ss, prev_s, batch=batch, t_len=t_len)
    o_b, new_kv = _attention(p, lp["qnw"], lp["knw"], bd, rope_tabs, ck, cv, prev_kv, batch=batch, t_len=t_len)
    x_new = _out_proj(x, o_a, o_b, y_c, p["gates"], mod_l, lp["ssm_norm_w"], lp["wa"], lp["wb"],
                      lp["wc"], lp["wo"], final_norm_w, t_len=t_len, mod_base=mod_base,
                      mod_per_batch=mod_per_batch, final=final)
    return x_new, (new_kv, new_sg, new_ss)


def kernel(x_prompt, x_sample, cache_k, cache_v, state_gdn, state_ssm, c, c_ctx, norm_w, w_mod, b_mod, w_in, gdn_conv_w, gdn_a_log, gdn_dt_bias, gdn_norm_w, attn_q_norm, attn_k_norm, ssm_conv_w, ssm_conv_b, ssm_a_log, ssm_dt_bias, ssm_d, ssm_norm_w, w_branch_a, w_branch_b, w_branch_c, w_out, final_norm_w):
    batch, seq, d_model = x_prompt.shape
    dec_batch, dec_seq, _ = x_sample.shape
    depth = w_in.shape[0]
    past = cache_k.shape[2]
    assert d_model == D_MODEL and seq % CHUNK == 0 and dec_seq % CHUNK == 0

    rows = -(-(1 + dec_batch) // 8) * 8
    cond = jnp.zeros((rows, D_MODEL), F32).at[0].set(c_ctx).at[1:1 + dec_batch].set(c)
    mod = _modulation(cond, w_mod, b_mod).reshape(depth, rows, 1, 3 * D_MODEL)

    seg = jnp.arange(ATTN_WIDTH) // HEAD_DIM
    bd = jnp.where(seg[:, None] == seg[None, :], 1.0 / HEAD_DIM, 0.0).astype(BF16)
    rope_tabs = _rope_tables(dec_seq)
    fnw = final_norm_w.reshape(1, D_MODEL)

    xp = x_prompt.reshape(batch * seq, D_MODEL)
    xs = x_sample.reshape(dec_batch * dec_seq, D_MODEL)
    new_cache = None
    w_in_b = w_in.astype(BF16)
    for l in range(depth):
        lp = _layer_params(l, w_in_b, gdn_conv_w, gdn_a_log, gdn_dt_bias, gdn_norm_w, attn_q_norm,
                           attn_k_norm, ssm_conv_w, ssm_conv_b, ssm_a_log, ssm_dt_bias, ssm_d,
                           ssm_norm_w, w_branch_a, w_branch_b, w_branch_c, w_out)
        nw = norm_w[l].reshape(1, D_MODEL)
        final = l == depth - 1
        xp, new_cache = _stream_layer(
            xp, lp, mod[l], nw, fnw, bd, None, None, new_cache,
            batch=batch, t_len=seq, mod_base=0, mod_per_batch=False, final=final)
        ctx = (cache_k[:, l].reshape(dec_batch, past, KV_WIDTH),
               cache_v[:, l].reshape(dec_batch, past, KV_WIDTH),
               state_gdn[:, l], state_ssm[:, l])
        xs, _ = _stream_layer(
            xs, lp, mod[l], nw, fnw, bd, rope_tabs, ctx, None,
            batch=dec_batch, t_len=dec_seq, mod_base=1, mod_per_batch=True, final=final)
    (new_k, new_v), new_gdn, new_ssm = new_cache
    return (xp.reshape(batch, seq, D_MODEL), xs.reshape(dec_batch, dec_seq, D_MODEL),
            new_k.reshape(batch, depth, seq, KV_HEADS, HEAD_DIM),
            new_v.reshape(batch, depth, seq, KV_HEADS, HEAD_DIM), new_gdn, new_ssm)
```

```python
import functools
import math

import jax
import jax.numpy as jnp
from jax import lax
from jax.experimental import pallas as pl
from jax.experimental.pallas import tpu as pltpu

F32 = jnp.float32
BF16 = jnp.bfloat16

D_MODEL = 1024
EPS = 1e-6
GDN_HEADS = 4
GDN_DK = 128
GDN_WIDTH = 512
N_HEADS = 8
KV_HEADS = 2
HEAD_DIM = 64
ATTN_WIDTH = 512
KV_WIDTH = 128
GRID_W = 64
ROPE_BASE = 10000.0
SSM_HEADS = 16
SSM_HEAD_DIM = 64
D_INNER = 1024
SSM_STATE = 128
SSM_XBC = 1536
CHUNK = 128
LANES = 128
SMALL_ROWS = 48
VMEM_LIMIT = 56 * 1024 * 1024

_SRC = dict(qkv=(0, 1536), az=(1536, 512), beta=(2048, 8), alpha=(2056, 8), bq=(2064, 512),
            kv=(2576, 256), bz=(2832, 512), xbc=(3344, 1536), cz=(4880, 1024), dt=(5904, 32),
            gates=(5936, 3072))
_MAIN = ("qkv", "az", "bq", "kv", "bz", "xbc", "cz", "gates")
_MAIN_OFF = {}
_o = 0
for _n in _MAIN:
    _MAIN_OFF[_n] = _o
    _o += _SRC[_n][1]
MAIN_WIDTH = _o


def _nt(a, b):
    return lax.dot_general(a, b, (((1,), (1,)), ((), ())), preferred_element_type=F32)


def _tn(a, b):
    return lax.dot_general(a, b, (((0,), (0,)), ((), ())), preferred_element_type=F32)


def _mm(a, b):
    return jnp.dot(a, b, preferred_element_type=F32)


def _sigmoid(x):
    return 0.5 * jnp.tanh(0.5 * x) + 0.5


def _silu(x):
    h = 0.5 * x
    return h * jnp.tanh(h) + h


def _softplus(x):
    return jnp.maximum(x, 0.0) + jnp.log1p(jnp.exp(-jnp.abs(x)))


def _split3(x):
    hi = x.astype(BF16)
    r1 = x - hi.astype(F32)
    mid = r1.astype(BF16)
    lo = (r1 - mid.astype(F32)).astype(BF16)
    return hi, mid, lo


def _cumsum_rows(x, tri):
    hi, mid, lo = _split3(x)
    return _mm(hi, tri) + _mm(mid, tri) + _mm(lo, tri)


def _rows_to_cols(x, eye_b):
    hi, mid, lo = _split3(x)
    return _nt(eye_b, hi) + _nt(eye_b, mid) + _nt(eye_b, lo)


def _conv_tile(src_ref, r, rt, n_tiles, w, bias, off):
    t_len = src_ref.shape[0]
    cols = slice(off, off + LANES)
    start = pl.multiple_of(r * rt, rt)
    cur = src_ref[pl.ds(start, rt), cols].astype(F32)
    pstart = pl.multiple_of(jnp.maximum(start - 16, 0), 16)
    nstart = pl.multiple_of(jnp.minimum(start + rt, t_len - 16), 16)
    prev_row = src_ref[pl.ds(pstart, 16), cols][15:16, :].astype(F32)
    next_row = src_ref[pl.ds(nstart, 16), cols][0:1, :].astype(F32)
    pos = lax.rem(r, n_tiles)
    prev_row = jnp.where(pos > 0, prev_row, 0.0)
    next_row = jnp.where(pos < n_tiles - 1, next_row, 0.0)
    ri = lax.broadcasted_iota(jnp.int32, (rt, LANES), 0)
    x_prev = jnp.where(ri == 0, prev_row, pltpu.roll(cur, 1, 0))
    x_next = jnp.where(ri == rt - 1, next_row, pltpu.roll(cur, rt - 1, 0))
    y = w[0:1, :] * x_prev + w[1:2, :] * cur + w[2:3, :] * x_next
    if bias is not None:
        y = y + bias
    return y


def _row_tile(t_len):
    return min(t_len, 256)


SCAN_STEP_ROWS = 4096
GDN_SLOTS = 2
SSD_SLOTS = 4


def _positions_per_body(half, max_slots):
    return max(n for n in (1, 2, 4) if half % n == 0 and n <= max_slots)


def _seqs_per_step(batch, t_len, max_slots):
    npar = _positions_per_body(t_len // CHUNK // 2, max_slots)
    return max(n for n in (1, 2, 4)
               if batch % n == 0 and n * npar <= max_slots and n * t_len <= SCAN_STEP_ROWS)


def _mod_kernel(c_ref, w_ref, b_ref, o_ref):
    s = _silu(c_ref[...])
    w = w_ref[...]
    s_hi = s.astype(BF16)
    s_lo = (s - s_hi.astype(F32)).astype(BF16)
    w_hi = w.astype(BF16)
    w_lo = (w - w_hi.astype(F32)).astype(BF16)
    o_ref[...] = _mm(s_hi, w_hi) + _mm(s_lo, w_hi) + _mm(s_hi, w_lo) + b_ref[...]


def _modulation(cond, w_mod, b_mod):
    depth = w_mod.shape[0]
    rows = cond.shape[0]
    tn = 1024
    return pl.pallas_call(
        _mod_kernel,
        grid=(depth, 3 * D_MODEL // tn),
        in_specs=[
            pl.BlockSpec((rows, D_MODEL), lambda l, j: (0, 0)),
            pl.BlockSpec((None, D_MODEL, tn), lambda l, j: (l, 0, j)),
            pl.BlockSpec((None, 1, tn), lambda l, j: (l, 0, j)),
        ],
        out_specs=pl.BlockSpec((None, rows, tn), lambda l, j: (l, 0, j)),
        out_shape=jax.ShapeDtypeStruct((depth, rows, 3 * D_MODEL), F32),
        name="modulation",
    )(cond, w_mod, b_mod.reshape(depth, 1, 3 * D_MODEL))


def _in_kernel(x_ref, nw_ref, mod_ref, w_ref, wst_ref,
               o_qkv, o_az, o_bq, o_kv, o_bz, o_xbc, o_cz, o_gates, o_small):
    x = x_ref[...]
    y = x * lax.rsqrt(jnp.mean(x * x, axis=-1, keepdims=True) + EPS) * nw_ref[...]
    shift = mod_ref[:, 0:D_MODEL]
    scale = mod_ref[:, D_MODEL:2 * D_MODEL]
    h = (y * (1.0 + scale) + shift).astype(BF16)
    outs = dict(qkv=o_qkv, az=o_az, bq=o_bq, kv=o_kv, bz=o_bz, xbc=o_xbc, cz=o_cz, gates=o_gates)
    for name in _MAIN:
        off, width = _MAIN_OFF[name], _SRC[name][1]
        outs[name][...] = _mm(h, w_ref[:, off:off + width]).astype(outs[name].dtype)
    o_small[...] = _nt(wst_ref[...], h)


def _in_proj(x, norm_w, mod, w_main, w_small_t, *, t_len, mod_base, mod_per_batch):
    n = x.shape[0]
    tm = 256
    assert n % tm == 0 and t_len % tm == 0
    w_all, layer = w_main

    def mod_idx(i):
        return (mod_base + (i * tm) // t_len if mod_per_batch else mod_base, 0, 0)

    widths = {k: _SRC[k][1] for k in _MAIN}
    dtypes = {k: (F32 if k == "kv" else BF16) for k in _MAIN}
    out_shape = [jax.ShapeDtypeStruct((n, widths[k]), dtypes[k]) for k in _MAIN]
    out_shape.append(jax.ShapeDtypeStruct((SMALL_ROWS, n), F32))
    out_specs = [pl.BlockSpec((tm, widths[k]), lambda i: (i, 0)) for k in _MAIN]
    out_specs.append(pl.BlockSpec((SMALL_ROWS, tm), lambda i: (0, i)))
    outs = pl.pallas_call(
        _in_kernel,
        grid=(n // tm,),
        in_specs=[
            pl.BlockSpec((tm, D_MODEL), lambda i: (i, 0)),
            pl.BlockSpec((1, D_MODEL), lambda i: (0, 0)),
            pl.BlockSpec((None, 1, 3 * D_MODEL), mod_idx),
            pl.BlockSpec((None, D_MODEL, MAIN_WIDTH), lambda i: (layer, 0, 0), pipeline_mode=pl.Buffered(1)),
            pl.BlockSpec((SMALL_ROWS, D_MODEL), lambda i: (0, 0)),
        ],
        out_specs=out_specs,
        out_shape=out_shape,
        compiler_params=pltpu.CompilerParams(
            dimension_semantics=("arbitrary",), vmem_limit_bytes=VMEM_LIMIT),
        name="in_proj",
    )(x, norm_w, mod, w_all, w_small_t)
    res = dict(zip(_MAIN, outs[:-1]))
    res["small_t"] = outs[-1]
    return res


INV_BASE = 8


def _tri_inverse(nms, dirs, sub, lane, eye_f):
    def same_block(s):
        sh = int(math.log2(s))
        return (sub >> sh) == (lane >> sh)

    base = same_block(INV_BASE)
    n8 = [jnp.where(base, nm, 0.0) for nm in nms]
    p = [n.astype(BF16) for n in n8]
    y = [eye_f - n for n in n8]
    p2 = [_mm(a, a).astype(BF16) for a in p]
    both = [_mm(a, jnp.concatenate([a, b.astype(BF16)], axis=1)) for a, b in zip(p2, y)]
    y = [b + c[:, CHUNK:] for b, c in zip(y, both)]
    p4 = [c[:, :CHUNK].astype(BF16) for c in both]
    y = [b + _mm(a, b.astype(BF16)) for a, b in zip(p4, y)]
    s = INV_BASE
    while s < CHUNK:
        sel = jnp.logical_and(same_block(2 * s), jnp.logical_not(same_block(s)))
        off = [jnp.where(sel, nm, 0.0).astype(BF16) for nm in nms]
        yb = [b.astype(BF16) for b in y]
        starts = [[2 * s * b + (s if d == 0 else 0) for b in range(CHUNK // (2 * s))] for d in dirs]
        rows = [jnp.concatenate([b[a:a + s] for a in st], axis=0).astype(BF16) for b, st in zip(y, starts)]
        t = [_mm(r, o).astype(BF16) for r, o in zip(rows, off)]
        upd = [_mm(a, c) for a, c in zip(t, yb)]
        new_y = []
        for b, u, st in zip(y, upd, starts):
            parts = []
            for n, a in enumerate(st):
                keep = b[a - s:a] if a % (2 * s) else b[a + s:a + 2 * s]
                changed = b[a:a + s] - u[n * s:(n + 1) * s]
                parts += [keep, changed] if a % (2 * s) else [changed, keep]
            new_y.append(jnp.concatenate(parts, axis=0))
        y = new_y
        s *= 2
    return y


def _gdn_kernel(*refs, t_len, nseq, has_ctx, n_prev):
    (q_ref, k_ref, v_ref, z_ref, gtf_ref, gtb_ref, wq_ref, wk_ref, wv_ref, nw_ref, alog_ref, bias_ref) = refs[:12]
    rest = refs[12:]
    s0_ref = prev_ref = sout_ref = None
    if has_ctx:
        s0_ref, o_ref, qs, ks, vs, oacc, st = rest
    else:
        if n_prev:
            prev_ref, rest = rest[0], rest[1:]
        o_ref, sout_ref, qs, ks, vs, oacc, st = rest
    rt = _row_tile(t_len)
    n_tiles = t_len // rt
    n_chunks = t_len // CHUNK
    chains = [(d, h) for d in range(2) for h in range(GDN_HEADS)]

    def prep(r, carry):
        start = pl.multiple_of(r * rt, rt)
        for h in range(GDN_HEADS):
            off = h * LANES
            q = _silu(_conv_tile(q_ref, r, rt, n_tiles, wq_ref[:, off:off + LANES], None, off))
            k = _silu(_conv_tile(k_ref, r, rt, n_tiles, wk_ref[:, off:off + LANES], None, off))
            v = _silu(_conv_tile(v_ref, r, rt, n_tiles, wv_ref[:, off:off + LANES], None, off))
            q = q * lax.rsqrt(jnp.sum(q * q, axis=-1, keepdims=True) + EPS) * (GDN_DK ** -0.5)
            k = k * lax.rsqrt(jnp.sum(k * k, axis=-1, keepdims=True) + EPS)
            qs[pl.ds(start, rt), off:off + LANES] = q.astype(BF16)
            ks[pl.ds(start, rt), off:off + LANES] = k.astype(BF16)
            vs[pl.ds(start, rt), off:off + LANES] = v.astype(BF16)
        return carry

    lax.fori_loop(0, nseq * n_tiles, prep, 0)

    nc = len(chains)
    for sq in range(nseq):
        for ci, (d, h) in enumerate(chains):
            st[sq * nc + ci] = s0_ref[sq, d, h] if has_ctx else jnp.zeros((GDN_DK, CHUNK), F32)

    sub = lax.broadcasted_iota(jnp.int32, (CHUNK, CHUNK), 0)
    lane = lax.broadcasted_iota(jnp.int32, (CHUNK, CHUNK), 1)
    eye = sub == lane
    eye_f = eye.astype(F32)
    incl = [lane <= sub, lane >= sub]
    strict = [lane < sub, lane > sub]
    eye_b = eye.astype(BF16)
    tri = [(sub <= lane).astype(BF16), (sub >= lane).astype(BF16)]
    grow = lax.broadcasted_iota(jnp.int32, (4 * GDN_HEADS, CHUNK), 0)
    is_beta = ((grow >> int(math.log2(GDN_HEADS))) & 1) == 0
    is_fwd = grow < 2 * GDN_HEADS
    neg_a = jnp.where(is_beta, 0.0, -jnp.exp(alog_ref[...]))
    beta_row = lambda d, h: 2 * d * GDN_HEADS + h
    decay_row = lambda d, h: (2 * d + 1) * GDN_HEADS + h

    half = n_chunks // 2
    npar = _positions_per_body(half, GDN_SLOTS)

    def step(i, second):
        slots = [(sq, k) for sq in range(nseq) for k in range(npar)]
        c0 = [[pl.multiple_of(sq * t_len + (i * npar + k) * CHUNK, CHUNK),
               pl.multiple_of(sq * t_len + (n_chunks - 1 - (i * npar + k)) * CHUNK, CHUNK)] for sq, k in slots]
        raw = [jnp.concatenate([gtf_ref[:, pl.ds(c[0], CHUNK)], gtb_ref[:, pl.ds(c[1], CHUNK)]], axis=0)
               for c in c0]
        g_all = [neg_a * _softplus(r + bias_ref[...]) for r in raw]
        gc_all = [jnp.where(is_fwd, _cumsum_rows(g, tri[0]), _cumsum_rows(g, tri[1])) for g in g_all]
        comb = [jnp.where(is_beta, _sigmoid(r), g) for r, g in zip(raw, gc_all)]
        comb_t = [_rows_to_cols(c, eye_b) for c in comb]
        ch = [(k, d, h) for k in range(len(slots)) for d, h in chains]
        cols = [slice(h * LANES, (h + 1) * LANES) for _, _, h in ch]
        qb = [qs[pl.ds(c0[k][d], CHUNK), cs] for (k, d, _), cs in zip(ch, cols)]
        kb = [ks[pl.ds(c0[k][d], CHUNK), cs] for (k, d, _), cs in zip(ch, cols)]
        vf = [vs[pl.ds(c0[k][d], CHUNK), cs].astype(F32) for (k, d, _), cs in zip(ch, cols)]
        qf = [a.astype(F32) for a in qb]
        kf = [a.astype(F32) for a in kb]
        gc = [comb[k][decay_row(d, h):decay_row(d, h) + 1, :] for k, d, h in ch]
        beta_col = [jnp.broadcast_to(comb_t[k][:, beta_row(d, h):beta_row(d, h) + 1], (CHUNK, CHUNK))
                    for k, d, h in ch]
        gc_col = [jnp.broadcast_to(comb_t[k][:, decay_row(d, h):decay_row(d, h) + 1], (CHUNK, CHUNK))
                  for k, d, h in ch]
        kk = [_nt(a, a) for a in kb]
        qk = [_nt(a, b) for a, b in zip(qb, kb)]
        dec = [jnp.where(incl[d], jnp.exp(jnp.where(incl[d], c - r, 0.0)), 0.0)
               for c, r, (_, d, _) in zip(gc_col, gc, ch)]
        nm = [jnp.where(strict[d], a * b * c, 0.0) for a, b, c, (_, d, _) in zip(kk, dec, beta_col, ch)]
        t_inv = _tri_inverse(nm, [d for _, d, _ in ch], sub, lane, eye_f)
        eg = [jnp.exp(a) for a in gc_col]
        rhs = [jnp.concatenate([a * b, c * (b * e)], axis=1).astype(BF16)
               for a, b, c, e in zip(vf, beta_col, kf, eg)]
        sol = [_mm(a.astype(BF16), b) for a, b in zip(t_inv, rhs)]
        tot = [a[:, CHUNK - 1:CHUNK] if d == 0 else a[:, 0:1] for a, (_, d, _) in zip(gc, ch)]
        q_dec = [(a * e).astype(BF16) for a, e in zip(qf, eg)]
        k_dec = [(a * jnp.exp(t - c)).astype(BF16) for a, t, c in zip(kf, tot, gc_col)]
        intra = [(a * b).astype(BF16) for a, b in zip(qk, dec)]
        state = [st[n] for n in range(nseq * nc)]
        for k in range(npar):
            idx = [(sq * npar + k) * nc + ci for sq in range(nseq) for ci in range(nc)]
            sidx = [sq * nc + ci for sq in range(nseq) for ci in range(nc)]
            sb = [state[n].astype(BF16) for n in sidx]
            v_new = [sol[m][:, :CHUNK] - _mm(sol[m][:, CHUNK:].astype(BF16), b) for m, b in zip(idx, sb)]
            vb = [a.astype(BF16) for a in v_new]
            o = [_mm(jnp.concatenate([q_dec[m], intra[m]], axis=1), jnp.concatenate([b, e], axis=0))
                 for m, b, e in zip(idx, sb, vb)]
            for m, n, e, ov in zip(idx, sidx, vb, o):
                state[n] = state[n] * jnp.exp(tot[m]) + _tn(k_dec[m], e)
                slot, d, h = ch[m]
                rows = pl.ds(c0[slot][d], CHUNK)
                cs = slice(h * LANES, (h + 1) * LANES)
                if not second:
                    oacc[rows, cs] = ov
                else:
                    ot = oacc[rows, cs] + ov
                    ot = ot * lax.rsqrt(jnp.mean(ot * ot, axis=-1, keepdims=True) + EPS) * nw_ref[...]
                    o_ref[rows, cs] = (ot * _silu(z_ref[rows, cs].astype(F32))).astype(o_ref.dtype)
        for n in range(nseq * nc):
            st[n] = state[n]

    lax.fori_loop(0, half // npar, lambda i, c: (step(i, False), c)[1], 0)
    lax.fori_loop(half // npar, n_chunks // npar, lambda i, c: (step(i, True), c)[1], 0)

    if sout_ref is not None:
        if n_prev:
            sout_ref[:, 0:n_prev] = prev_ref[...]
        for sq in range(nseq):
            for ci, (d, h) in enumerate(chains):
                sout_ref[sq, n_prev, d, h] = st[sq * nc + ci]


def _stacked_state_specs(prev, nseq, tail, index_map):
    n_prev = 0 if prev is None else prev.shape[1]
    in_spec = None if prev is None else pl.BlockSpec((nseq, n_prev) + tail, index_map)
    return n_prev, in_spec, pl.BlockSpec((nseq, n_prev + 1) + tail, index_map)


def _gdn(p, conv_w, norm_w, a_log, dt_bias, s0, prev, *, batch, t_len):
    n = batch * t_len
    has_ctx = s0 is not None
    assert (t_len // CHUNK) % 2 == 0
    nseq = _seqs_per_step(batch, t_len, GDN_SLOTS)
    rows = nseq * t_len
    big = dict(pipeline_mode=pl.Buffered(1)) if rows * GDN_WIDTH * 2 >= (2 << 20) else {}
    seq = lambda j: pl.BlockSpec((rows, GDN_WIDTH), lambda b, j=j: (b, j), **big)
    cw = lambda j: pl.BlockSpec((3, GDN_WIDTH), lambda b, j=j: (0, j))
    gate_rows = 2 * GDN_HEADS
    first = 2 * SSM_HEADS // gate_rows
    par = pl.BlockSpec((2 * gate_rows, 1), lambda b: (0, 0))
    in_specs = [seq(0), seq(1), seq(2), seq(0),
                pl.BlockSpec((gate_rows, rows), lambda b: (first, b)),
                pl.BlockSpec((gate_rows, rows), lambda b: (first + 1, b)),
                cw(0), cw(1), cw(2),
                pl.BlockSpec((1, LANES), lambda b: (0, 0)), par, par]
    args = [p["qkv"], p["qkv"], p["qkv"], p["az"], p["small_t"], p["small_t"], conv_w, conv_w, conv_w,
            norm_w, a_log, dt_bias]
    out_shape = [jax.ShapeDtypeStruct((n, GDN_WIDTH), BF16)]
    out_specs = [pl.BlockSpec((rows, GDN_WIDTH), lambda b: (b, 0))]
    n_prev = 0
    if has_ctx:
        s0_all, layer = s0
        in_specs.append(pl.BlockSpec((nseq, None, 2, GDN_HEADS, GDN_DK, CHUNK),
                                     lambda b: (b, layer, 0, 0, 0, 0)))
        args.append(s0_all)
    else:
        tail = (2, GDN_HEADS, GDN_DK, CHUNK)
        n_prev, prev_spec, out_spec = _stacked_state_specs(prev, nseq, tail, lambda b: (b, 0, 0, 0, 0, 0))
        if n_prev:
            in_specs.append(prev_spec)
            args.append(prev)
        out_shape.append(jax.ShapeDtypeStruct((batch, n_prev + 1) + tail, F32))
        out_specs.append(out_spec)
    scratch = [pltpu.VMEM((rows, GDN_WIDTH), BF16) for _ in range(3)]
    scratch += [pltpu.VMEM((rows, GDN_WIDTH), F32), pltpu.VMEM((nseq * 2 * GDN_HEADS, GDN_DK, CHUNK), F32)]
    outs = pl.pallas_call(
        functools.partial(_gdn_kernel, t_len=t_len, nseq=nseq, has_ctx=has_ctx, n_prev=n_prev),
        grid=(batch // nseq,),
        in_specs=in_specs, out_specs=out_specs, out_shape=out_shape, scratch_shapes=scratch,
        compiler_params=pltpu.CompilerParams(
            dimension_semantics=("arbitrary",), vmem_limit_bytes=VMEM_LIMIT),
        name="gdn_ctx" if has_ctx else "gdn",
    )(*args)
    return outs[0], (None if has_ctx else outs[1])


SSD_GROUP_HEADS = 8
SSD_PAIRS = SSD_GROUP_HEADS // 2
SSD_GROUP_WIDTH = SSD_GROUP_HEADS * SSM_HEAD_DIM


def _ssd_kernel(*refs, t_len, nseq, has_ctx, n_prev):
    (x_ref, b_ref, c_ref, z_ref, dtf_ref, dtb_ref, wx_ref, wb_ref, wc_ref, bx_ref, bb_ref, bc_ref,
     dvec_ref, alog_ref, bias_ref) = refs[:15]
    rest = refs[15:]
    s0_ref = prev_ref = sout_ref = None
    if has_ctx:
        s0_ref, o_ref, xs, bs, cs, yacc, st = rest
    else:
        if n_prev:
            prev_ref, rest = rest[0], rest[1:]
        o_ref, sout_ref, xs, bs, cs, yacc, st = rest
    rt = _row_tile(t_len)
    n_tiles = t_len // rt
    n_chunks = t_len // CHUNK
    chains = [(d, q) for d in range(2) for q in range(SSD_PAIRS)]

    def prep(r, carry):
        start = pl.multiple_of(r * rt, rt)
        for q in range(SSD_PAIRS):
            off = q * LANES
            xs[pl.ds(start, rt), off:off + LANES] = _silu(_conv_tile(
                x_ref, r, rt, n_tiles, wx_ref[:, off:off + LANES], bx_ref[:, off:off + LANES], off)).astype(BF16)
        bs[pl.ds(start, rt), :] = _silu(_conv_tile(b_ref, r, rt, n_tiles, wb_ref[...], bb_ref[...], 0)).astype(BF16)
        cs[pl.ds(start, rt), :] = _silu(_conv_tile(c_ref, r, rt, n_tiles, wc_ref[...], bc_ref[...], 0)).astype(BF16)
        return carry

    lax.fori_loop(0, nseq * n_tiles, prep, 0)

    nc = len(chains)
    for sq in range(nseq):
        for ci, (d, q) in enumerate(chains):
            if has_ctx:
                st[sq * nc + ci] = jnp.concatenate([s0_ref[sq, d, 2 * q], s0_ref[sq, d, 2 * q + 1]], axis=0)
            else:
                st[sq * nc + ci] = jnp.zeros((2 * SSM_HEAD_DIM, SSM_STATE), F32)

    sub = lax.broadcasted_iota(jnp.int32, (CHUNK, CHUNK), 0)
    lane = lax.broadcasted_iota(jnp.int32, (CHUNK, CHUNK), 1)
    eye = sub == lane
    incl = [lane <= sub, lane >= sub]
    head0_lane = lane < SSM_HEAD_DIM
    head0_sub = sub < SSM_HEAD_DIM
    eye_b = eye.astype(BF16)
    tri = [(sub <= lane).astype(BF16), (sub >= lane).astype(BF16)]
    is_fwd = lax.broadcasted_iota(jnp.int32, (2 * SSD_GROUP_HEADS, CHUNK), 0) < SSD_GROUP_HEADS
    lane_r = lax.broadcasted_iota(jnp.int32, (2 * SSD_GROUP_HEADS, CHUNK), 1)
    neg_a = -jnp.exp(alog_ref[...])
    hrow = lambda d, q, j: d * SSD_GROUP_HEADS + 2 * q + j

    half = n_chunks // 2
    npar = _positions_per_body(half, SSD_SLOTS)

    def step(i, second_half):
        slots = [(sq, k) for sq in range(nseq) for k in range(npar)]
        c0 = [[pl.multiple_of(sq * t_len + (i * npar + k) * CHUNK, CHUNK),
               pl.multiple_of(sq * t_len + (n_chunks - 1 - (i * npar + k)) * CHUNK, CHUNK)] for sq, k in slots]
        kd = [(k, d) for k in range(len(slots)) for d in range(2)]
        bm = {(k, d): bs[pl.ds(c0[k][d], CHUNK), :] for k, d in kd}
        cm = {(k, d): cs[pl.ds(c0[k][d], CHUNK), :] for k, d in kd}
        cb = {key: _nt(cm[key], bm[key]) for key in kd}
        raw = [jnp.concatenate([dtf_ref[:, pl.ds(c[0], CHUNK)], dtb_ref[:, pl.ds(c[1], CHUNK)]], axis=0)
               for c in c0]
        dt_all = [_softplus(r + bias_ref[...]) for r in raw]
        da_all = [a * neg_a for a in dt_all]
        acum_all = [jnp.where(is_fwd, _cumsum_rows(a, tri[0]), _cumsum_rows(a, tri[1])) for a in da_all]
        tot_all = [jnp.sum(jnp.where(lane_r == jnp.where(is_fwd, CHUNK - 1, 0), a, 0.0), axis=1, keepdims=True)
                   for a in acum_all]
        w_all = [a * jnp.exp(t - c) for a, t, c in zip(dt_all, tot_all, acum_all)]
        cols_t = [_rows_to_cols(a, eye_b) for a in acum_all]
        cd_all = [jnp.exp(t) for t in tot_all]
        shifted_all = [a - jnp.log(t) for a, t in zip(acum_all, dt_all)]
        order = [(sq * npar + k, sq, d, q) for k in range(npar) for sq in range(nseq) for d, q in chains]
        state = {(sq, d, q): st[sq * nc + ci] for sq in range(nseq) for ci, (d, q) in enumerate(chains)}

        def first(slot, sq, d, q):
            cs_ = slice(q * LANES, (q + 1) * LANES)
            xb = xs[pl.ds(c0[slot][d], CHUNK), cs_]
            x01 = [jnp.where(head0_lane, xb, jnp.zeros_like(xb)), jnp.where(head0_lane, jnp.zeros_like(xb), xb)]
            r = [hrow(d, q, j) for j in range(2)]
            acum_b = [jnp.broadcast_to(cols_t[slot][:, a:a + 1], (CHUNK, CHUNK)) for a in r]
            m_mat = [(cb[(slot, d)] * jnp.where(incl[d], jnp.exp(c - shifted_all[slot][a:a + 1, :]), 0.0)
                      ).astype(BF16) for c, a in zip(acum_b, r)]
            w_diag = [jnp.where(eye, w_all[slot][a:a + 1, :], 0.0).astype(BF16) for a in r]
            y = _mm(m_mat[0], x01[0]) + _mm(m_mat[1], x01[1])
            xw = _mm(w_diag[0], x01[0]) + _mm(w_diag[1], x01[1])
            din = jnp.where(head0_lane, jnp.exp(acum_b[0]), jnp.exp(acum_b[1]))
            cd = jnp.where(head0_sub, cd_all[slot][r[0]:r[0] + 1, :], cd_all[slot][r[1]:r[1] + 1, :])
            return xb, y, xw, din, cd

        def second(slot, sq, d, q, xb, y, xw, din, cd):
            cs_ = slice(q * LANES, (q + 1) * LANES)
            s_old = state[(sq, d, q)]
            yk = y + _nt(cm[(slot, d)], s_old.astype(BF16)) * din
            state[(sq, d, q)] = s_old * cd + _tn(xw.astype(BF16), bm[(slot, d)])
            rows = pl.ds(c0[slot][d], CHUNK)
            if not second_half:
                yacc[rows, cs_] = yk
            else:
                yt = yacc[rows, cs_] + yk + dvec_ref[:, cs_] * xb.astype(F32)
                o_ref[rows, cs_] = (yt * _silu(z_ref[rows, cs_].astype(F32))).astype(o_ref.dtype)

        pending = None
        for item in order:
            res = first(*item)
            if pending is not None:
                second(*pending)
            pending = item + res
        second(*pending)
        for sq in range(nseq):
            for ci, (d, q) in enumerate(chains):
                st[sq * nc + ci] = state[(sq, d, q)]

    lax.fori_loop(0, half // npar, lambda i, c: (step(i, False), c)[1], 0)
    lax.fori_loop(half // npar, n_chunks // npar, lambda i, c: (step(i, True), c)[1], 0)

    if sout_ref is not None:
        if n_prev:
            sout_ref[:, 0:n_prev] = prev_ref[...]
        for sq in range(nseq):
            for ci, (d, q) in enumerate(chains):
                s = st[sq * nc + ci]
                sout_ref[sq, n_prev, d, 2 * q] = s[:SSM_HEAD_DIM]
                sout_ref[sq, n_prev, d, 2 * q + 1] = s[SSM_HEAD_DIM:]


def _ssd(p, conv_w, conv_b, d_vec, a_log, dt_bias, s0, prev, *, batch, t_len):
    n = batch * t_len
    has_ctx = s0 is not None
    assert (t_len // CHUNK) % 2 == 0
    n_groups = SSM_HEADS // SSD_GROUP_HEADS
    nseq = _seqs_per_step(batch, t_len, SSD_SLOTS)
    rows = nseq * t_len
    x_blocks = D_INNER // LANES
    big = dict(pipeline_mode=pl.Buffered(1)) if rows * SSD_GROUP_WIDTH * 2 >= (2 << 20) else {}
    par = pl.BlockSpec((None, 2 * SSD_GROUP_HEADS, 1), lambda b, g: (g, 0, 0))
    in_specs = [pl.BlockSpec((rows, SSD_GROUP_WIDTH), lambda b, g: (b, g), **big),
                pl.BlockSpec((rows, LANES), lambda b, g: (b, x_blocks + g)),
                pl.BlockSpec((rows, LANES), lambda b, g: (b, x_blocks + n_groups + g)),
                pl.BlockSpec((rows, SSD_GROUP_WIDTH), lambda b, g: (b, g), **big),
                pl.BlockSpec((SSD_GROUP_HEADS, rows), lambda b, g: (g, b)),
                pl.BlockSpec((SSD_GROUP_HEADS, rows), lambda b, g: (n_groups + g, b)),
                pl.BlockSpec((3, SSD_GROUP_WIDTH), lambda b, g: (0, g)),
                pl.BlockSpec((3, LANES), lambda b, g: (0, x_blocks + g)),
                pl.BlockSpec((3, LANES), lambda b, g: (0, x_blocks + n_groups + g)),
                pl.BlockSpec((1, SSD_GROUP_WIDTH), lambda b, g: (0, g)),
                pl.BlockSpec((1, LANES), lambda b, g: (0, x_blocks + g)),
                pl.BlockSpec((1, LANES), lambda b, g: (0, x_blocks + n_groups + g)),
                pl.BlockSpec((1, SSD_GROUP_WIDTH), lambda b, g: (0, g)), par, par]
    args = [p["xbc"], p["xbc"], p["xbc"], p["cz"], p["small_t"], p["small_t"], conv_w, conv_w, conv_w,
            conv_b, conv_b, conv_b, d_vec, a_log, dt_bias]
    out_shape = [jax.ShapeDtypeStruct((n, D_INNER), BF16)]
    out_specs = [pl.BlockSpec((rows, SSD_GROUP_WIDTH), lambda b, g: (b, g))]
    n_prev = 0
    if has_ctx:
        s0_all, layer = s0
        in_specs.append(pl.BlockSpec((nseq, None, 2, SSD_GROUP_HEADS, SSM_HEAD_DIM, SSM_STATE),
                                     lambda b, g: (b, layer, 0, g, 0, 0)))
        args.append(s0_all)
    else:
        n_prev, prev_spec, out_spec = _stacked_state_specs(
            prev, nseq, (2, SSD_GROUP_HEADS, SSM_HEAD_DIM, SSM_STATE), lambda b, g: (b, 0, 0, g, 0, 0))
        if n_prev:
            in_specs.append(prev_spec)
            args.append(prev)
        out_shape.append(jax.ShapeDtypeStruct((batch, n_prev + 1, 2, SSM_HEADS, SSM_HEAD_DIM, SSM_STATE), F32))
        out_specs.append(out_spec)
    scratch = [pltpu.VMEM((rows, SSD_GROUP_WIDTH), BF16), pltpu.VMEM((rows, LANES), BF16),
               pltpu.VMEM((rows, LANES), BF16), pltpu.VMEM((rows, SSD_GROUP_WIDTH), F32),
               pltpu.VMEM((nseq * 2 * SSD_PAIRS, 2 * SSM_HEAD_DIM, SSM_STATE), F32)]
    outs = pl.pallas_call(
        functools.partial(_ssd_kernel, t_len=t_len, nseq=nseq, has_ctx=has_ctx, n_prev=n_prev),
        grid=(batch // nseq, n_groups),
        in_specs=in_specs, out_specs=out_specs, out_shape=out_shape, scratch_shapes=scratch,
        compiler_params=pltpu.CompilerParams(
            dimension_semantics=("arbitrary", "arbitrary"), vmem_limit_bytes=VMEM_LIMIT),
        name="ssd_ctx" if has_ctx else "ssd",
    )(*args)
    return outs[0], (None if has_ctx else outs[1])


KEY_TILE = 256


def _seg_rmsnorm(x, bd):
    x2 = x * x
    hi = x2.astype(BF16)
    lo = (x2 - hi.astype(F32)).astype(BF16)
    ms = _mm(hi, bd) + _mm(lo, bd)
    return x * lax.rsqrt(ms + EPS)


def _rope128(x, cos, sin_signed):
    lane = lax.broadcasted_iota(jnp.int32, x.shape, 1)
    first = (lane % 32) < 16
    swapped = jnp.where(first, pltpu.roll(x, LANES - 16, 1), pltpu.roll(x, 16, 1))
    return x * cos + swapped * sin_signed


def _attn_kernel(*refs, t_len, tq, past, has_ctx, n_prev):
    q_ref, kv_ref, z_ref, qnw_ref, knw_ref, bd_ref = refs[:6]
    rest = refs[6:]
    kn_ref = vn_ref = pk_ref = pv_ref = None
    if has_ctx:
        cosq_ref, sinq_ref, cosk_ref, sink_ref, ck_ref, cv_ref, o_ref, k_s, v_s = rest
    else:
        if n_prev:
            pk_ref, pv_ref, rest = rest[0], rest[1], rest[2:]
        o_ref, kn_ref, vn_ref, k_s, v_s = rest
    qi = pl.program_id(1)
    rt = _row_tile(t_len)
    n_tiles = t_len // rt

    def store_values(rows, v):
        low_v = lax.broadcasted_iota(jnp.int32, v.shape, 1) < HEAD_DIM
        v_s[0, rows, :] = jnp.where(low_v, v, 1.0).astype(BF16)
        v_s[1, rows, :] = jnp.where(low_v, 1.0, v).astype(BF16)

    @pl.when(qi == 0)
    def _():
        def prep(r, carry):
            start = pl.multiple_of(r * rt, rt)
            kv = kv_ref[pl.ds(start, rt), :]
            kn = _seg_rmsnorm(kv[:, :KV_WIDTH], bd_ref[0:KV_WIDTH, 0:KV_WIDTH]) * knw_ref[...]
            if kn_ref is not None:
                kn_ref[n_prev, pl.ds(start, rt), :] = kn
                vn_ref[n_prev, pl.ds(start, rt), :] = kv[:, KV_WIDTH:]
            if has_ctx:
                kn = _rope128(kn, cosk_ref[pl.ds(start, rt), :], sink_ref[pl.ds(start, rt), :])
            k_s[pl.ds(start, rt), :] = kn.astype(BF16)
            store_values(pl.ds(start, rt), kv[:, KV_WIDTH:])
            return carry

        lax.fori_loop(0, n_tiles, prep, 0)
        if n_prev:
            kn_ref[0:n_prev] = pk_ref[...]
            vn_ref[0:n_prev] = pv_ref[...]
        if has_ctx:
            k_s[t_len:t_len + past, :] = ck_ref[...].astype(BF16)
            store_values(slice(t_len, t_len + past), cv_ref[...])

    q = _seg_rmsnorm(q_ref[...].astype(F32), bd_ref[...]) * qnw_ref[...]
    blocks = []
    for cbk in range(ATTN_WIDTH // LANES):
        blk = q[:, cbk * LANES:(cbk + 1) * LANES]
        if has_ctx:
            blk = _rope128(blk, cosq_ref[...], sinq_ref[...])
        blocks.append((blk * (HEAD_DIM ** -0.5)).astype(BF16))
    lane = lax.broadcasted_iota(jnp.int32, (tq, LANES), 1)
    low = lane < HEAD_DIM
    group = N_HEADS // KV_HEADS
    head_out = [None] * N_HEADS
    qgs = []
    for kvh in range(KV_HEADS):
        keep = low if kvh == 0 else jnp.logical_not(low)
        rows = []
        for g in range(group):
            head = kvh * group + g
            blk = blocks[head // 2]
            if head % 2 != kvh:
                blk = pltpu.roll(blk, HEAD_DIM, 1)
            rows.append(jnp.where(keep, blk, jnp.zeros_like(blk)))
        qgs.append(jnp.concatenate(rows, axis=0))

    n_kt = (t_len + past) // KEY_TILE
    keys = lambda j: slice(j * KEY_TILE, (j + 1) * KEY_TILE)
    score_tile = lambda kvh, j: _nt(qgs[kvh], k_s[keys(j), :]).astype(BF16)
    row_max = lambda tiles: jnp.max(functools.reduce(jnp.maximum, tiles), axis=-1, keepdims=True)
    s_tiles = [[score_tile(0, j) for j in range(n_kt)], []]
    m = [row_max(s_tiles[0]), None]
    p_tiles = [[], []]

    def exp_tile(kvh, j):
        p_tiles[kvh].append(jnp.exp(s_tiles[kvh][j] - m[kvh]))

    acc = [None, None]

    def pv_tile(kvh, j):
        part = _mm(p_tiles[kvh][j], v_s[kvh, keys(j), :])
        acc[kvh] = part if acc[kvh] is None else acc[kvh] + part

    for j in range(n_kt):
        s_tiles[1].append(score_tile(1, j))
        exp_tile(0, j)
    m[1] = row_max(s_tiles[1])
    for j in range(n_kt):
        pv_tile(0, j)
        exp_tile(1, j)
    for j in range(n_kt):
        pv_tile(1, j)
    for kvh in range(KV_HEADS):
        ones_lane = HEAD_DIM if kvh == 0 else 0
        o = acc[kvh] / acc[kvh][:, ones_lane:ones_lane + 1]
        for g in range(group):
            head = kvh * group + g
            og = o[g * tq:(g + 1) * tq, :]
            if head % 2 != kvh:
                og = pltpu.roll(og, HEAD_DIM, 1)
            head_out[head] = og
    outs = [jnp.where(low, head_out[2 * i], head_out[2 * i + 1]) for i in range(N_HEADS // 2)]
    o_all = jnp.concatenate(outs, axis=1)
    o_ref[...] = (o_all * _silu(z_ref[...].astype(F32))).astype(o_ref.dtype)


def _attention(p, qnw, knw, bd, rope_tabs, cache_k, cache_v, prev_kv, *, batch, t_len):
    n = batch * t_len
    has_ctx = cache_k is not None
    past = cache_k.shape[1] if has_ctx else 0
    n_prev = 0
    tq = min(t_len, 256)
    nq = t_len // tq
    once = dict(pipeline_mode=pl.Buffered(1)) if t_len * 2 * KV_WIDTH * 4 >= (2 << 20) else {}
    in_specs = [pl.BlockSpec((tq, ATTN_WIDTH), lambda b, i: (b * nq + i, 0)),
                pl.BlockSpec((t_len, 2 * KV_WIDTH), lambda b, i: (b, 0), **once),
                pl.BlockSpec((tq, ATTN_WIDTH), lambda b, i: (b * nq + i, 0)),
                pl.BlockSpec((1, ATTN_WIDTH), lambda b, i: (0, 0)),
                pl.BlockSpec((1, KV_WIDTH), lambda b, i: (0, 0)),
                pl.BlockSpec((ATTN_WIDTH, ATTN_WIDTH), lambda b, i: (0, 0))]
    args = [p["bq"], p["kv"], p["bz"], qnw, knw, bd]
    out_shape = [jax.ShapeDtypeStruct((n, ATTN_WIDTH), BF16)]
    out_specs = [pl.BlockSpec((tq, ATTN_WIDTH), lambda b, i: (b * nq + i, 0))]
    if has_ctx:
        cos, sin = rope_tabs
        in_specs += [pl.BlockSpec((tq, LANES), lambda b, i: (i, 0)),
                     pl.BlockSpec((tq, LANES), lambda b, i: (i, 0)),
                     pl.BlockSpec((t_len, LANES), lambda b, i: (0, 0), **once),
                     pl.BlockSpec((t_len, LANES), lambda b, i: (0, 0), **once),
                     pl.BlockSpec((None, past, KV_WIDTH), lambda b, i: (b, 0, 0)),
                     pl.BlockSpec((None, past, KV_WIDTH), lambda b, i: (b, 0, 0))]
        args += [cos, sin, cos, sin, cache_k, cache_v]
    else:
        n_prev = 0 if prev_kv is None else prev_kv[0].shape[1]
        if n_prev:
            in_specs += [pl.BlockSpec((None, n_prev, t_len, KV_WIDTH), lambda b, i: (b, 0, 0, 0))] * 2
            args += list(prev_kv)
        out_shape += [jax.ShapeDtypeStruct((batch, n_prev + 1, t_len, KV_WIDTH), F32)] * 2
        out_specs += [pl.BlockSpec((None, n_prev + 1, t_len, KV_WIDTH), lambda b, i: (b, 0, 0, 0))] * 2
    scratch = [pltpu.VMEM((t_len + past, KV_WIDTH), BF16),
               pltpu.VMEM((KV_HEADS, t_len + past, KV_WIDTH), BF16)]
    outs = pl.pallas_call(
        functools.partial(_attn_kernel, t_len=t_len, tq=tq, past=past, has_ctx=has_ctx, n_prev=n_prev),
        grid=(batch, nq),
        in_specs=in_specs, out_specs=out_specs, out_shape=out_shape, scratch_shapes=scratch,
        compiler_params=pltpu.CompilerParams(
            dimension_semantics=("arbitrary", "arbitrary"), vmem_limit_bytes=VMEM_LIMIT),
        name="attn_ctx" if has_ctx else "attn",
    )(*args)
    return outs[0], (None if has_ctx else (outs[1], outs[2]))


def _out_kernel(x_ref, oa_ref, ob_ref, yc_ref, g_ref, mod_ref, snw_ref, wa_ref, wb_ref, wc_ref, wo_ref,
                fnw_ref, o_ref, *, final):
    ba = _mm(oa_ref[...], wa_ref[...])
    bb = _mm(ob_ref[...], wb_ref[...])
    yc = yc_ref[...].astype(F32)
    yn = yc * lax.rsqrt(jnp.mean(yc * yc, axis=-1, keepdims=True) + EPS) * snw_ref[...]
    bc = _mm(yn.astype(BF16), wc_ref[...])
    g = g_ref[...].astype(F32)
    merged = (_sigmoid(g[:, 0:D_MODEL]) * ba + _sigmoid(g[:, D_MODEL:2 * D_MODEL]) * bb
              + _sigmoid(g[:, 2 * D_MODEL:]) * bc)
    out = _mm(merged.astype(BF16), wo_ref[...])
    xn = x_ref[...] + mod_ref[:, 2 * D_MODEL:] * out
    if final:
        xn = xn * lax.rsqrt(jnp.mean(xn * xn, axis=-1, keepdims=True) + EPS) * fnw_ref[...]
    o_ref[...] = xn


def _out_proj(x, o_a, o_b, y_c, gates, mod, ssm_norm_w, wa, wb, wc, wo, final_norm_w, *,
              t_len, mod_base, mod_per_batch, final):
    n = x.shape[0]
    tm = 512 if t_len % 512 == 0 else 256
    assert n % tm == 0 and t_len % tm == 0

    def mod_idx(i):
        return (mod_base + (i * tm) // t_len if mod_per_batch else mod_base, 0, 0)

    tok = lambda w: pl.BlockSpec((tm, w), lambda i: (i, 0))
    full = lambda a, b: pl.BlockSpec((a, b), lambda i: (0, 0))
    return pl.pallas_call(
        functools.partial(_out_kernel, final=final),
        grid=(n // tm,),
        in_specs=[tok(D_MODEL), tok(GDN_WIDTH), tok(ATTN_WIDTH), tok(D_INNER), tok(3 * D_MODEL),
                  pl.BlockSpec((None, 1, 3 * D_MODEL), mod_idx), full(1, D_INNER),
                  full(GDN_WIDTH, D_MODEL), full(ATTN_WIDTH, D_MODEL), full(D_INNER, D_MODEL),
                  full(D_MODEL, D_MODEL), full(1, D_MODEL)],
        out_specs=tok(D_MODEL),
        out_shape=jax.ShapeDtypeStruct((n, D_MODEL), F32),
        compiler_params=pltpu.CompilerParams(
            dimension_semantics=("arbitrary",), vmem_limit_bytes=VMEM_LIMIT),
        name="out_proj",
    )(x, o_a, o_b, y_c, gates, mod, ssm_norm_w, wa, wb, wc, wo, final_norm_w)


def _wprep_kernel(w_ref, o_ref):
    for name in _MAIN:
        src, width = _SRC[name]
        off = _MAIN_OFF[name]
        o_ref[:, off:off + width] = w_ref[:, src:src + width].astype(BF16)


def _permute_in_weights(w_in):
    depth, d_model, in_width = w_in.shape
    tr = 128
    return pl.pallas_call(
        _wprep_kernel,
        grid=(depth, d_model // tr),
        in_specs=[pl.BlockSpec((None, tr, in_width), lambda l, i: (l, i, 0))],
        out_specs=pl.BlockSpec((None, tr, MAIN_WIDTH), lambda l, i: (l, i, 0)),
        out_shape=jax.ShapeDtypeStruct((depth, d_model, MAIN_WIDTH), BF16),
        name="weight_prep",
    )(w_in)


def _rope_tables(t_len):
    rot = HEAD_DIM // 2
    pos = jnp.arange(t_len)
    freqs = ROPE_BASE ** (-jnp.arange(rot // 2, dtype=F32) / (rot // 2))
    ang_row = (pos // GRID_W).astype(F32)[:, None] * freqs
    ang_col = (pos % GRID_W).astype(F32)[:, None] * freqs
    ang = jnp.concatenate([ang_row, ang_row, ang_col, ang_col], axis=1)
    sign = jnp.tile(jnp.concatenate([-jnp.ones((rot // 2,), F32), jnp.ones((rot // 2,), F32)]), 2)
    cos = jnp.tile(jnp.cos(ang), (1, 2))
    sin = jnp.tile(jnp.sin(ang) * sign, (1, 2))
    return cos, sin


def _layer_params(l, w_in, gdn_conv_w, gdn_a_log, gdn_dt_bias, gdn_norm_w, attn_q_norm, attn_k_norm,
                  ssm_conv_w, ssm_conv_b, ssm_a_log, ssm_dt_bias, ssm_d, ssm_norm_w,
                  w_branch_a, w_branch_b, w_branch_c, w_out):
    sl = lambda name: w_in[l, :, _SRC[name][0]:_SRC[name][0] + _SRC[name][1]]
    beta, alpha = sl("beta"), sl("alpha")
    hd = GDN_HEADS
    w_small_t = jnp.concatenate([sl("dt"), beta[:, :hd], alpha[:, :hd], beta[:, hd:], alpha[:, hd:]],
                                axis=1).T.astype(BF16)
    zeros = jnp.zeros((hd,), F32)

    def gdn_col(a):
        return jnp.concatenate([zeros, a[0], zeros, a[1]]).reshape(4 * hd, 1)

    def ssd_col(a):
        g = a.reshape(2, SSM_HEADS // SSD_GROUP_HEADS, SSD_GROUP_HEADS)
        return jnp.concatenate([g[0], g[1]], axis=1)[..., None]

    return dict(
        w_small_t=w_small_t,
        gdn_conv_w=gdn_conv_w[l], gdn_a_log=gdn_col(gdn_a_log[l]), gdn_dt_bias=gdn_col(gdn_dt_bias[l]),
        gdn_norm_w=gdn_norm_w[l].reshape(1, LANES),
        qnw=jnp.tile(attn_q_norm[l], N_HEADS).reshape(1, ATTN_WIDTH),
        knw=jnp.tile(attn_k_norm[l], KV_HEADS).reshape(1, KV_WIDTH),
        ssm_conv_w=ssm_conv_w[l], ssm_conv_b=ssm_conv_b[l].reshape(1, SSM_XBC),
        ssm_a_log=ssd_col(ssm_a_log[l]), ssm_dt_bias=ssd_col(ssm_dt_bias[l]),
        ssm_d=jnp.repeat(ssm_d[l], SSM_HEAD_DIM).reshape(1, D_INNER),
        ssm_norm_w=ssm_norm_w[l].reshape(1, D_INNER),
        wa=w_branch_a[l].astype(BF16), wb=w_branch_b[l].astype(BF16),
        wc=w_branch_c[l].astype(BF16), wo=w_out[l].astype(BF16))


def _stream_layer(x, lp, mod_l, norm_w_l, final_norm_w, bd, rope_tabs, ctx, prev, *,
                  batch, t_len, mod_base, mod_per_batch, final):
    p = _in_proj(x, norm_w_l, mod_l, lp["w_main"], lp["w_small_t"],
                 t_len=t_len, mod_base=mod_base, mod_per_batch=mod_per_batch)
    ck, cv, sg, ss = ctx if ctx is not None else (None, None, None, None)
    prev_kv, prev_g, prev_s = prev if prev is not None else (None, None, None)
    o_a, new_sg = _gdn(p, lp["gdn_conv_w"], lp["gdn_norm_w"], lp["gdn_a_log"], lp["gdn_dt_bias"], sg, prev_g,
                       batch=batch, t_len=t_len)
    y_c, new_ss = _ssd(p, lp["ssm_conv_w"], lp["ssm_conv_b"], lp["ssm_d"], lp["ssm_a_log"],
                       lp["ssm_dt_bias"], ss, prev_s, batch=batch, t_len=t_len)
    o_b, new_kv = _attention(p, lp["qnw"], lp["knw"], bd, rope_tabs, ck, cv, prev_kv, batch=batch, t_len=t_len)
    x_new = _out_proj(x, o_a, o_b, y_c, p["gates"], mod_l, lp["ssm_norm_w"], lp["wa"], lp["wb"],
                      lp["wc"], lp["wo"], final_norm_w, t_len=t_len, mod_base=mod_base,
                      mod_per_batch=mod_per_batch, final=final)
    return x_new, (new_kv, new_sg, new_ss)


def kernel(x_prompt, x_sample, cache_k, cache_v, state_gdn, state_ssm, c, c_ctx, norm_w, w_mod, b_mod, w_in, gdn_conv_w, gdn_a_log, gdn_dt_bias, gdn_norm_w, attn_q_norm, attn_k_norm, ssm_conv_w, ssm_conv_b, ssm_a_log, ssm_dt_bias, ssm_d, ssm_norm_w, w_branch_a, w_branch_b, w_branch_c, w_out, final_norm_w):
    batch, seq, d_model = x_prompt.shape
    dec_batch, dec_seq, _ = x_sample.shape
    depth = w_in.shape[0]
    past = cache_k.shape[2]
    assert d_model == D_MODEL and seq % CHUNK == 0 and dec_seq % CHUNK == 0

    rows = -(-(1 + dec_batch) // 8) * 8
    cond = jnp.zeros((rows, D_MODEL), F32).at[0].set(c_ctx).at[1:1 + dec_batch].set(c)
    mod = _modulation(cond, w_mod, b_mod).reshape(depth, rows, 1, 3 * D_MODEL)

    seg = jnp.arange(ATTN_WIDTH) // HEAD_DIM
    bd = jnp.where(seg[:, None] == seg[None, :], 1.0 / HEAD_DIM, 0.0).astype(BF16)
    rope_tabs = _rope_tables(dec_seq)
    fnw = final_norm_w.reshape(1, D_MODEL)

    xp = x_prompt.reshape(batch * seq, D_MODEL)
    xs = x_sample.reshape(dec_batch * dec_seq, D_MODEL)
    new_cache = None
    w_perm = _permute_in_weights(w_in)
    for l in range(depth):
        lp = _layer_params(l, w_in, gdn_conv_w, gdn_a_log, gdn_dt_bias, gdn_norm_w, attn_q_norm,
                           attn_k_norm, ssm_conv_w, ssm_conv_b, ssm_a_log, ssm_dt_bias, ssm_d,
                           ssm_norm_w, w_branch_a, w_branch_b, w_branch_c, w_out)
        lp["w_main"] = (w_perm, l)
        nw = norm_w[l].reshape(1, D_MODEL)
        final = l == depth - 1
        xp, new_cache = _stream_layer(
            xp, lp, mod[l], nw, fnw, bd, None, None, new_cache,
            batch=batch, t_len=seq, mod_base=0, mod_per_batch=False, final=final)
        ctx = (cache_k[:, l].reshape(dec_batch, past, KV_WIDTH),
               cache_v[:, l].reshape(dec_batch, past, KV_WIDTH),
               (state_gdn, l), (state_ssm, l))
        xs, _ = _stream_layer(
            xs, lp, mod[l], nw, fnw, bd, rope_tabs, ctx, None,
            batch=dec_batch, t_len=dec_seq, mod_base=1, mod_per_batch=True, final=final)
    (new_k, new_v), new_gdn, new_ssm = new_cache
    return (xp.reshape(batch, seq, D_MODEL), xs.reshape(dec_batch, dec_seq, D_MODEL),
            new_k.reshape(batch, depth, seq, KV_HEADS, HEAD_DIM),
            new_v.reshape(batch, depth, seq, KV_HEADS, HEAD_DIM), new_gdn, new_ssm)
```

```python
import functools
import math

import jax
import jax.numpy as jnp
from jax import lax
from jax.experimental import pallas as pl
from jax.experimental.pallas import tpu as pltpu

F32 = jnp.float32
BF16 = jnp.bfloat16

D_MODEL = 1024
EPS = 1e-6
GDN_HEADS = 4
GDN_DK = 128
GDN_WIDTH = 512
N_HEADS = 8
KV_HEADS = 2
HEAD_DIM = 64
ATTN_WIDTH = 512
KV_WIDTH = 128
GRID_W = 64
ROPE_BASE = 10000.0
SSM_HEADS = 16
SSM_HEAD_DIM = 64
D_INNER = 1024
SSM_STATE = 128
SSM_XBC = 1536
CHUNK = 128
LANES = 128
SMALL_ROWS = 48
VMEM_LIMIT = 56 * 1024 * 1024

_SRC = dict(qkv=(0, 1536), az=(1536, 512), beta=(2048, 8), alpha=(2056, 8), bq=(2064, 512),
            kv=(2576, 256), bz=(2832, 512), xbc=(3344, 1536), cz=(4880, 1024), dt=(5904, 32),
            gates=(5936, 3072))
_MAIN = ("qkv", "az", "bq", "kv", "bz", "xbc", "cz", "gates")
_MAIN_OFF = {}
_o = 0
for _n in _MAIN:
    _MAIN_OFF[_n] = _o
    _o += _SRC[_n][1]
MAIN_WIDTH = _o


def _nt(a, b):
    return lax.dot_general(a, b, (((1,), (1,)), ((), ())), preferred_element_type=F32)


def _tn(a, b):
    return lax.dot_general(a, b, (((0,), (0,)), ((), ())), preferred_element_type=F32)


def _mm(a, b):
    return jnp.dot(a, b, preferred_element_type=F32)


def _sigmoid(x):
    return 0.5 * jnp.tanh(0.5 * x) + 0.5


def _silu(x):
    h = 0.5 * x
    return h * jnp.tanh(h) + h


def _softplus(x):
    return jnp.maximum(x, 0.0) + jnp.log1p(jnp.exp(-jnp.abs(x)))


def _split3(x):
    hi = x.astype(BF16)
    r1 = x - hi.astype(F32)
    mid = r1.astype(BF16)
    lo = (r1 - mid.astype(F32)).astype(BF16)
    return hi, mid, lo


def _cumsum_rows(x, tri):
    hi, mid, lo = _split3(x)
    return _mm(hi, tri) + _mm(mid, tri) + _mm(lo, tri)


def _rows_to_cols(x, eye_b):
    hi, mid, lo = _split3(x)
    return _nt(eye_b, hi) + _nt(eye_b, mid) + _nt(eye_b, lo)


def _conv_tile(src_ref, r, rt, n_tiles, w, bias, off):
    t_len = src_ref.shape[0]
    cols = slice(off, off + LANES)
    start = pl.multiple_of(r * rt, rt)
    cur = src_ref[pl.ds(start, rt), cols].astype(F32)
    pstart = pl.multiple_of(jnp.maximum(start - 16, 0), 16)
    nstart = pl.multiple_of(jnp.minimum(start + rt, t_len - 16), 16)
    prev_row = src_ref[pl.ds(pstart, 16), cols][15:16, :].astype(F32)
    next_row = src_ref[pl.ds(nstart, 16), cols][0:1, :].astype(F32)
    pos = lax.rem(r, n_tiles)
    prev_row = jnp.where(pos > 0, prev_row, 0.0)
    next_row = jnp.where(pos < n_tiles - 1, next_row, 0.0)
    ri = lax.broadcasted_iota(jnp.int32, (rt, LANES), 0)
    x_prev = jnp.where(ri == 0, prev_row, pltpu.roll(cur, 1, 0))
    x_next = jnp.where(ri == rt - 1, next_row, pltpu.roll(cur, rt - 1, 0))
    y = w[0:1, :] * x_prev + w[1:2, :] * cur + w[2:3, :] * x_next
    if bias is not None:
        y = y + bias
    return y


def _row_tile(t_len):
    return min(t_len, 256)


SCAN_STEP_ROWS = 4096
GDN_SLOTS = 2
SSD_SLOTS = 4


def _positions_per_body(half, max_slots):
    return max(n for n in (1, 2, 4) if half % n == 0 and n <= max_slots)


def _seqs_per_step(batch, t_len, max_slots):
    npar = _positions_per_body(t_len // CHUNK // 2, max_slots)
    return max(n for n in (1, 2, 4)
               if batch % n == 0 and n * npar <= max_slots and n * t_len <= SCAN_STEP_ROWS)


def _mod_kernel(c_ref, w_ref, b_ref, o_ref):
    s = _silu(c_ref[...])
    w = w_ref[...]
    s_hi = s.astype(BF16)
    s_lo = (s - s_hi.astype(F32)).astype(BF16)
    w_hi = w.astype(BF16)
    w_lo = (w - w_hi.astype(F32)).astype(BF16)
    o_ref[...] = _mm(s_hi, w_hi) + _mm(s_lo, w_hi) + _mm(s_hi, w_lo) + b_ref[...]


def _modulation(cond, w_mod, b_mod):
    depth = w_mod.shape[0]
    rows = cond.shape[0]
    tn = 1024
    return pl.pallas_call(
        _mod_kernel,
        grid=(depth, 3 * D_MODEL // tn),
        in_specs=[
            pl.BlockSpec((rows, D_MODEL), lambda l, j: (0, 0)),
            pl.BlockSpec((None, D_MODEL, tn), lambda l, j: (l, 0, j)),
            pl.BlockSpec((None, 1, tn), lambda l, j: (l, 0, j)),
        ],
        out_specs=pl.BlockSpec((None, rows, tn), lambda l, j: (l, 0, j)),
        out_shape=jax.ShapeDtypeStruct((depth, rows, 3 * D_MODEL), F32),
        name="modulation",
    )(cond, w_mod, b_mod.reshape(depth, 1, 3 * D_MODEL))


def _in_kernel(x_ref, nw_ref, mod_ref, w_ref, wst_ref,
               o_qkv, o_az, o_bq, o_kv, o_bz, o_xbc, o_cz, o_gates, o_small):
    x = x_ref[...]
    y = x * lax.rsqrt(jnp.mean(x * x, axis=-1, keepdims=True) + EPS) * nw_ref[...]
    shift = mod_ref[:, 0:D_MODEL]
    scale = mod_ref[:, D_MODEL:2 * D_MODEL]
    h = (y * (1.0 + scale) + shift).astype(BF16)
    outs = dict(qkv=o_qkv, az=o_az, bq=o_bq, kv=o_kv, bz=o_bz, xbc=o_xbc, cz=o_cz, gates=o_gates)
    for name in _MAIN:
        off, width = _MAIN_OFF[name], _SRC[name][1]
        outs[name][...] = _nt(h, w_ref[off:off + width, :]).astype(outs[name].dtype)
    o_small[...] = _nt(wst_ref[...], h)


def _in_proj(x, norm_w, mod, w_main, w_small_t, *, t_len, mod_base, mod_per_batch):
    n = x.shape[0]
    tm = 256
    assert n % tm == 0 and t_len % tm == 0
    w_all, layer = w_main

    def mod_idx(i):
        return (mod_base + (i * tm) // t_len if mod_per_batch else mod_base, 0, 0)

    widths = {k: _SRC[k][1] for k in _MAIN}
    dtypes = {k: (F32 if k == "kv" else BF16) for k in _MAIN}
    out_shape = [jax.ShapeDtypeStruct((n, widths[k]), dtypes[k]) for k in _MAIN]
    out_shape.append(jax.ShapeDtypeStruct((SMALL_ROWS, n), F32))
    out_specs = [pl.BlockSpec((tm, widths[k]), lambda i: (i, 0)) for k in _MAIN]
    out_specs.append(pl.BlockSpec((SMALL_ROWS, tm), lambda i: (0, i)))
    outs = pl.pallas_call(
        _in_kernel,
        grid=(n // tm,),
        in_specs=[
            pl.BlockSpec((tm, D_MODEL), lambda i: (i, 0)),
            pl.BlockSpec((1, D_MODEL), lambda i: (0, 0)),
            pl.BlockSpec((None, 1, 3 * D_MODEL), mod_idx),
            pl.BlockSpec((None, MAIN_WIDTH, D_MODEL), lambda i: (layer, 0, 0), pipeline_mode=pl.Buffered(1)),
            pl.BlockSpec((SMALL_ROWS, D_MODEL), lambda i: (0, 0)),
        ],
        out_specs=out_specs,
        out_shape=out_shape,
        compiler_params=pltpu.CompilerParams(
            dimension_semantics=("arbitrary",), vmem_limit_bytes=VMEM_LIMIT),
        name="in_proj",
    )(x, norm_w, mod, w_all, w_small_t)
    res = dict(zip(_MAIN, outs[:-1]))
    res["small_t"] = outs[-1]
    return res


INV_BASE = 8


def _tri_inverse(nms, dirs, sub, lane, eye_f):
    def same_block(s):
        sh = int(math.log2(s))
        return (sub >> sh) == (lane >> sh)

    base = same_block(INV_BASE)
    n8 = [jnp.where(base, nm, 0.0) for nm in nms]
    p = [n.astype(BF16) for n in n8]
    y = [eye_f - n for n in n8]
    p2 = [_mm(a, a).astype(BF16) for a in p]
    both = [_mm(a, jnp.concatenate([a, b.astype(BF16)], axis=1)) for a, b in zip(p2, y)]
    y = [b + c[:, CHUNK:] for b, c in zip(y, both)]
    p4 = [c[:, :CHUNK].astype(BF16) for c in both]
    y = [b + _mm(a, b.astype(BF16)) for a, b in zip(p4, y)]
    s = INV_BASE
    while s < CHUNK:
        sel = jnp.logical_and(same_block(2 * s), jnp.logical_not(same_block(s)))
        off = [jnp.where(sel, nm, 0.0).astype(BF16) for nm in nms]
        yb = [b.astype(BF16) for b in y]
        starts = [[2 * s * b + (s if d == 0 else 0) for b in range(CHUNK // (2 * s))] for d in dirs]
        rows = [jnp.concatenate([b[a:a + s] for a in st], axis=0).astype(BF16) for b, st in zip(y, starts)]
        t = [_mm(r, o).astype(BF16) for r, o in zip(rows, off)]
        upd = [_mm(a, c) for a, c in zip(t, yb)]
        new_y = []
        for b, u, st in zip(y, upd, starts):
            parts = []
            for n, a in enumerate(st):
                keep = b[a - s:a] if a % (2 * s) else b[a + s:a + 2 * s]
                changed = b[a:a + s] - u[n * s:(n + 1) * s]
                parts += [keep, changed] if a % (2 * s) else [changed, keep]
            new_y.append(jnp.concatenate(parts, axis=0))
        y = new_y
        s *= 2
    return y


def _gdn_kernel(*refs, t_len, nseq, has_ctx, n_prev):
    (q_ref, k_ref, v_ref, z_ref, gtf_ref, gtb_ref, wq_ref, wk_ref, wv_ref, nw_ref, alog_ref, bias_ref) = refs[:12]
    rest = refs[12:]
    s0_ref = prev_ref = sout_ref = None
    if has_ctx:
        s0_ref, o_ref, qs, ks, vs, oacc, st = rest
    else:
        if n_prev:
            prev_ref, rest = rest[0], rest[1:]
        o_ref, sout_ref, qs, ks, vs, oacc, st = rest
    rt = _row_tile(t_len)
    n_tiles = t_len // rt
    n_chunks = t_len // CHUNK
    chains = [(d, h) for d in range(2) for h in range(GDN_HEADS)]

    def prep(r, carry):
        start = pl.multiple_of(r * rt, rt)
        for h in range(GDN_HEADS):
            off = h * LANES
            q = _silu(_conv_tile(q_ref, r, rt, n_tiles, wq_ref[:, off:off + LANES], None, off))
            k = _silu(_conv_tile(k_ref, r, rt, n_tiles, wk_ref[:, off:off + LANES], None, off))
            v = _silu(_conv_tile(v_ref, r, rt, n_tiles, wv_ref[:, off:off + LANES], None, off))
            q = q * lax.rsqrt(jnp.sum(q * q, axis=-1, keepdims=True) + EPS) * (GDN_DK ** -0.5)
            k = k * lax.rsqrt(jnp.sum(k * k, axis=-1, keepdims=True) + EPS)
            qs[pl.ds(start, rt), off:off + LANES] = q.astype(BF16)
            ks[pl.ds(start, rt), off:off + LANES] = k.astype(BF16)
            vs[pl.ds(start, rt), off:off + LANES] = v.astype(BF16)
        return carry

    lax.fori_loop(0, nseq * n_tiles, prep, 0)

    nc = len(chains)
    for sq in range(nseq):
        for ci, (d, h) in enumerate(chains):
            st[sq * nc + ci] = s0_ref[sq, d, h] if has_ctx else jnp.zeros((GDN_DK, CHUNK), F32)

    sub = lax.broadcasted_iota(jnp.int32, (CHUNK, CHUNK), 0)
    lane = lax.broadcasted_iota(jnp.int32, (CHUNK, CHUNK), 1)
    eye = sub == lane
    eye_f = eye.astype(F32)
    incl = [lane <= sub, lane >= sub]
    strict = [lane < sub, lane > sub]
    eye_b = eye.astype(BF16)
    tri = [(sub <= lane).astype(BF16), (sub >= lane).astype(BF16)]
    grow = lax.broadcasted_iota(jnp.int32, (4 * GDN_HEADS, CHUNK), 0)
    is_beta = ((grow >> int(math.log2(GDN_HEADS))) & 1) == 0
    is_fwd = grow < 2 * GDN_HEADS
    neg_a = jnp.where(is_beta, 0.0, -jnp.exp(alog_ref[...]))
    beta_row = lambda d, h: 2 * d * GDN_HEADS + h
    decay_row = lambda d, h: (2 * d + 1) * GDN_HEADS + h

    half = n_chunks // 2
    npar = _positions_per_body(half, GDN_SLOTS)

    def step(i, second):
        slots = [(sq, k) for sq in range(nseq) for k in range(npar)]
        c0 = [[pl.multiple_of(sq * t_len + (i * npar + k) * CHUNK, CHUNK),
               pl.multiple_of(sq * t_len + (n_chunks - 1 - (i * npar + k)) * CHUNK, CHUNK)] for sq, k in slots]
        raw = [jnp.concatenate([gtf_ref[:, pl.ds(c[0], CHUNK)], gtb_ref[:, pl.ds(c[1], CHUNK)]], axis=0)
               for c in c0]
        g_all = [neg_a * _softplus(r + bias_ref[...]) for r in raw]
        gc_all = [jnp.where(is_fwd, _cumsum_rows(g, tri[0]), _cumsum_rows(g, tri[1])) for g in g_all]
        comb = [jnp.where(is_beta, _sigmoid(r), g) for r, g in zip(raw, gc_all)]
        comb_t = [_rows_to_cols(c, eye_b) for c in comb]
        ch = [(k, d, h) for k in range(len(slots)) for d, h in chains]
        cols = [slice(h * LANES, (h + 1) * LANES) for _, _, h in ch]
        qb = [qs[pl.ds(c0[k][d], CHUNK), cs] for (k, d, _), cs in zip(ch, cols)]
        kb = [ks[pl.ds(c0[k][d], CHUNK), cs] for (k, d, _), cs in zip(ch, cols)]
        vf = [vs[pl.ds(c0[k][d], CHUNK), cs].astype(F32) for (k, d, _), cs in zip(ch, cols)]
        qf = [a.astype(F32) for a in qb]
        kf = [a.astype(F32) for a in kb]
        gc = [comb[k][decay_row(d, h):decay_row(d, h) + 1, :] for k, d, h in ch]
        beta_col = [jnp.broadcast_to(comb_t[k][:, beta_row(d, h):beta_row(d, h) + 1], (CHUNK, CHUNK))
                    for k, d, h in ch]
        gc_col = [jnp.broadcast_to(comb_t[k][:, decay_row(d, h):decay_row(d, h) + 1], (CHUNK, CHUNK))
                  for k, d, h in ch]
        kk = [_nt(a, a) for a in kb]
        qk = [_nt(a, b) for a, b in zip(qb, kb)]
        dec = [jnp.where(incl[d], jnp.exp(jnp.where(incl[d], c - r, 0.0)), 0.0)
               for c, r, (_, d, _) in zip(gc_col, gc, ch)]
        nm = [jnp.where(strict[d], a * b * c, 0.0) for a, b, c, (_, d, _) in zip(kk, dec, beta_col, ch)]
        t_inv = _tri_inverse(nm, [d for _, d, _ in ch], sub, lane, eye_f)
        eg = [jnp.exp(a) for a in gc_col]
        rhs = [jnp.concatenate([a * b, c * (b * e)], axis=1).astype(BF16)
               for a, b, c, e in zip(vf, beta_col, kf, eg)]
        sol = [_mm(a.astype(BF16), b) for a, b in zip(t_inv, rhs)]
        tot = [a[:, CHUNK - 1:CHUNK] if d == 0 else a[:, 0:1] for a, (_, d, _) in zip(gc, ch)]
        q_dec = [(a * e).astype(BF16) for a, e in zip(qf, eg)]
        k_dec = [(a * jnp.exp(t - c)).astype(BF16) for a, t, c in zip(kf, tot, gc_col)]
        intra = [(a * b).astype(BF16) for a, b in zip(qk, dec)]
        state = [st[n] for n in range(nseq * nc)]
        for k in range(npar):
            idx = [(sq * npar + k) * nc + ci for sq in range(nseq) for ci in range(nc)]
            sidx = [sq * nc + ci for sq in range(nseq) for ci in range(nc)]
            sb = [state[n].astype(BF16) for n in sidx]
            v_new = [sol[m][:, :CHUNK] - _mm(sol[m][:, CHUNK:].astype(BF16), b) for m, b in zip(idx, sb)]
            vb = [a.astype(BF16) for a in v_new]
            o = [_mm(jnp.concatenate([q_dec[m], intra[m]], axis=1), jnp.concatenate([b, e], axis=0))
                 for m, b, e in zip(idx, sb, vb)]
            for m, n, e, ov in zip(idx, sidx, vb, o):
                state[n] = state[n] * jnp.exp(tot[m]) + _tn(k_dec[m], e)
                slot, d, h = ch[m]
                rows = pl.ds(c0[slot][d], CHUNK)
                cs = slice(h * LANES, (h + 1) * LANES)
                if not second:
                    oacc[rows, cs] = ov
                else:
                    ot = oacc[rows, cs] + ov
                    ot = ot * lax.rsqrt(jnp.mean(ot * ot, axis=-1, keepdims=True) + EPS) * nw_ref[...]
                    o_ref[rows, cs] = (ot * _silu(z_ref[rows, cs].astype(F32))).astype(o_ref.dtype)
        for n in range(nseq * nc):
            st[n] = state[n]

    lax.fori_loop(0, half // npar, lambda i, c: (step(i, False), c)[1], 0)
    lax.fori_loop(half // npar, n_chunks // npar, lambda i, c: (step(i, True), c)[1], 0)

    if sout_ref is not None:
        if n_prev:
            sout_ref[:, 0:n_prev] = prev_ref[...]
        for sq in range(nseq):
            for ci, (d, h) in enumerate(chains):
                sout_ref[sq, n_prev, d, h] = st[sq * nc + ci]


def _stacked_state_specs(prev, nseq, tail, index_map):
    n_prev = 0 if prev is None else prev.shape[1]
    in_spec = None if prev is None else pl.BlockSpec((nseq, n_prev) + tail, index_map)
    return n_prev, in_spec, pl.BlockSpec((nseq, n_prev + 1) + tail, index_map)


def _gdn(p, conv_w, norm_w, a_log, dt_bias, s0, prev, *, batch, t_len):
    n = batch * t_len
    has_ctx = s0 is not None
    assert (t_len // CHUNK) % 2 == 0
    nseq = _seqs_per_step(batch, t_len, GDN_SLOTS)
    rows = nseq * t_len
    big = dict(pipeline_mode=pl.Buffered(1)) if rows * GDN_WIDTH * 2 >= (2 << 20) else {}
    seq = lambda j: pl.BlockSpec((rows, GDN_WIDTH), lambda b, j=j: (b, j), **big)
    cw = lambda j: pl.BlockSpec((3, GDN_WIDTH), lambda b, j=j: (0, j))
    gate_rows = 2 * GDN_HEADS
    first = 2 * SSM_HEADS // gate_rows
    par = pl.BlockSpec((2 * gate_rows, 1), lambda b: (0, 0))
    in_specs = [seq(0), seq(1), seq(2), seq(0),
                pl.BlockSpec((gate_rows, rows), lambda b: (first, b)),
                pl.BlockSpec((gate_rows, rows), lambda b: (first + 1, b)),
                cw(0), cw(1), cw(2),
                pl.BlockSpec((1, LANES), lambda b: (0, 0)), par, par]
    args = [p["qkv"], p["qkv"], p["qkv"], p["az"], p["small_t"], p["small_t"], conv_w, conv_w, conv_w,
            norm_w, a_log, dt_bias]
    out_shape = [jax.ShapeDtypeStruct((n, GDN_WIDTH), BF16)]
    out_specs = [pl.BlockSpec((rows, GDN_WIDTH), lambda b: (b, 0))]
    n_prev = 0
    if has_ctx:
        s0_all, layer = s0
        in_specs.append(pl.BlockSpec((nseq, None, 2, GDN_HEADS, GDN_DK, CHUNK),
                                     lambda b: (b, layer, 0, 0, 0, 0)))
        args.append(s0_all)
    else:
        tail = (2, GDN_HEADS, GDN_DK, CHUNK)
        n_prev, prev_spec, out_spec = _stacked_state_specs(prev, nseq, tail, lambda b: (b, 0, 0, 0, 0, 0))
        if n_prev:
            in_specs.append(prev_spec)
            args.append(prev)
        out_shape.append(jax.ShapeDtypeStruct((batch, n_prev + 1) + tail, F32))
        out_specs.append(out_spec)
    scratch = [pltpu.VMEM((rows, GDN_WIDTH), BF16) for _ in range(3)]
    scratch += [pltpu.VMEM((rows, GDN_WIDTH), F32), pltpu.VMEM((nseq * 2 * GDN_HEADS, GDN_DK, CHUNK), F32)]
    outs = pl.pallas_call(
        functools.partial(_gdn_kernel, t_len=t_len, nseq=nseq, has_ctx=has_ctx, n_prev=n_prev),
        grid=(batch // nseq,),
        in_specs=in_specs, out_specs=out_specs, out_shape=out_shape, scratch_shapes=scratch,
        compiler_params=pltpu.CompilerParams(
            dimension_semantics=("arbitrary",), vmem_limit_bytes=VMEM_LIMIT),
        name="gdn_ctx" if has_ctx else "gdn",
    )(*args)
    return outs[0], (None if has_ctx else outs[1])


SSD_GROUP_HEADS = 8
SSD_PAIRS = SSD_GROUP_HEADS // 2
SSD_GROUP_WIDTH = SSD_GROUP_HEADS * SSM_HEAD_DIM


def _ssd_kernel(*refs, t_len, nseq, has_ctx, n_prev):
    (x_ref, b_ref, c_ref, z_ref, dtf_ref, dtb_ref, wx_ref, wb_ref, wc_ref, bx_ref, bb_ref, bc_ref,
     dvec_ref, alog_ref, bias_ref) = refs[:15]
    rest = refs[15:]
    s0_ref = prev_ref = sout_ref = None
    if has_ctx:
        s0_ref, o_ref, xs, bs, cs, yacc, st = rest
    else:
        if n_prev:
            prev_ref, rest = rest[0], rest[1:]
        o_ref, sout_ref, xs, bs, cs, yacc, st = rest
    rt = _row_tile(t_len)
    n_tiles = t_len // rt
    n_chunks = t_len // CHUNK
    chains = [(d, q) for d in range(2) for q in range(SSD_PAIRS)]

    def prep(r, carry):
        start = pl.multiple_of(r * rt, rt)
        for q in range(SSD_PAIRS):
            off = q * LANES
            xs[pl.ds(start, rt), off:off + LANES] = _silu(_conv_tile(
                x_ref, r, rt, n_tiles, wx_ref[:, off:off + LANES], bx_ref[:, off:off + LANES], off)).astype(BF16)
        bs[pl.ds(start, rt), :] = _silu(_conv_tile(b_ref, r, rt, n_tiles, wb_ref[...], bb_ref[...], 0)).astype(BF16)
        cs[pl.ds(start, rt), :] = _silu(_conv_tile(c_ref, r, rt, n_tiles, wc_ref[...], bc_ref[...], 0)).astype(BF16)
        return carry

    lax.fori_loop(0, nseq * n_tiles, prep, 0)

    nc = len(chains)
    for sq in range(nseq):
        for ci, (d, q) in enumerate(chains):
            if has_ctx:
                st[sq * nc + ci] = jnp.concatenate([s0_ref[sq, d, 2 * q], s0_ref[sq, d, 2 * q + 1]], axis=0)
            else:
                st[sq * nc + ci] = jnp.zeros((2 * SSM_HEAD_DIM, SSM_STATE), F32)

    sub = lax.broadcasted_iota(jnp.int32, (CHUNK, CHUNK), 0)
    lane = lax.broadcasted_iota(jnp.int32, (CHUNK, CHUNK), 1)
    eye = sub == lane
    incl = [lane <= sub, lane >= sub]
    head0_lane = lane < SSM_HEAD_DIM
    head0_sub = sub < SSM_HEAD_DIM
    eye_b = eye.astype(BF16)
    tri = [(sub <= lane).astype(BF16), (sub >= lane).astype(BF16)]
    is_fwd = lax.broadcasted_iota(jnp.int32, (2 * SSD_GROUP_HEADS, CHUNK), 0) < SSD_GROUP_HEADS
    lane_r = lax.broadcasted_iota(jnp.int32, (2 * SSD_GROUP_HEADS, CHUNK), 1)
    neg_a = -jnp.exp(alog_ref[...])
    hrow = lambda d, q, j: d * SSD_GROUP_HEADS + 2 * q + j

    half = n_chunks // 2
    npar = _positions_per_body(half, SSD_SLOTS)

    def step(i, second_half):
        slots = [(sq, k) for sq in range(nseq) for k in range(npar)]
        c0 = [[pl.multiple_of(sq * t_len + (i * npar + k) * CHUNK, CHUNK),
               pl.multiple_of(sq * t_len + (n_chunks - 1 - (i * npar + k)) * CHUNK, CHUNK)] for sq, k in slots]
        kd = [(k, d) for k in range(len(slots)) for d in range(2)]
        bm = {(k, d): bs[pl.ds(c0[k][d], CHUNK), :] for k, d in kd}
        cm = {(k, d): cs[pl.ds(c0[k][d], CHUNK), :] for k, d in kd}
        cb = {key: _nt(cm[key], bm[key]) for key in kd}
        raw = [jnp.concatenate([dtf_ref[:, pl.ds(c[0], CHUNK)], dtb_ref[:, pl.ds(c[1], CHUNK)]], axis=0)
               for c in c0]
        dt_all = [_softplus(r + bias_ref[...]) for r in raw]
        da_all = [a * neg_a for a in dt_all]
        acum_all = [jnp.where(is_fwd, _cumsum_rows(a, tri[0]), _cumsum_rows(a, tri[1])) for a in da_all]
        tot_all = [jnp.sum(jnp.where(lane_r == jnp.where(is_fwd, CHUNK - 1, 0), a, 0.0), axis=1, keepdims=True)
                   for a in acum_all]
        w_all = [a * jnp.exp(t - c) for a, t, c in zip(dt_all, tot_all, acum_all)]
        cols_t = [_rows_to_cols(a, eye_b) for a in acum_all]
        cd_all = [jnp.exp(t) for t in tot_all]
        shifted_all = [a - jnp.log(t) for a, t in zip(acum_all, dt_all)]
        order = [(sq * npar + k, sq, d, q) for k in range(npar) for sq in range(nseq) for d, q in chains]
        state = {(sq, d, q): st[sq * nc + ci] for sq in range(nseq) for ci, (d, q) in enumerate(chains)}

        def first(slot, sq, d, q):
            cs_ = slice(q * LANES, (q + 1) * LANES)
            xb = xs[pl.ds(c0[slot][d], CHUNK), cs_]
            x01 = [jnp.where(head0_lane, xb, jnp.zeros_like(xb)), jnp.where(head0_lane, jnp.zeros_like(xb), xb)]
            r = [hrow(d, q, j) for j in range(2)]
            acum_b = [jnp.broadcast_to(cols_t[slot][:, a:a + 1], (CHUNK, CHUNK)) for a in r]
            m_mat = [(cb[(slot, d)] * jnp.where(incl[d], jnp.exp(c - shifted_all[slot][a:a + 1, :]), 0.0)
                      ).astype(BF16) for c, a in zip(acum_b, r)]
            w_diag = [jnp.where(eye, w_all[slot][a:a + 1, :], 0.0).astype(BF16) for a in r]
            y = _mm(m_mat[0], x01[0]) + _mm(m_mat[1], x01[1])
            xw = _mm(w_diag[0], x01[0]) + _mm(w_diag[1], x01[1])
            din = jnp.where(head0_lane, jnp.exp(acum_b[0]), jnp.exp(acum_b[1]))
            cd = jnp.where(head0_sub, cd_all[slot][r[0]:r[0] + 1, :], cd_all[slot][r[1]:r[1] + 1, :])
            return xb, y, xw, din, cd

        def second(slot, sq, d, q, xb, y, xw, din, cd):
            cs_ = slice(q * LANES, (q + 1) * LANES)
            s_old = state[(sq, d, q)]
            yk = y + _nt(cm[(slot, d)], s_old.astype(BF16)) * din
            state[(sq, d, q)] = s_old * cd + _tn(xw.astype(BF16), bm[(slot, d)])
            rows = pl.ds(c0[slot][d], CHUNK)
            if not second_half:
                yacc[rows, cs_] = yk
            else:
                yt = yacc[rows, cs_] + yk + dvec_ref[:, cs_] * xb.astype(F32)
                o_ref[rows, cs_] = (yt * _silu(z_ref[rows, cs_].astype(F32))).astype(o_ref.dtype)

        pending = None
        for item in order:
            res = first(*item)
            if pending is not None:
                second(*pending)
            pending = item + res
        second(*pending)
        for sq in range(nseq):
            for ci, (d, q) in enumerate(chains):
                st[sq * nc + ci] = state[(sq, d, q)]

    lax.fori_loop(0, half // npar, lambda i, c: (step(i, False), c)[1], 0)
    lax.fori_loop(half // npar, n_chunks // npar, lambda i, c: (step(i, True), c)[1], 0)

    if sout_ref is not None:
        if n_prev:
            sout_ref[:, 0:n_prev] = prev_ref[...]
        for sq in range(nseq):
            for ci, (d, q) in enumerate(chains):
                s = st[sq * nc + ci]
                sout_ref[sq, n_prev, d, 2 * q] = s[:SSM_HEAD_DIM]
                sout_ref[sq, n_prev, d, 2 * q + 1] = s[SSM_HEAD_DIM:]


def _ssd(p, conv_w, conv_b, d_vec, a_log, dt_bias, s0, prev, *, batch, t_len):
    n = batch * t_len
    has_ctx = s0 is not None
    assert (t_len // CHUNK) % 2 == 0
    n_groups = SSM_HEADS // SSD_GROUP_HEADS
    nseq = _seqs_per_step(batch, t_len, SSD_SLOTS)
    rows = nseq * t_len
    x_blocks = D_INNER // LANES
    big = dict(pipeline_mode=pl.Buffered(1)) if rows * SSD_GROUP_WIDTH * 2 >= (2 << 20) else {}
    par = pl.BlockSpec((None, 2 * SSD_GROUP_HEADS, 1), lambda b, g: (g, 0, 0))
    in_specs = [pl.BlockSpec((rows, SSD_GROUP_WIDTH), lambda b, g: (b, g), **big),
                pl.BlockSpec((rows, LANES), lambda b, g: (b, x_blocks + g)),
                pl.BlockSpec((rows, LANES), lambda b, g: (b, x_blocks + n_groups + g)),
                pl.BlockSpec((rows, SSD_GROUP_WIDTH), lambda b, g: (b, g), **big),
                pl.BlockSpec((SSD_GROUP_HEADS, rows), lambda b, g: (g, b)),
                pl.BlockSpec((SSD_GROUP_HEADS, rows), lambda b, g: (n_groups + g, b)),
                pl.BlockSpec((3, SSD_GROUP_WIDTH), lambda b, g: (0, g)),
                pl.BlockSpec((3, LANES), lambda b, g: (0, x_blocks + g)),
                pl.BlockSpec((3, LANES), lambda b, g: (0, x_blocks + n_groups + g)),
                pl.BlockSpec((1, SSD_GROUP_WIDTH), lambda b, g: (0, g)),
                pl.BlockSpec((1, LANES), lambda b, g: (0, x_blocks + g)),
                pl.BlockSpec((1, LANES), lambda b, g: (0, x_blocks + n_groups + g)),
                pl.BlockSpec((1, SSD_GROUP_WIDTH), lambda b, g: (0, g)), par, par]
    args = [p["xbc"], p["xbc"], p["xbc"], p["cz"], p["small_t"], p["small_t"], conv_w, conv_w, conv_w,
            conv_b, conv_b, conv_b, d_vec, a_log, dt_bias]
    out_shape = [jax.ShapeDtypeStruct((n, D_INNER), BF16)]
    out_specs = [pl.BlockSpec((rows, SSD_GROUP_WIDTH), lambda b, g: (b, g))]
    n_prev = 0
    if has_ctx:
        s0_all, layer = s0
        in_specs.append(pl.BlockSpec((nseq, None, 2, SSD_GROUP_HEADS, SSM_HEAD_DIM, SSM_STATE),
                                     lambda b, g: (b, layer, 0, g, 0, 0)))
        args.append(s0_all)
    else:
        n_prev, prev_spec, out_spec = _stacked_state_specs(
            prev, nseq, (2, SSD_GROUP_HEADS, SSM_HEAD_DIM, SSM_STATE), lambda b, g: (b, 0, 0, g, 0, 0))
        if n_prev:
            in_specs.append(prev_spec)
            args.append(prev)
        out_shape.append(jax.ShapeDtypeStruct((batch, n_prev + 1, 2, SSM_HEADS, SSM_HEAD_DIM, SSM_STATE), F32))
        out_specs.append(out_spec)
    scratch = [pltpu.VMEM((rows, SSD_GROUP_WIDTH), BF16), pltpu.VMEM((rows, LANES), BF16),
               pltpu.VMEM((rows, LANES), BF16), pltpu.VMEM((rows, SSD_GROUP_WIDTH), F32),
               pltpu.VMEM((nseq * 2 * SSD_PAIRS, 2 * SSM_HEAD_DIM, SSM_STATE), F32)]
    outs = pl.pallas_call(
        functools.partial(_ssd_kernel, t_len=t_len, nseq=nseq, has_ctx=has_ctx, n_prev=n_prev),
        grid=(batch // nseq, n_groups),
        in_specs=in_specs, out_specs=out_specs, out_shape=out_shape, scratch_shapes=scratch,
        compiler_params=pltpu.CompilerParams(
            dimension_semantics=("arbitrary", "arbitrary"), vmem_limit_bytes=VMEM_LIMIT),
        name="ssd_ctx" if has_ctx else "ssd",
    )(*args)
    return outs[0], (None if has_ctx else outs[1])


KEY_TILE = 256


def _seg_rmsnorm(x, bd):
    x2 = x * x
    hi = x2.astype(BF16)
    lo = (x2 - hi.astype(F32)).astype(BF16)
    ms = _mm(hi, bd) + _mm(lo, bd)
    return x * lax.rsqrt(ms + EPS)


def _rope128(x, cos, sin_signed):
    lane = lax.broadcasted_iota(jnp.int32, x.shape, 1)
    first = (lane % 32) < 16
    swapped = jnp.where(first, pltpu.roll(x, LANES - 16, 1), pltpu.roll(x, 16, 1))
    return x * cos + swapped * sin_signed


def _attn_kernel(*refs, t_len, tq, past, has_ctx, n_prev):
    q_ref, kv_ref, z_ref, qnw_ref, knw_ref, bd_ref = refs[:6]
    rest = refs[6:]
    kn_ref = vn_ref = pk_ref = pv_ref = None
    if has_ctx:
        cosq_ref, sinq_ref, cosk_ref, sink_ref, ck_ref, cv_ref, o_ref, k_s, v_s = rest
    else:
        if n_prev:
            pk_ref, pv_ref, rest = rest[0], rest[1], rest[2:]
        o_ref, kn_ref, vn_ref, k_s, v_s = rest
    qi = pl.program_id(1)
    rt = _row_tile(t_len)
    n_tiles = t_len // rt

    def store_values(rows, v):
        low_v = lax.broadcasted_iota(jnp.int32, v.shape, 1) < HEAD_DIM
        v_s[0, rows, :] = jnp.where(low_v, v, 1.0).astype(BF16)
        v_s[1, rows, :] = jnp.where(low_v, 1.0, v).astype(BF16)

    @pl.when(qi == 0)
    def _():
        def prep(r, carry):
            start = pl.multiple_of(r * rt, rt)
            kv = kv_ref[pl.ds(start, rt), :]
            kn = _seg_rmsnorm(kv[:, :KV_WIDTH], bd_ref[0:KV_WIDTH, 0:KV_WIDTH]) * knw_ref[...]
            if kn_ref is not None:
                kn_ref[n_prev, pl.ds(start, rt), :] = kn
                vn_ref[n_prev, pl.ds(start, rt), :] = kv[:, KV_WIDTH:]
            if has_ctx:
                kn = _rope128(kn, cosk_ref[pl.ds(start, rt), :], sink_ref[pl.ds(start, rt), :])
            k_s[pl.ds(start, rt), :] = kn.astype(BF16)
            store_values(pl.ds(start, rt), kv[:, KV_WIDTH:])
            return carry

        lax.fori_loop(0, n_tiles, prep, 0)
        if n_prev:
            kn_ref[0:n_prev] = pk_ref[...]
            vn_ref[0:n_prev] = pv_ref[...]
        if has_ctx:
            k_s[t_len:t_len + past, :] = ck_ref[...].astype(BF16)
            store_values(slice(t_len, t_len + past), cv_ref[...])

    q = _seg_rmsnorm(q_ref[...].astype(F32), bd_ref[...]) * qnw_ref[...]
    blocks = []
    for cbk in range(ATTN_WIDTH // LANES):
        blk = q[:, cbk * LANES:(cbk + 1) * LANES]
        if has_ctx:
            blk = _rope128(blk, cosq_ref[...], sinq_ref[...])
        blocks.append((blk * (HEAD_DIM ** -0.5)).astype(BF16))
    lane = lax.broadcasted_iota(jnp.int32, (tq, LANES), 1)
    low = lane < HEAD_DIM
    group = N_HEADS // KV_HEADS
    head_out = [None] * N_HEADS
    qgs = []
    for kvh in range(KV_HEADS):
        keep = low if kvh == 0 else jnp.logical_not(low)
        rows = []
        for g in range(group):
            head = kvh * group + g
            blk = blocks[head // 2]
            if head % 2 != kvh:
                blk = pltpu.roll(blk, HEAD_DIM, 1)
            rows.append(jnp.where(keep, blk, jnp.zeros_like(blk)))
        qgs.append(jnp.concatenate(rows, axis=0))

    n_kt = (t_len + past) // KEY_TILE
    keys = lambda j: slice(j * KEY_TILE, (j + 1) * KEY_TILE)
    score_tile = lambda kvh, j: _nt(qgs[kvh], k_s[keys(j), :]).astype(BF16)
    row_max = lambda tiles: jnp.max(functools.reduce(jnp.maximum, tiles), axis=-1, keepdims=True)
    s_tiles = [[score_tile(0, j) for j in range(n_kt)], []]
    m = [row_max(s_tiles[0]), None]
    p_tiles = [[], []]

    def exp_tile(kvh, j):
        p_tiles[kvh].append(jnp.exp(s_tiles[kvh][j] - m[kvh]))

    acc = [None, None]

    def pv_tile(kvh, j):
        part = _mm(p_tiles[kvh][j], v_s[kvh, keys(j), :])
        acc[kvh] = part if acc[kvh] is None else acc[kvh] + part

    for j in range(n_kt):
        s_tiles[1].append(score_tile(1, j))
        exp_tile(0, j)
    m[1] = row_max(s_tiles[1])
    for j in range(n_kt):
        pv_tile(0, j)
        exp_tile(1, j)
    for j in range(n_kt):
        pv_tile(1, j)
    for kvh in range(KV_HEADS):
        ones_lane = HEAD_DIM if kvh == 0 else 0
        o = acc[kvh] / acc[kvh][:, ones_lane:ones_lane + 1]
        for g in range(group):
            head = kvh * group + g
            og = o[g * tq:(g + 1) * tq, :]
            if head % 2 != kvh:
                og = pltpu.roll(og, HEAD_DIM, 1)
            head_out[head] = og
    outs = [jnp.where(low, head_out[2 * i], head_out[2 * i + 1]) for i in range(N_HEADS // 2)]
    o_all = jnp.concatenate(outs, axis=1)
    o_ref[...] = (o_all * _silu(z_ref[...].astype(F32))).astype(o_ref.dtype)


def _attention(p, qnw, knw, bd, rope_tabs, cache_k, cache_v, prev_kv, *, batch, t_len):
    n = batch * t_len
    has_ctx = cache_k is not None
    past = cache_k.shape[1] if has_ctx else 0
    n_prev = 0
    tq = min(t_len, 256)
    nq = t_len // tq
    once = dict(pipeline_mode=pl.Buffered(1)) if t_len * 2 * KV_WIDTH * 4 >= (2 << 20) else {}
    in_specs = [pl.BlockSpec((tq, ATTN_WIDTH), lambda b, i: (b * nq + i, 0)),
                pl.BlockSpec((t_len, 2 * KV_WIDTH), lambda b, i: (b, 0), **once),
                pl.BlockSpec((tq, ATTN_WIDTH), lambda b, i: (b * nq + i, 0)),
                pl.BlockSpec((1, ATTN_WIDTH), lambda b, i: (0, 0)),
                pl.BlockSpec((1, KV_WIDTH), lambda b, i: (0, 0)),
                pl.BlockSpec((ATTN_WIDTH, ATTN_WIDTH), lambda b, i: (0, 0))]
    args = [p["bq"], p["kv"], p["bz"], qnw, knw, bd]
    out_shape = [jax.ShapeDtypeStruct((n, ATTN_WIDTH), BF16)]
    out_specs = [pl.BlockSpec((tq, ATTN_WIDTH), lambda b, i: (b * nq + i, 0))]
    if has_ctx:
        cos, sin = rope_tabs
        in_specs += [pl.BlockSpec((tq, LANES), lambda b, i: (i, 0)),
                     pl.BlockSpec((tq, LANES), lambda b, i: (i, 0)),
                     pl.BlockSpec((t_len, LANES), lambda b, i: (0, 0), **once),
                     pl.BlockSpec((t_len, LANES), lambda b, i: (0, 0), **once),
                     pl.BlockSpec((None, past, KV_WIDTH), lambda b, i: (b, 0, 0)),
                     pl.BlockSpec((None, past, KV_WIDTH), lambda b, i: (b, 0, 0))]
        args += [cos, sin, cos, sin, cache_k, cache_v]
    else:
        n_prev = 0 if prev_kv is None else prev_kv[0].shape[1]
        if n_prev:
            in_specs += [pl.BlockSpec((None, n_prev, t_len, KV_WIDTH), lambda b, i: (b, 0, 0, 0))] * 2
            args += list(prev_kv)
        out_shape += [jax.ShapeDtypeStruct((batch, n_prev + 1, t_len, KV_WIDTH), F32)] * 2
        out_specs += [pl.BlockSpec((None, n_prev + 1, t_len, KV_WIDTH), lambda b, i: (b, 0, 0, 0))] * 2
    scratch = [pltpu.VMEM((t_len + past, KV_WIDTH), BF16),
               pltpu.VMEM((KV_HEADS, t_len + past, KV_WIDTH), BF16)]
    outs = pl.pallas_call(
        functools.partial(_attn_kernel, t_len=t_len, tq=tq, past=past, has_ctx=has_ctx, n_prev=n_prev),
        grid=(batch, nq),
        in_specs=in_specs, out_specs=out_specs, out_shape=out_shape, scratch_shapes=scratch,
        compiler_params=pltpu.CompilerParams(
            dimension_semantics=("arbitrary", "arbitrary"), vmem_limit_bytes=VMEM_LIMIT),
        name="attn_ctx" if has_ctx else "attn",
    )(*args)
    return outs[0], (None if has_ctx else (outs[1], outs[2]))


def _out_kernel(x_ref, oa_ref, ob_ref, yc_ref, g_ref, mod_ref, snw_ref, wa_ref, wb_ref, wc_ref, wo_ref,
                fnw_ref, o_ref, *, final):
    ba = _mm(oa_ref[...], wa_ref[...])
    bb = _mm(ob_ref[...], wb_ref[...])
    yc = yc_ref[...].astype(F32)
    yn = yc * lax.rsqrt(jnp.mean(yc * yc, axis=-1, keepdims=True) + EPS) * snw_ref[...]
    bc = _mm(yn.astype(BF16), wc_ref[...])
    g = g_ref[...].astype(F32)
    merged = (_sigmoid(g[:, 0:D_MODEL]) * ba + _sigmoid(g[:, D_MODEL:2 * D_MODEL]) * bb
              + _sigmoid(g[:, 2 * D_MODEL:]) * bc)
    out = _mm(merged.astype(BF16), wo_ref[...])
    xn = x_ref[...] + mod_ref[:, 2 * D_MODEL:] * out
    if final:
        xn = xn * lax.rsqrt(jnp.mean(xn * xn, axis=-1, keepdims=True) + EPS) * fnw_ref[...]
    o_ref[...] = xn


def _out_proj(x, o_a, o_b, y_c, gates, mod, ssm_norm_w, wa, wb, wc, wo, final_norm_w, *,
              t_len, mod_base, mod_per_batch, final):
    n = x.shape[0]
    tm = 512 if t_len % 512 == 0 else 256
    assert n % tm == 0 and t_len % tm == 0

    def mod_idx(i):
        return (mod_base + (i * tm) // t_len if mod_per_batch else mod_base, 0, 0)

    tok = lambda w: pl.BlockSpec((tm, w), lambda i: (i, 0))
    full = lambda a, b: pl.BlockSpec((a, b), lambda i: (0, 0))
    return pl.pallas_call(
        functools.partial(_out_kernel, final=final),
        grid=(n // tm,),
        in_specs=[tok(D_MODEL), tok(GDN_WIDTH), tok(ATTN_WIDTH), tok(D_INNER), tok(3 * D_MODEL),
                  pl.BlockSpec((None, 1, 3 * D_MODEL), mod_idx), full(1, D_INNER),
                  full(GDN_WIDTH, D_MODEL), full(ATTN_WIDTH, D_MODEL), full(D_INNER, D_MODEL),
                  full(D_MODEL, D_MODEL), full(1, D_MODEL)],
        out_specs=tok(D_MODEL),
        out_shape=jax.ShapeDtypeStruct((n, D_MODEL), F32),
        compiler_params=pltpu.CompilerParams(
            dimension_semantics=("arbitrary",), vmem_limit_bytes=VMEM_LIMIT),
        name="out_proj",
    )(x, o_a, o_b, y_c, gates, mod, ssm_norm_w, wa, wb, wc, wo, final_norm_w)


_SMALL_SRC = ((_SRC["dt"][0], 2 * SSM_HEADS),
              (_SRC["beta"][0], GDN_HEADS), (_SRC["alpha"][0], GDN_HEADS),
              (_SRC["beta"][0] + GDN_HEADS, GDN_HEADS), (_SRC["alpha"][0] + GDN_HEADS, GDN_HEADS))


def _wprep_kernel(w_ref, o_ref, os_ref):
    for name in _MAIN:
        src, width = _SRC[name]
        off = _MAIN_OFF[name]
        o_ref[off:off + width, :] = w_ref[src:src + width, :].astype(BF16)
    off = 0
    for src, width in _SMALL_SRC:
        os_ref[off:off + width, :] = w_ref[src:src + width, :].astype(BF16)
        off += width


def _permute_in_weights(w_in):
    depth, d_model, in_width = w_in.shape
    w_t = jnp.swapaxes(w_in, 1, 2)
    tk = 128
    return pl.pallas_call(
        _wprep_kernel,
        grid=(depth, d_model // tk),
        in_specs=[pl.BlockSpec((None, in_width, tk), lambda l, i: (l, 0, i))],
        out_specs=[pl.BlockSpec((None, MAIN_WIDTH, tk), lambda l, i: (l, 0, i)),
                   pl.BlockSpec((None, SMALL_ROWS, tk), lambda l, i: (l, 0, i))],
        out_shape=[jax.ShapeDtypeStruct((depth, MAIN_WIDTH, d_model), BF16),
                   jax.ShapeDtypeStruct((depth, SMALL_ROWS, d_model), BF16)],
        name="weight_prep",
    )(w_t)


def _rope_tables(t_len):
    rot = HEAD_DIM // 2
    pos = jnp.arange(t_len)
    freqs = ROPE_BASE ** (-jnp.arange(rot // 2, dtype=F32) / (rot // 2))
    ang_row = (pos // GRID_W).astype(F32)[:, None] * freqs
    ang_col = (pos % GRID_W).astype(F32)[:, None] * freqs
    ang = jnp.concatenate([ang_row, ang_row, ang_col, ang_col], axis=1)
    sign = jnp.tile(jnp.concatenate([-jnp.ones((rot // 2,), F32), jnp.ones((rot // 2,), F32)]), 2)
    cos = jnp.tile(jnp.cos(ang), (1, 2))
    sin = jnp.tile(jnp.sin(ang) * sign, (1, 2))
    return cos, sin


def _layer_params(l, w_small, gdn_conv_w, gdn_a_log, gdn_dt_bias, gdn_norm_w, attn_q_norm, attn_k_norm,
                  ssm_conv_w, ssm_conv_b, ssm_a_log, ssm_dt_bias, ssm_d, ssm_norm_w,
                  w_branch_a, w_branch_b, w_branch_c, w_out):
    hd = GDN_HEADS
    w_small_t = w_small[l]
    zeros = jnp.zeros((hd,), F32)

    def gdn_col(a):
        return jnp.concatenate([zeros, a[0], zeros, a[1]]).reshape(4 * hd, 1)

    def ssd_col(a):
        g = a.reshape(2, SSM_HEADS // SSD_GROUP_HEADS, SSD_GROUP_HEADS)
        return jnp.concatenate([g[0], g[1]], axis=1)[..., None]

    return dict(
        w_small_t=w_small_t,
        gdn_conv_w=gdn_conv_w[l], gdn_a_log=gdn_col(gdn_a_log[l]), gdn_dt_bias=gdn_col(gdn_dt_bias[l]),
        gdn_norm_w=gdn_norm_w[l].reshape(1, LANES),
        qnw=jnp.tile(attn_q_norm[l], N_HEADS).reshape(1, ATTN_WIDTH),
        knw=jnp.tile(attn_k_norm[l], KV_HEADS).reshape(1, KV_WIDTH),
        ssm_conv_w=ssm_conv_w[l], ssm_conv_b=ssm_conv_b[l].reshape(1, SSM_XBC),
        ssm_a_log=ssd_col(ssm_a_log[l]), ssm_dt_bias=ssd_col(ssm_dt_bias[l]),
        ssm_d=jnp.repeat(ssm_d[l], SSM_HEAD_DIM).reshape(1, D_INNER),
        ssm_norm_w=ssm_norm_w[l].reshape(1, D_INNER),
        wa=w_branch_a[l].astype(BF16), wb=w_branch_b[l].astype(BF16),
        wc=w_branch_c[l].astype(BF16), wo=w_out[l].astype(BF16))


def _stream_layer(x, lp, mod_l, norm_w_l, final_norm_w, bd, rope_tabs, ctx, prev, *,
                  batch, t_len, mod_base, mod_per_batch, final):
    p = _in_proj(x, norm_w_l, mod_l, lp["w_main"], lp["w_small_t"],
                 t_len=t_len, mod_base=mod_base, mod_per_batch=mod_per_batch)
    ck, cv, sg, ss = ctx if ctx is not None else (None, None, None, None)
    prev_kv, prev_g, prev_s = prev if prev is not None else (None, None, None)
    o_a, new_sg = _gdn(p, lp["gdn_conv_w"], lp["gdn_norm_w"], lp["gdn_a_log"], lp["gdn_dt_bias"], sg, prev_g,
                       batch=batch, t_len=t_len)
    y_c, new_ss = _ssd(p, lp["ssm_conv_w"], lp["ssm_conv_b"], lp["ssm_d"], lp["ssm_a_log"],
                       lp["ssm_dt_bias"], ss, prev_s, batch=batch, t_len=t_len)
    o_b, new_kv = _attention(p, lp["qnw"], lp["knw"], bd, rope_tabs, ck, cv, prev_kv, batch=batch, t_len=t_len)
    x_new = _out_proj(x, o_a, o_b, y_c, p["gates"], mod_l, lp["ssm_norm_w"], lp["wa"], lp["wb"],
                      lp["wc"], lp["wo"], final_norm_w, t_len=t_len, mod_base=mod_base,
                      mod_per_batch=mod_per_batch, final=final)
    return x_new, (new_kv, new_sg, new_ss)


def kernel(x_prompt, x_sample, cache_k, cache_v, state_gdn, state_ssm, c, c_ctx, norm_w, w_mod, b_mod, w_in, gdn_conv_w, gdn_a_log, gdn_dt_bias, gdn_norm_w, attn_q_norm, attn_k_norm, ssm_conv_w, ssm_conv_b, ssm_a_log, ssm_dt_bias, ssm_d, ssm_norm_w, w_branch_a, w_branch_b, w_branch_c, w_out, final_norm_w):
    batch, seq, d_model = x_prompt.shape
    dec_batch, dec_seq, _ = x_sample.shape
    depth = w_in.shape[0]
    past = cache_k.shape[2]
    assert d_model == D_MODEL and seq % CHUNK == 0 and dec_seq % CHUNK == 0

    rows = -(-(1 + dec_batch) // 8) * 8
    cond = jnp.zeros((rows, D_MODEL), F32).at[0].set(c_ctx).at[1:1 + dec_batch].set(c)
    mod = _modulation(cond, w_mod, b_mod).reshape(depth, rows, 1, 3 * D_MODEL)

    seg = jnp.arange(ATTN_WIDTH) // HEAD_DIM
    bd = jnp.where(seg[:, None] == seg[None, :], 1.0 / HEAD_DIM, 0.0).astype(BF16)
    rope_tabs = _rope_tables(dec_seq)
    fnw = final_norm_w.reshape(1, D_MODEL)

    xp = x_prompt.reshape(batch * seq, D_MODEL)
    xs = x_sample.reshape(dec_batch * dec_seq, D_MODEL)
    new_cache = None
    w_perm, w_small = _permute_in_weights(w_in)
    for l in range(depth):
        lp = _layer_params(l, w_small, gdn_conv_w, gdn_a_log, gdn_dt_bias, gdn_norm_w, attn_q_norm,
                           attn_k_norm, ssm_conv_w, ssm_conv_b, ssm_a_log, ssm_dt_bias, ssm_d,
                           ssm_norm_w, w_branch_a, w_branch_b, w_branch_c, w_out)
        lp["w_main"] = (w_perm, l)
        nw = norm_w[l].reshape(1, D_MODEL)
        final = l == depth - 1
        xp, new_cache = _stream_layer(
            xp, lp, mod[l], nw, fnw, bd, None, None, new_cache,
            batch=batch, t_len=seq, mod_base=0, mod_per_batch=False, final=final)
        ctx = (cache_k[:, l].reshape(dec_batch, past, KV_WIDTH),
               cache_v[:, l].reshape(dec_batch, past, KV_WIDTH),
               (state_gdn, l), (state_ssm, l))
        xs, _ = _stream_layer(
            xs, lp, mod[l], nw, fnw, bd, rope_tabs, ctx, None,
            batch=dec_batch, t_len=dec_seq, mod_base=1, mod_per_batch=True, final=final)
    (new_k, new_v), new_gdn, new_ssm = new_cache
    return (xp.reshape(batch, seq, D_MODEL), xs.reshape(dec_batch, dec_seq, D_MODEL),
            new_k.reshape(batch, depth, seq, KV_HEADS, HEAD_DIM),
            new_v.reshape(batch, depth, seq, KV_HEADS, HEAD_DIM), new_gdn, new_ssm)
```

```python
import functools
import math

import jax
import jax.numpy as jnp
from jax import lax
from jax.experimental import pallas as pl
from jax.experimental.pallas import tpu as pltpu

F32 = jnp.float32
BF16 = jnp.bfloat16

D_MODEL = 1024
EPS = 1e-6
GDN_HEADS = 4
GDN_DK = 128
GDN_WIDTH = 512
N_HEADS = 8
KV_HEADS = 2
HEAD_DIM = 64
ATTN_WIDTH = 512
KV_WIDTH = 128
GRID_W = 64
ROPE_BASE = 10000.0
SSM_HEADS = 16
SSM_HEAD_DIM = 64
D_INNER = 1024
SSM_STATE = 128
SSM_XBC = 1536
CHUNK = 128
LANES = 128
SMALL_ROWS = 48
VMEM_LIMIT = 56 * 1024 * 1024

_SRC = dict(qkv=(0, 1536), az=(1536, 512), beta=(2048, 8), alpha=(2056, 8), bq=(2064, 512),
            kv=(2576, 256), bz=(2832, 512), xbc=(3344, 1536), cz=(4880, 1024), dt=(5904, 32),
            gates=(5936, 3072))
_MAIN = ("qkv", "az", "bq", "kv", "bz", "xbc", "cz", "gates")
_MAIN_OFF = {}
_o = 0
for _n in _MAIN:
    _MAIN_OFF[_n] = _o
    _o += _SRC[_n][1]
MAIN_WIDTH = _o


def _nt(a, b):
    return lax.dot_general(a, b, (((1,), (1,)), ((), ())), preferred_element_type=F32)


def _tn(a, b):
    return lax.dot_general(a, b, (((0,), (0,)), ((), ())), preferred_element_type=F32)


def _mm(a, b):
    return jnp.dot(a, b, preferred_element_type=F32)


def _sigmoid(x):
    return 0.5 * jnp.tanh(0.5 * x) + 0.5


def _silu(x):
    h = 0.5 * x
    return h * jnp.tanh(h) + h


def _softplus(x):
    return jnp.maximum(x, 0.0) + jnp.log1p(jnp.exp(-jnp.abs(x)))


def _split3(x):
    hi = x.astype(BF16)
    r1 = x - hi.astype(F32)
    mid = r1.astype(BF16)
    lo = (r1 - mid.astype(F32)).astype(BF16)
    return hi, mid, lo


def _cumsum_rows(x, tri):
    hi, mid, lo = _split3(x)
    return _mm(hi, tri) + _mm(mid, tri) + _mm(lo, tri)


def _rows_to_cols(x, eye_b):
    hi, mid, lo = _split3(x)
    return _nt(eye_b, hi) + _nt(eye_b, mid) + _nt(eye_b, lo)


def _conv_tile(src_ref, r, rt, n_tiles, w, bias, off):
    t_len = src_ref.shape[0]
    cols = slice(off, off + LANES)
    start = pl.multiple_of(r * rt, rt)
    cur = src_ref[pl.ds(start, rt), cols].astype(F32)
    pstart = pl.multiple_of(jnp.maximum(start - 16, 0), 16)
    nstart = pl.multiple_of(jnp.minimum(start + rt, t_len - 16), 16)
    prev_row = src_ref[pl.ds(pstart, 16), cols][15:16, :].astype(F32)
    next_row = src_ref[pl.ds(nstart, 16), cols][0:1, :].astype(F32)
    pos = lax.rem(r, n_tiles)
    prev_row = jnp.where(pos > 0, prev_row, 0.0)
    next_row = jnp.where(pos < n_tiles - 1, next_row, 0.0)
    ri = lax.broadcasted_iota(jnp.int32, (rt, LANES), 0)
    x_prev = jnp.where(ri == 0, prev_row, pltpu.roll(cur, 1, 0))
    x_next = jnp.where(ri == rt - 1, next_row, pltpu.roll(cur, rt - 1, 0))
    y = w[0:1, :] * x_prev + w[1:2, :] * cur + w[2:3, :] * x_next
    if bias is not None:
        y = y + bias
    return y


def _row_tile(t_len):
    return min(t_len, 256)


SCAN_STEP_ROWS = 4096
GDN_SLOTS = 2
SSD_SLOTS = 4


def _positions_per_body(half, max_slots):
    return max(n for n in (1, 2, 4) if half % n == 0 and n <= max_slots)


def _seqs_per_step(batch, t_len, max_slots):
    npar = _positions_per_body(t_len // CHUNK // 2, max_slots)
    return max(n for n in (1, 2, 4)
               if batch % n == 0 and n * npar <= max_slots and n * t_len <= SCAN_STEP_ROWS)


def _mod_kernel(c_ref, w_ref, b_ref, o_ref):
    s = _silu(c_ref[...])
    w = w_ref[...]
    s_hi = s.astype(BF16)
    s_lo = (s - s_hi.astype(F32)).astype(BF16)
    w_hi = w.astype(BF16)
    w_lo = (w - w_hi.astype(F32)).astype(BF16)
    o_ref[...] = _mm(s_hi, w_hi) + _mm(s_lo, w_hi) + _mm(s_hi, w_lo) + b_ref[...]


def _modulation(cond, w_mod, b_mod):
    depth = w_mod.shape[0]
    rows = cond.shape[0]
    tn = 1024
    return pl.pallas_call(
        _mod_kernel,
        grid=(depth, 3 * D_MODEL // tn),
        in_specs=[
            pl.BlockSpec((rows, D_MODEL), lambda l, j: (0, 0)),
            pl.BlockSpec((None, D_MODEL, tn), lambda l, j: (l, 0, j)),
            pl.BlockSpec((None, 1, tn), lambda l, j: (l, 0, j)),
        ],
        out_specs=pl.BlockSpec((None, rows, tn), lambda l, j: (l, 0, j)),
        out_shape=jax.ShapeDtypeStruct((depth, rows, 3 * D_MODEL), F32),
        name="modulation",
    )(cond, w_mod, b_mod.reshape(depth, 1, 3 * D_MODEL))


def _in_kernel(x_ref, nw_ref, mod_ref, w_ref, wst_ref,
               o_qkv, o_az, o_bq, o_kv, o_bz, o_xbc, o_cz, o_gates, o_small):
    x = x_ref[...]
    y = x * lax.rsqrt(jnp.mean(x * x, axis=-1, keepdims=True) + EPS) * nw_ref[...]
    shift = mod_ref[:, 0:D_MODEL]
    scale = mod_ref[:, D_MODEL:2 * D_MODEL]
    h = (y * (1.0 + scale) + shift).astype(BF16)
    outs = dict(qkv=o_qkv, az=o_az, bq=o_bq, kv=o_kv, bz=o_bz, xbc=o_xbc, cz=o_cz, gates=o_gates)
    for name in _MAIN:
        off, width = _MAIN_OFF[name], _SRC[name][1]
        outs[name][...] = _nt(h, w_ref[off:off + width, :]).astype(outs[name].dtype)
    o_small[...] = _nt(wst_ref[...], h)


def _in_proj(x, norm_w, mod, w_main, w_small_t, *, t_len, mod_base, mod_per_batch):
    n = x.shape[0]
    tm = 512 if t_len % 512 == 0 else 256
    assert n % tm == 0 and t_len % tm == 0
    w_all, layer = w_main

    def mod_idx(i):
        return (mod_base + (i * tm) // t_len if mod_per_batch else mod_base, 0, 0)

    widths = {k: _SRC[k][1] for k in _MAIN}
    dtypes = {k: (F32 if k == "kv" else BF16) for k in _MAIN}
    out_shape = [jax.ShapeDtypeStruct((n, widths[k]), dtypes[k]) for k in _MAIN]
    out_shape.append(jax.ShapeDtypeStruct((SMALL_ROWS, n), F32))
    out_specs = [pl.BlockSpec((tm, widths[k]), lambda i: (i, 0)) for k in _MAIN]
    out_specs.append(pl.BlockSpec((SMALL_ROWS, tm), lambda i: (0, i)))
    outs = pl.pallas_call(
        _in_kernel,
        grid=(n // tm,),
        in_specs=[
            pl.BlockSpec((tm, D_MODEL), lambda i: (i, 0)),
            pl.BlockSpec((1, D_MODEL), lambda i: (0, 0)),
            pl.BlockSpec((None, 1, 3 * D_MODEL), mod_idx),
            pl.BlockSpec((None, MAIN_WIDTH, D_MODEL), lambda i: (layer, 0, 0), pipeline_mode=pl.Buffered(1)),
            pl.BlockSpec((SMALL_ROWS, D_MODEL), lambda i: (0, 0)),
        ],
        out_specs=out_specs,
        out_shape=out_shape,
        compiler_params=pltpu.CompilerParams(
            dimension_semantics=("arbitrary",), vmem_limit_bytes=VMEM_LIMIT),
        name="in_proj",
    )(x, norm_w, mod, w_all, w_small_t)
    res = dict(zip(_MAIN, outs[:-1]))
    res["small_t"] = outs[-1]
    return res


INV_BASE = 8


def _tri_inverse(nms, dirs, sub, lane, eye_f):
    def same_block(s):
        sh = int(math.log2(s))
        return (sub >> sh) == (lane >> sh)

    base = same_block(INV_BASE)
    n8 = [jnp.where(base, nm, 0.0) for nm in nms]
    p = [n.astype(BF16) for n in n8]
    y = [eye_f - n for n in n8]
    p2 = [_mm(a, a).astype(BF16) for a in p]
    both = [_mm(a, jnp.concatenate([a, b.astype(BF16)], axis=1)) for a, b in zip(p2, y)]
    y = [b + c[:, CHUNK:] for b, c in zip(y, both)]
    p4 = [c[:, :CHUNK].astype(BF16) for c in both]
    y = [b + _mm(a, b.astype(BF16)) for a, b in zip(p4, y)]
    s = INV_BASE
    while s < CHUNK:
        sel = jnp.logical_and(same_block(2 * s), jnp.logical_not(same_block(s)))
        off = [jnp.where(sel, nm, 0.0).astype(BF16) for nm in nms]
        yb = [b.astype(BF16) for b in y]
        starts = [[2 * s * b + (s if d == 0 else 0) for b in range(CHUNK // (2 * s))] for d in dirs]
        rows = [jnp.concatenate([b[a:a + s] for a in st], axis=0).astype(BF16) for b, st in zip(y, starts)]
        t = [_mm(r, o).astype(BF16) for r, o in zip(rows, off)]
        upd = [_mm(a, c) for a, c in zip(t, yb)]
        new_y = []
        for b, u, st in zip(y, upd, starts):
            parts = []
            for n, a in enumerate(st):
                keep = b[a - s:a] if a % (2 * s) else b[a + s:a + 2 * s]
                changed = b[a:a + s] - u[n * s:(n + 1) * s]
                parts += [keep, changed] if a % (2 * s) else [changed, keep]
            new_y.append(jnp.concatenate(parts, axis=0))
        y = new_y
        s *= 2
    return y


def _gdn_kernel(*refs, t_len, nseq, has_ctx, n_prev):
    (q_ref, k_ref, v_ref, z_ref, gtf_ref, gtb_ref, wq_ref, wk_ref, wv_ref, nw_ref, alog_ref, bias_ref) = refs[:12]
    rest = refs[12:]
    s0_ref = prev_ref = sout_ref = None
    if has_ctx:
        s0_ref, o_ref, qs, ks, vs, oacc, st = rest
    else:
        if n_prev:
            prev_ref, rest = rest[0], rest[1:]
        o_ref, sout_ref, qs, ks, vs, oacc, st = rest
    rt = _row_tile(t_len)
    n_tiles = t_len // rt
    n_chunks = t_len // CHUNK
    chains = [(d, h) for d in range(2) for h in range(GDN_HEADS)]

    def prep(r, carry):
        start = pl.multiple_of(r * rt, rt)
        for h in range(GDN_HEADS):
            off = h * LANES
            q = _silu(_conv_tile(q_ref, r, rt, n_tiles, wq_ref[:, off:off + LANES], None, off))
            k = _silu(_conv_tile(k_ref, r, rt, n_tiles, wk_ref[:, off:off + LANES], None, off))
            v = _silu(_conv_tile(v_ref, r, rt, n_tiles, wv_ref[:, off:off + LANES], None, off))
            q = q * lax.rsqrt(jnp.sum(q * q, axis=-1, keepdims=True) + EPS) * (GDN_DK ** -0.5)
            k = k * lax.rsqrt(jnp.sum(k * k, axis=-1, keepdims=True) + EPS)
            qs[pl.ds(start, rt), off:off + LANES] = q.astype(BF16)
            ks[pl.ds(start, rt), off:off + LANES] = k.astype(BF16)
            vs[pl.ds(start, rt), off:off + LANES] = v.astype(BF16)
        return carry

    lax.fori_loop(0, nseq * n_tiles, prep, 0)

    nc = len(chains)
    for sq in range(nseq):
        for ci, (d, h) in enumerate(chains):
            st[sq * nc + ci] = s0_ref[sq, d, h] if has_ctx else jnp.zeros((GDN_DK, CHUNK), F32)

    sub = lax.broadcasted_iota(jnp.int32, (CHUNK, CHUNK), 0)
    lane = lax.broadcasted_iota(jnp.int32, (CHUNK, CHUNK), 1)
    eye = sub == lane
    eye_f = eye.astype(F32)
    incl = [lane <= sub, lane >= sub]
    strict = [lane < sub, lane > sub]
    eye_b = eye.astype(BF16)
    tri = [(sub <= lane).astype(BF16), (sub >= lane).astype(BF16)]
    grow = lax.broadcasted_iota(jnp.int32, (4 * GDN_HEADS, CHUNK), 0)
    is_beta = ((grow >> int(math.log2(GDN_HEADS))) & 1) == 0
    is_fwd = grow < 2 * GDN_HEADS
    neg_a = jnp.where(is_beta, 0.0, -jnp.exp(alog_ref[...]))
    beta_row = lambda d, h: 2 * d * GDN_HEADS + h
    decay_row = lambda d, h: (2 * d + 1) * GDN_HEADS + h

    half = n_chunks // 2
    npar = _positions_per_body(half, GDN_SLOTS)

    def step(i, second):
        slots = [(sq, k) for sq in range(nseq) for k in range(npar)]
        c0 = [[pl.multiple_of(sq * t_len + (i * npar + k) * CHUNK, CHUNK),
               pl.multiple_of(sq * t_len + (n_chunks - 1 - (i * npar + k)) * CHUNK, CHUNK)] for sq, k in slots]
        raw = [jnp.concatenate([gtf_ref[:, pl.ds(c[0], CHUNK)], gtb_ref[:, pl.ds(c[1], CHUNK)]], axis=0)
               for c in c0]
        g_all = [neg_a * _softplus(r + bias_ref[...]) for r in raw]
        gc_all = [jnp.where(is_fwd, _cumsum_rows(g, tri[0]), _cumsum_rows(g, tri[1])) for g in g_all]
        comb = [jnp.where(is_beta, _sigmoid(r), g) for r, g in zip(raw, gc_all)]
        comb_t = [_rows_to_cols(c, eye_b) for c in comb]
        ch = [(k, d, h) for k in range(len(slots)) for d, h in chains]
        cols = [slice(h * LANES, (h + 1) * LANES) for _, _, h in ch]
        qb = [qs[pl.ds(c0[k][d], CHUNK), cs] for (k, d, _), cs in zip(ch, cols)]
        kb = [ks[pl.ds(c0[k][d], CHUNK), cs] for (k, d, _), cs in zip(ch, cols)]
        vf = [vs[pl.ds(c0[k][d], CHUNK), cs].astype(F32) for (k, d, _), cs in zip(ch, cols)]
        qf = [a.astype(F32) for a in qb]
        kf = [a.astype(F32) for a in kb]
        gc = [comb[k][decay_row(d, h):decay_row(d, h) + 1, :] for k, d, h in ch]
        beta_col = [jnp.broadcast_to(comb_t[k][:, beta_row(d, h):beta_row(d, h) + 1], (CHUNK, CHUNK))
                    for k, d, h in ch]
        gc_col = [jnp.broadcast_to(comb_t[k][:, decay_row(d, h):decay_row(d, h) + 1], (CHUNK, CHUNK))
                  for k, d, h in ch]
        kk = [_nt(a, a) for a in kb]
        qk = [_nt(a, b) for a, b in zip(qb, kb)]
        dec = [jnp.where(incl[d], jnp.exp(jnp.where(incl[d], c - r, 0.0)), 0.0)
               for c, r, (_, d, _) in zip(gc_col, gc, ch)]
        nm = [jnp.where(strict[d], a * b * c, 0.0) for a, b, c, (_, d, _) in zip(kk, dec, beta_col, ch)]
        t_inv = _tri_inverse(nm, [d for _, d, _ in ch], sub, lane, eye_f)
        eg = [jnp.exp(a) for a in gc_col]
        rhs = [jnp.concatenate([a * b, c * (b * e)], axis=1).astype(BF16)
               for a, b, c, e in zip(vf, beta_col, kf, eg)]
        sol = [_mm(a.astype(BF16), b) for a, b in zip(t_inv, rhs)]
        tot = [a[:, CHUNK - 1:CHUNK] if d == 0 else a[:, 0:1] for a, (_, d, _) in zip(gc, ch)]
        q_dec = [(a * e).astype(BF16) for a, e in zip(qf, eg)]
        k_dec = [(a * jnp.exp(t - c)).astype(BF16) for a, t, c in zip(kf, tot, gc_col)]
        intra = [(a * b).astype(BF16) for a, b in zip(qk, dec)]
        state = [st[n] for n in range(nseq * nc)]
        for k in range(npar):
            idx = [(sq * npar + k) * nc + ci for sq in range(nseq) for ci in range(nc)]
            sidx = [sq * nc + ci for sq in range(nseq) for ci in range(nc)]
            sb = [state[n].astype(BF16) for n in sidx]
            v_new = [sol[m][:, :CHUNK] - _mm(sol[m][:, CHUNK:].astype(BF16), b) for m, b in zip(idx, sb)]
            vb = [a.astype(BF16) for a in v_new]
            o = [_mm(jnp.concatenate([q_dec[m], intra[m]], axis=1), jnp.concatenate([b, e], axis=0))
                 for m, b, e in zip(idx, sb, vb)]
            for m, n, e, ov in zip(idx, sidx, vb, o):
                state[n] = state[n] * jnp.exp(tot[m]) + _tn(k_dec[m], e)
                slot, d, h = ch[m]
                rows = pl.ds(c0[slot][d], CHUNK)
                cs = slice(h * LANES, (h + 1) * LANES)
                if not second:
                    oacc[rows, cs] = ov
                else:
                    ot = oacc[rows, cs] + ov
                    ot = ot * lax.rsqrt(jnp.mean(ot * ot, axis=-1, keepdims=True) + EPS) * nw_ref[...]
                    o_ref[rows, cs] = (ot * _silu(z_ref[rows, cs].astype(F32))).astype(o_ref.dtype)
        for n in range(nseq * nc):
            st[n] = state[n]

    lax.fori_loop(0, half // npar, lambda i, c: (step(i, False), c)[1], 0)
    lax.fori_loop(half // npar, n_chunks // npar, lambda i, c: (step(i, True), c)[1], 0)

    if sout_ref is not None:
        if n_prev:
            sout_ref[:, 0:n_prev] = prev_ref[...]
        for sq in range(nseq):
            for ci, (d, h) in enumerate(chains):
                sout_ref[sq, n_prev, d, h] = st[sq * nc + ci]


def _stacked_state_specs(prev, nseq, tail, index_map):
    n_prev = 0 if prev is None else prev.shape[1]
    in_spec = None if prev is None else pl.BlockSpec((nseq, n_prev) + tail, index_map)
    return n_prev, in_spec, pl.BlockSpec((nseq, n_prev + 1) + tail, index_map)


def _gdn(p, conv_w, norm_w, a_log, dt_bias, s0, prev, *, batch, t_len):
    n = batch * t_len
    has_ctx = s0 is not None
    assert (t_len // CHUNK) % 2 == 0
    nseq = _seqs_per_step(batch, t_len, GDN_SLOTS)
    rows = nseq * t_len
    big = dict(pipeline_mode=pl.Buffered(1)) if rows * GDN_WIDTH * 2 >= (2 << 20) else {}
    seq = lambda j: pl.BlockSpec((rows, GDN_WIDTH), lambda b, j=j: (b, j), **big)
    cw = lambda j: pl.BlockSpec((3, GDN_WIDTH), lambda b, j=j: (0, j))
    gate_rows = 2 * GDN_HEADS
    first = 2 * SSM_HEADS // gate_rows
    par = pl.BlockSpec((2 * gate_rows, 1), lambda b: (0, 0))
    in_specs = [seq(0), seq(1), seq(2), seq(0),
                pl.BlockSpec((gate_rows, rows), lambda b: (first, b)),
                pl.BlockSpec((gate_rows, rows), lambda b: (first + 1, b)),
                cw(0), cw(1), cw(2),
                pl.BlockSpec((1, LANES), lambda b: (0, 0)), par, par]
    args = [p["qkv"], p["qkv"], p["qkv"], p["az"], p["small_t"], p["small_t"], conv_w, conv_w, conv_w,
            norm_w, a_log, dt_bias]
    out_shape = [jax.ShapeDtypeStruct((n, GDN_WIDTH), BF16)]
    out_specs = [pl.BlockSpec((rows, GDN_WIDTH), lambda b: (b, 0))]
    n_prev = 0
    if has_ctx:
        s0_all, layer = s0
        in_specs.append(pl.BlockSpec((nseq, None, 2, GDN_HEADS, GDN_DK, CHUNK),
                                     lambda b: (b, layer, 0, 0, 0, 0)))
        args.append(s0_all)
    else:
        tail = (2, GDN_HEADS, GDN_DK, CHUNK)
        n_prev, prev_spec, out_spec = _stacked_state_specs(prev, nseq, tail, lambda b: (b, 0, 0, 0, 0, 0))
        if n_prev:
            in_specs.append(prev_spec)
            args.append(prev)
        out_shape.append(jax.ShapeDtypeStruct((batch, n_prev + 1) + tail, F32))
        out_specs.append(out_spec)
    scratch = [pltpu.VMEM((rows, GDN_WIDTH), BF16) for _ in range(3)]
    scratch += [pltpu.VMEM((rows, GDN_WIDTH), F32), pltpu.VMEM((nseq * 2 * GDN_HEADS, GDN_DK, CHUNK), F32)]
    outs = pl.pallas_call(
        functools.partial(_gdn_kernel, t_len=t_len, nseq=nseq, has_ctx=has_ctx, n_prev=n_prev),
        grid=(batch // nseq,),
        in_specs=in_specs, out_specs=out_specs, out_shape=out_shape, scratch_shapes=scratch,
        compiler_params=pltpu.CompilerParams(
            dimension_semantics=("arbitrary",), vmem_limit_bytes=VMEM_LIMIT),
        name="gdn_ctx" if has_ctx else "gdn",
    )(*args)
    return outs[0], (None if has_ctx else outs[1])


SSD_GROUP_HEADS = 8
SSD_PAIRS = SSD_GROUP_HEADS // 2
SSD_GROUP_WIDTH = SSD_GROUP_HEADS * SSM_HEAD_DIM


def _ssd_kernel(*refs, t_len, nseq, has_ctx, n_prev):
    (x_ref, b_ref, c_ref, z_ref, dtf_ref, dtb_ref, wx_ref, wb_ref, wc_ref, bx_ref, bb_ref, bc_ref,
     dvec_ref, alog_ref, bias_ref) = refs[:15]
    rest = refs[15:]
    s0_ref = prev_ref = sout_ref = None
    if has_ctx:
        s0_ref, o_ref, xs, bs, cs, yacc, st = rest
    else:
        if n_prev:
            prev_ref, rest = rest[0], rest[1:]
        o_ref, sout_ref, xs, bs, cs, yacc, st = rest
    rt = _row_tile(t_len)
    n_tiles = t_len // rt
    n_chunks = t_len // CHUNK
    chains = [(d, q) for d in range(2) for q in range(SSD_PAIRS)]

    def prep(r, carry):
        start = pl.multiple_of(r * rt, rt)
        for q in range(SSD_PAIRS):
            off = q * LANES
            xs[pl.ds(start, rt), off:off + LANES] = _silu(_conv_tile(
                x_ref, r, rt, n_tiles, wx_ref[:, off:off + LANES], bx_ref[:, off:off + LANES], off)).astype(BF16)
        bs[pl.ds(start, rt), :] = _silu(_conv_tile(b_ref, r, rt, n_tiles, wb_ref[...], bb_ref[...], 0)).astype(BF16)
        cs[pl.ds(start, rt), :] = _silu(_conv_tile(c_ref, r, rt, n_tiles, wc_ref[...], bc_ref[...], 0)).astype(BF16)
        return carry

    lax.fori_loop(0, nseq * n_tiles, prep, 0)

    nc = len(chains)
    for sq in range(nseq):
        for ci, (d, q) in enumerate(chains):
            if has_ctx:
                st[sq * nc + ci] = jnp.concatenate([s0_ref[sq, d, 2 * q], s0_ref[sq, d, 2 * q + 1]], axis=0)
            else:
                st[sq * nc + ci] = jnp.zeros((2 * SSM_HEAD_DIM, SSM_STATE), F32)

    sub = lax.broadcasted_iota(jnp.int32, (CHUNK, CHUNK), 0)
    lane = lax.broadcasted_iota(jnp.int32, (CHUNK, CHUNK), 1)
    eye = sub == lane
    incl = [lane <= sub, lane >= sub]
    head0_lane = lane < SSM_HEAD_DIM
    head0_sub = sub < SSM_HEAD_DIM
    eye_b = eye.astype(BF16)
    tri = [(sub <= lane).astype(BF16), (sub >= lane).astype(BF16)]
    is_fwd = lax.broadcasted_iota(jnp.int32, (2 * SSD_GROUP_HEADS, CHUNK), 0) < SSD_GROUP_HEADS
    lane_r = lax.broadcasted_iota(jnp.int32, (2 * SSD_GROUP_HEADS, CHUNK), 1)
    neg_a = -jnp.exp(alog_ref[...])
    hrow = lambda d, q, j: d * SSD_GROUP_HEADS + 2 * q + j

    half = n_chunks // 2
    npar = _positions_per_body(half, SSD_SLOTS)

    def step(i, second_half):
        slots = [(sq, k) for sq in range(nseq) for k in range(npar)]
        c0 = [[pl.multiple_of(sq * t_len + (i * npar + k) * CHUNK, CHUNK),
               pl.multiple_of(sq * t_len + (n_chunks - 1 - (i * npar + k)) * CHUNK, CHUNK)] for sq, k in slots]
        kd = [(k, d) for k in range(len(slots)) for d in range(2)]
        bm = {(k, d): bs[pl.ds(c0[k][d], CHUNK), :] for k, d in kd}
        cm = {(k, d): cs[pl.ds(c0[k][d], CHUNK), :] for k, d in kd}
        cb = {key: _nt(cm[key], bm[key]) for key in kd}
        raw = [jnp.concatenate([dtf_ref[:, pl.ds(c[0], CHUNK)], dtb_ref[:, pl.ds(c[1], CHUNK)]], axis=0)
               for c in c0]
        dt_all = [_softplus(r + bias_ref[...]) for r in raw]
        da_all = [a * neg_a for a in dt_all]
        acum_all = [jnp.where(is_fwd, _cumsum_rows(a, tri[0]), _cumsum_rows(a, tri[1])) for a in da_all]
        tot_all = [jnp.sum(jnp.where(lane_r == jnp.where(is_fwd, CHUNK - 1, 0), a, 0.0), axis=1, keepdims=True)
                   for a in acum_all]
        w_all = [a * jnp.exp(t - c) for a, t, c in zip(dt_all, tot_all, acum_all)]
        cols_t = [_rows_to_cols(a, eye_b) for a in acum_all]
        cd_all = [jnp.exp(t) for t in tot_all]
        shifted_all = [a - jnp.log(t) for a, t in zip(acum_all, dt_all)]
        order = [(sq * npar + k, sq, d, q) for k in range(npar) for sq in range(nseq) for d, q in chains]
        state = {(sq, d, q): st[sq * nc + ci] for sq in range(nseq) for ci, (d, q) in enumerate(chains)}

        def first(slot, sq, d, q):
            cs_ = slice(q * LANES, (q + 1) * LANES)
            xb = xs[pl.ds(c0[slot][d], CHUNK), cs_]
            x01 = [jnp.where(head0_lane, xb, jnp.zeros_like(xb)), jnp.where(head0_lane, jnp.zeros_like(xb), xb)]
            r = [hrow(d, q, j) for j in range(2)]
            acum_b = [jnp.broadcast_to(cols_t[slot][:, a:a + 1], (CHUNK, CHUNK)) for a in r]
            m_mat = [(cb[(slot, d)] * jnp.where(incl[d], jnp.exp(c - shifted_all[slot][a:a + 1, :]), 0.0)
                      ).astype(BF16) for c, a in zip(acum_b, r)]
            w_diag = [jnp.where(eye, w_all[slot][a:a + 1, :], 0.0).astype(BF16) for a in r]
            y = _mm(m_mat[0], x01[0]) + _mm(m_mat[1], x01[1])
            xw = _mm(w_diag[0], x01[0]) + _mm(w_diag[1], x01[1])
            din = jnp.where(head0_lane, jnp.exp(acum_b[0]), jnp.exp(acum_b[1]))
            cd = jnp.where(head0_sub, cd_all[slot][r[0]:r[0] + 1, :], cd_all[slot][r[1]:r[1] + 1, :])
            return xb, y, xw, din, cd

        def second(slot, sq, d, q, xb, y, xw, din, cd):
            cs_ = slice(q * LANES, (q + 1) * LANES)
            s_old = state[(sq, d, q)]
            yk = y + _nt(cm[(slot, d)], s_old.astype(BF16)) * din
            state[(sq, d, q)] = s_old * cd + _tn(xw.astype(BF16), bm[(slot, d)])
            rows = pl.ds(c0[slot][d], CHUNK)
            if not second_half:
                yacc[rows, cs_] = yk
            else:
                yt = yacc[rows, cs_] + yk + dvec_ref[:, cs_] * xb.astype(F32)
                o_ref[rows, cs_] = (yt * _silu(z_ref[rows, cs_].astype(F32))).astype(o_ref.dtype)

        pending = None
        for item in order:
            res = first(*item)
            if pending is not None:
                second(*pending)
            pending = item + res
        second(*pending)
        for sq in range(nseq):
            for ci, (d, q) in enumerate(chains):
                st[sq * nc + ci] = state[(sq, d, q)]

    lax.fori_loop(0, half // npar, lambda i, c: (step(i, False), c)[1], 0)
    lax.fori_loop(half // npar, n_chunks // npar, lambda i, c: (step(i, True), c)[1], 0)

    if sout_ref is not None:
        if n_prev:
            sout_ref[:, 0:n_prev] = prev_ref[...]
        for sq in range(nseq):
            for ci, (d, q) in enumerate(chains):
                s = st[sq * nc + ci]
                sout_ref[sq, n_prev, d, 2 * q] = s[:SSM_HEAD_DIM]
                sout_ref[sq, n_prev, d, 2 * q + 1] = s[SSM_HEAD_DIM:]


def _ssd(p, conv_w, conv_b, d_vec, a_log, dt_bias, s0, prev, *, batch, t_len):
    n = batch * t_len
    has_ctx = s0 is not None
    assert (t_len // CHUNK) % 2 == 0
    n_groups = SSM_HEADS // SSD_GROUP_HEADS
    nseq = _seqs_per_step(batch, t_len, SSD_SLOTS)
    rows = nseq * t_len
    x_blocks = D_INNER // LANES
    big = dict(pipeline_mode=pl.Buffered(1)) if rows * SSD_GROUP_WIDTH * 2 >= (2 << 20) else {}
    par = pl.BlockSpec((None, 2 * SSD_GROUP_HEADS, 1), lambda b, g: (g, 0, 0))
    in_specs = [pl.BlockSpec((rows, SSD_GROUP_WIDTH), lambda b, g: (b, g), **big),
                pl.BlockSpec((rows, LANES), lambda b, g: (b, x_blocks + g)),
                pl.BlockSpec((rows, LANES), lambda b, g: (b, x_blocks + n_groups + g)),
                pl.BlockSpec((rows, SSD_GROUP_WIDTH), lambda b, g: (b, g), **big),
                pl.BlockSpec((SSD_GROUP_HEADS, rows), lambda b, g: (g, b)),
                pl.BlockSpec((SSD_GROUP_HEADS, rows), lambda b, g: (n_groups + g, b)),
                pl.BlockSpec((3, SSD_GROUP_WIDTH), lambda b, g: (0, g)),
                pl.BlockSpec((3, LANES), lambda b, g: (0, x_blocks + g)),
                pl.BlockSpec((3, LANES), lambda b, g: (0, x_blocks + n_groups + g)),
                pl.BlockSpec((1, SSD_GROUP_WIDTH), lambda b, g: (0, g)),
                pl.BlockSpec((1, LANES), lambda b, g: (0, x_blocks + g)),
                pl.BlockSpec((1, LANES), lambda b, g: (0, x_blocks + n_groups + g)),
                pl.BlockSpec((1, SSD_GROUP_WIDTH), lambda b, g: (0, g)), par, par]
    args = [p["xbc"], p["xbc"], p["xbc"], p["cz"], p["small_t"], p["small_t"], conv_w, conv_w, conv_w,
            conv_b, conv_b, conv_b, d_vec, a_log, dt_bias]
    out_shape = [jax.ShapeDtypeStruct((n, D_INNER), BF16)]
    out_specs = [pl.BlockSpec((rows, SSD_GROUP_WIDTH), lambda b, g: (b, g))]
    n_prev = 0
    if has_ctx:
        s0_all, layer = s0
        in_specs.append(pl.BlockSpec((nseq, None, 2, SSD_GROUP_HEADS, SSM_HEAD_DIM, SSM_STATE),
                                     lambda b, g: (b, layer, 0, g, 0, 0)))
        args.append(s0_all)
    else:
        n_prev, prev_spec, out_spec = _stacked_state_specs(
            prev, nseq, (2, SSD_GROUP_HEADS, SSM_HEAD_DIM, SSM_STATE), lambda b, g: (b, 0, 0, g, 0, 0))
        if n_prev:
            in_specs.append(prev_spec)
            args.append(prev)
        out_shape.append(jax.ShapeDtypeStruct((batch, n_prev + 1, 2, SSM_HEADS, SSM_HEAD_DIM, SSM_STATE), F32))
        out_specs.append(out_spec)
    scratch = [pltpu.VMEM((rows, SSD_GROUP_WIDTH), BF16), pltpu.VMEM((rows, LANES), BF16),
               pltpu.VMEM((rows, LANES), BF16), pltpu.VMEM((rows, SSD_GROUP_WIDTH), F32),
               pltpu.VMEM((nseq * 2 * SSD_PAIRS, 2 * SSM_HEAD_DIM, SSM_STATE), F32)]
    outs = pl.pallas_call(
        functools.partial(_ssd_kernel, t_len=t_len, nseq=nseq, has_ctx=has_ctx, n_prev=n_prev),
        grid=(batch // nseq, n_groups),
        in_specs=in_specs, out_specs=out_specs, out_shape=out_shape, scratch_shapes=scratch,
        compiler_params=pltpu.CompilerParams(
            dimension_semantics=("arbitrary", "arbitrary"), vmem_limit_bytes=VMEM_LIMIT),
        name="ssd_ctx" if has_ctx else "ssd",
    )(*args)
    return outs[0], (None if has_ctx else outs[1])


KEY_TILE = 256


def _seg_rmsnorm(x, bd):
    x2 = x * x
    hi = x2.astype(BF16)
    lo = (x2 - hi.astype(F32)).astype(BF16)
    ms = _mm(hi, bd) + _mm(lo, bd)
    return x * lax.rsqrt(ms + EPS)


def _rope128(x, cos, sin_signed):
    lane = lax.broadcasted_iota(jnp.int32, x.shape, 1)
    first = (lane % 32) < 16
    swapped = jnp.where(first, pltpu.roll(x, LANES - 16, 1), pltpu.roll(x, 16, 1))
    return x * cos + swapped * sin_signed


def _attn_kernel(*refs, t_len, tq, past, has_ctx, n_prev):
    q_ref, kv_ref, z_ref, qnw_ref, knw_ref, bd_ref = refs[:6]
    rest = refs[6:]
    kn_ref = vn_ref = pk_ref = pv_ref = None
    if has_ctx:
        cosq_ref, sinq_ref, cosk_ref, sink_ref, ck_ref, cv_ref, o_ref, k_s, v_s = rest
    else:
        if n_prev:
            pk_ref, pv_ref, rest = rest[0], rest[1], rest[2:]
        o_ref, kn_ref, vn_ref, k_s, v_s = rest
    qi = pl.program_id(1)
    rt = _row_tile(t_len)
    n_tiles = t_len // rt

    def store_values(rows, v):
        low_v = lax.broadcasted_iota(jnp.int32, v.shape, 1) < HEAD_DIM
        v_s[0, rows, :] = jnp.where(low_v, v, 1.0).astype(BF16)
        v_s[1, rows, :] = jnp.where(low_v, 1.0, v).astype(BF16)

    @pl.when(qi == 0)
    def _():
        def prep(r, carry):
            start = pl.multiple_of(r * rt, rt)
            kv = kv_ref[pl.ds(start, rt), :]
            kn = _seg_rmsnorm(kv[:, :KV_WIDTH], bd_ref[0:KV_WIDTH, 0:KV_WIDTH]) * knw_ref[...]
            if kn_ref is not None:
                kn_ref[n_prev, pl.ds(start, rt), :] = kn
                vn_ref[n_prev, pl.ds(start, rt), :] = kv[:, KV_WIDTH:]
            if has_ctx:
                kn = _rope128(kn, cosk_ref[pl.ds(start, rt), :], sink_ref[pl.ds(start, rt), :])
            k_s[pl.ds(start, rt), :] = kn.astype(BF16)
            store_values(pl.ds(start, rt), kv[:, KV_WIDTH:])
            return carry

        lax.fori_loop(0, n_tiles, prep, 0)
        if n_prev:
            kn_ref[0:n_prev] = pk_ref[...]
            vn_ref[0:n_prev] = pv_ref[...]
        if has_ctx:
            k_s[t_len:t_len + past, :] = ck_ref[...].astype(BF16)
            store_values(slice(t_len, t_len + past), cv_ref[...])

    q = _seg_rmsnorm(q_ref[...].astype(F32), bd_ref[...]) * qnw_ref[...]
    blocks = []
    for cbk in range(ATTN_WIDTH // LANES):
        blk = q[:, cbk * LANES:(cbk + 1) * LANES]
        if has_ctx:
            blk = _rope128(blk, cosq_ref[...], sinq_ref[...])
        blocks.append((blk * (HEAD_DIM ** -0.5)).astype(BF16))
    lane = lax.broadcasted_iota(jnp.int32, (tq, LANES), 1)
    low = lane < HEAD_DIM
    group = N_HEADS // KV_HEADS
    head_out = [None] * N_HEADS
    qgs = []
    for kvh in range(KV_HEADS):
        keep = low if kvh == 0 else jnp.logical_not(low)
        rows = []
        for g in range(group):
            head = kvh * group + g
            blk = blocks[head // 2]
            if head % 2 != kvh:
                blk = pltpu.roll(blk, HEAD_DIM, 1)
            rows.append(jnp.where(keep, blk, jnp.zeros_like(blk)))
        qgs.append(jnp.concatenate(rows, axis=0))

    n_kt = (t_len + past) // KEY_TILE
    keys = lambda j: slice(j * KEY_TILE, (j + 1) * KEY_TILE)
    score_tile = lambda kvh, j: _nt(qgs[kvh], k_s[keys(j), :]).astype(BF16)
    row_max = lambda tiles: jnp.max(functools.reduce(jnp.maximum, tiles), axis=-1, keepdims=True)
    s_tiles = [[score_tile(0, j) for j in range(n_kt)], []]
    m = [row_max(s_tiles[0]), None]
    p_tiles = [[], []]

    def exp_tile(kvh, j):
        p_tiles[kvh].append(jnp.exp(s_tiles[kvh][j] - m[kvh]))

    acc = [None, None]

    def pv_tile(kvh, j):
        part = _mm(p_tiles[kvh][j], v_s[kvh, keys(j), :])
        acc[kvh] = part if acc[kvh] is None else acc[kvh] + part

    for j in range(n_kt):
        s_tiles[1].append(score_tile(1, j))
        exp_tile(0, j)
    m[1] = row_max(s_tiles[1])
    for j in range(n_kt):
        pv_tile(0, j)
        exp_tile(1, j)
    for j in range(n_kt):
        pv_tile(1, j)
    for kvh in range(KV_HEADS):
        ones_lane = HEAD_DIM if kvh == 0 else 0
        o = acc[kvh] / acc[kvh][:, ones_lane:ones_lane + 1]
        for g in range(group):
            head = kvh * group + g
            og = o[g * tq:(g + 1) * tq, :]
            if head % 2 != kvh:
                og = pltpu.roll(og, HEAD_DIM, 1)
            head_out[head] = og
    outs = [jnp.where(low, head_out[2 * i], head_out[2 * i + 1]) for i in range(N_HEADS // 2)]
    o_all = jnp.concatenate(outs, axis=1)
    o_ref[...] = (o_all * _silu(z_ref[...].astype(F32))).astype(o_ref.dtype)


def _attention(p, qnw, knw, bd, rope_tabs, cache_k, cache_v, prev_kv, *, batch, t_len):
    n = batch * t_len
    has_ctx = cache_k is not None
    past = cache_k.shape[1] if has_ctx else 0
    n_prev = 0
    tq = min(t_len, 256)
    nq = t_len // tq
    once = dict(pipeline_mode=pl.Buffered(1)) if t_len * 2 * KV_WIDTH * 4 >= (2 << 20) else {}
    in_specs = [pl.BlockSpec((tq, ATTN_WIDTH), lambda b, i: (b * nq + i, 0)),
                pl.BlockSpec((t_len, 2 * KV_WIDTH), lambda b, i: (b, 0), **once),
                pl.BlockSpec((tq, ATTN_WIDTH), lambda b, i: (b * nq + i, 0)),
                pl.BlockSpec((1, ATTN_WIDTH), lambda b, i: (0, 0)),
                pl.BlockSpec((1, KV_WIDTH), lambda b, i: (0, 0)),
                pl.BlockSpec((ATTN_WIDTH, ATTN_WIDTH), lambda b, i: (0, 0))]
    args = [p["bq"], p["kv"], p["bz"], qnw, knw, bd]
    out_shape = [jax.ShapeDtypeStruct((n, ATTN_WIDTH), BF16)]
    out_specs = [pl.BlockSpec((tq, ATTN_WIDTH), lambda b, i: (b * nq + i, 0))]
    if has_ctx:
        cos, sin = rope_tabs
        in_specs += [pl.BlockSpec((tq, LANES), lambda b, i: (i, 0)),
                     pl.BlockSpec((tq, LANES), lambda b, i: (i, 0)),
                     pl.BlockSpec((t_len, LANES), lambda b, i: (0, 0), **once),
                     pl.BlockSpec((t_len, LANES), lambda b, i: (0, 0), **once),
                     pl.BlockSpec((None, past, KV_WIDTH), lambda b, i: (b, 0, 0)),
                     pl.BlockSpec((None, past, KV_WIDTH), lambda b, i: (b, 0, 0))]
        args += [cos, sin, cos, sin, cache_k, cache_v]
    else:
        n_prev = 0 if prev_kv is None else prev_kv[0].shape[1]
        if n_prev:
            in_specs += [pl.BlockSpec((None, n_prev, t_len, KV_WIDTH), lambda b, i: (b, 0, 0, 0))] * 2
            args += list(prev_kv)
        out_shape += [jax.ShapeDtypeStruct((batch, n_prev + 1, t_len, KV_WIDTH), F32)] * 2
        out_specs += [pl.BlockSpec((None, n_prev + 1, t_len, KV_WIDTH), lambda b, i: (b, 0, 0, 0))] * 2
    scratch = [pltpu.VMEM((t_len + past, KV_WIDTH), BF16),
               pltpu.VMEM((KV_HEADS, t_len + past, KV_WIDTH), BF16)]
    outs = pl.pallas_call(
        functools.partial(_attn_kernel, t_len=t_len, tq=tq, past=past, has_ctx=has_ctx, n_prev=n_prev),
        grid=(batch, nq),
        in_specs=in_specs, out_specs=out_specs, out_shape=out_shape, scratch_shapes=scratch,
        compiler_params=pltpu.CompilerParams(
            dimension_semantics=("arbitrary", "arbitrary"), vmem_limit_bytes=VMEM_LIMIT),
        name="attn_ctx" if has_ctx else "attn",
    )(*args)
    return outs[0], (None if has_ctx else (outs[1], outs[2]))


def _out_kernel(x_ref, oa_ref, ob_ref, yc_ref, g_ref, mod_ref, snw_ref, wa_ref, wb_ref, wc_ref, wo_ref,
                fnw_ref, o_ref, *, final):
    ba = _mm(oa_ref[...], wa_ref[...])
    bb = _mm(ob_ref[...], wb_ref[...])
    yc = yc_ref[...].astype(F32)
    yn = yc * lax.rsqrt(jnp.mean(yc * yc, axis=-1, keepdims=True) + EPS) * snw_ref[...]
    bc = _mm(yn.astype(BF16), wc_ref[...])
    g = g_ref[...].astype(F32)
    merged = (_sigmoid(g[:, 0:D_MODEL]) * ba + _sigmoid(g[:, D_MODEL:2 * D_MODEL]) * bb
              + _sigmoid(g[:, 2 * D_MODEL:]) * bc)
    out = _mm(merged.astype(BF16), wo_ref[...])
    xn = x_ref[...] + mod_ref[:, 2 * D_MODEL:] * out
    if final:
        xn = xn * lax.rsqrt(jnp.mean(xn * xn, axis=-1, keepdims=True) + EPS) * fnw_ref[...]
    o_ref[...] = xn


def _out_proj(x, o_a, o_b, y_c, gates, mod, ssm_norm_w, wa, wb, wc, wo, final_norm_w, *,
              t_len, mod_base, mod_per_batch, final):
    n = x.shape[0]
    tm = 512 if t_len % 512 == 0 else 256
    assert n % tm == 0 and t_len % tm == 0

    def mod_idx(i):
        return (mod_base + (i * tm) // t_len if mod_per_batch else mod_base, 0, 0)

    tok = lambda w: pl.BlockSpec((tm, w), lambda i: (i, 0))
    full = lambda a, b: pl.BlockSpec((a, b), lambda i: (0, 0))
    return pl.pallas_call(
        functools.partial(_out_kernel, final=final),
        grid=(n // tm,),
        in_specs=[tok(D_MODEL), tok(GDN_WIDTH), tok(ATTN_WIDTH), tok(D_INNER), tok(3 * D_MODEL),
                  pl.BlockSpec((None, 1, 3 * D_MODEL), mod_idx), full(1, D_INNER),
                  full(GDN_WIDTH, D_MODEL), full(ATTN_WIDTH, D_MODEL), full(D_INNER, D_MODEL),
                  full(D_MODEL, D_MODEL), full(1, D_MODEL)],
        out_specs=tok(D_MODEL),
        out_shape=jax.ShapeDtypeStruct((n, D_MODEL), F32),
        compiler_params=pltpu.CompilerParams(
            dimension_semantics=("arbitrary",), vmem_limit_bytes=VMEM_LIMIT),
        name="out_proj",
    )(x, o_a, o_b, y_c, gates, mod, ssm_norm_w, wa, wb, wc, wo, final_norm_w)


_SMALL_SRC = ((_SRC["dt"][0], 2 * SSM_HEADS),
              (_SRC["beta"][0], GDN_HEADS), (_SRC["alpha"][0], GDN_HEADS),
              (_SRC["beta"][0] + GDN_HEADS, GDN_HEADS), (_SRC["alpha"][0] + GDN_HEADS, GDN_HEADS))


def _wprep_kernel(w_ref, o_ref, os_ref):
    for name in _MAIN:
        src, width = _SRC[name]
        off = _MAIN_OFF[name]
        o_ref[off:off + width, :] = w_ref[src:src + width, :].astype(BF16)
    off = 0
    for src, width in _SMALL_SRC:
        os_ref[off:off + width, :] = w_ref[src:src + width, :].astype(BF16)
        off += width


def _permute_in_weights(w_in):
    depth, d_model, in_width = w_in.shape
    w_t = jnp.swapaxes(w_in, 1, 2)
    tk = 128
    return pl.pallas_call(
        _wprep_kernel,
        grid=(depth, d_model // tk),
        in_specs=[pl.BlockSpec((None, in_width, tk), lambda l, i: (l, 0, i))],
        out_specs=[pl.BlockSpec((None, MAIN_WIDTH, tk), lambda l, i: (l, 0, i)),
                   pl.BlockSpec((None, SMALL_ROWS, tk), lambda l, i: (l, 0, i))],
        out_shape=[jax.ShapeDtypeStruct((depth, MAIN_WIDTH, d_model), BF16),
                   jax.ShapeDtypeStruct((depth, SMALL_ROWS, d_model), BF16)],
        name="weight_prep",
    )(w_t)


def _rope_tables(t_len):
    rot = HEAD_DIM // 2
    pos = jnp.arange(t_len)
    freqs = ROPE_BASE ** (-jnp.arange(rot // 2, dtype=F32) / (rot // 2))
    ang_row = (pos // GRID_W).astype(F32)[:, None] * freqs
    ang_col = (pos % GRID_W).astype(F32)[:, None] * freqs
    ang = jnp.concatenate([ang_row, ang_row, ang_col, ang_col], axis=1)
    sign = jnp.tile(jnp.concatenate([-jnp.ones((rot // 2,), F32), jnp.ones((rot // 2,), F32)]), 2)
    cos = jnp.tile(jnp.cos(ang), (1, 2))
    sin = jnp.tile(jnp.sin(ang) * sign, (1, 2))
    return cos, sin


def _layer_params(l, w_small, gdn_conv_w, gdn_a_log, gdn_dt_bias, gdn_norm_w, attn_q_norm, attn_k_norm,
                  ssm_conv_w, ssm_conv_b, ssm_a_log, ssm_dt_bias, ssm_d, ssm_norm_w,
                  w_branch_a, w_branch_b, w_branch_c, w_out):
    hd = GDN_HEADS
    w_small_t = w_small[l]
    zeros = jnp.zeros((hd,), F32)

    def gdn_col(a):
        return jnp.concatenate([zeros, a[0], zeros, a[1]]).reshape(4 * hd, 1)

    def ssd_col(a):
        g = a.reshape(2, SSM_HEADS // SSD_GROUP_HEADS, SSD_GROUP_HEADS)
        return jnp.concatenate([g[0], g[1]], axis=1)[..., None]

    return dict(
        w_small_t=w_small_t,
        gdn_conv_w=gdn_conv_w[l], gdn_a_log=gdn_col(gdn_a_log[l]), gdn_dt_bias=gdn_col(gdn_dt_bias[l]),
        gdn_norm_w=gdn_norm_w[l].reshape(1, LANES),
        qnw=jnp.tile(attn_q_norm[l], N_HEADS).reshape(1, ATTN_WIDTH),
        knw=jnp.tile(attn_k_norm[l], KV_HEADS).reshape(1, KV_WIDTH),
        ssm_conv_w=ssm_conv_w[l], ssm_conv_b=ssm_conv_b[l].reshape(1, SSM_XBC),
        ssm_a_log=ssd_col(ssm_a_log[l]), ssm_dt_bias=ssd_col(ssm_dt_bias[l]),
        ssm_d=jnp.repeat(ssm_d[l], SSM_HEAD_DIM).reshape(1, D_INNER),
        ssm_norm_w=ssm_norm_w[l].reshape(1, D_INNER),
        wa=w_branch_a[l].astype(BF16), wb=w_branch_b[l].astype(BF16),
        wc=w_branch_c[l].astype(BF16), wo=w_out[l].astype(BF16))


def _stream_layer(x, lp, mod_l, norm_w_l, final_norm_w, bd, rope_tabs, ctx, prev, *,
                  batch, t_len, mod_base, mod_per_batch, final):
    p = _in_proj(x, norm_w_l, mod_l, lp["w_main"], lp["w_small_t"],
                 t_len=t_len, mod_base=mod_base, mod_per_batch=mod_per_batch)
    ck, cv, sg, ss = ctx if ctx is not None else (None, None, None, None)
    prev_kv, prev_g, prev_s = prev if prev is not None else (None, None, None)
    o_a, new_sg = _gdn(p, lp["gdn_conv_w"], lp["gdn_norm_w"], lp["gdn_a_log"], lp["gdn_dt_bias"], sg, prev_g,
                       batch=batch, t_len=t_len)
    y_c, new_ss = _ssd(p, lp["ssm_conv_w"], lp["ssm_conv_b"], lp["ssm_d"], lp["ssm_a_log"],
                       lp["ssm_dt_bias"], ss, prev_s, batch=batch, t_len=t_len)
    o_b, new_kv = _attention(p, lp["qnw"], lp["knw"], bd, rope_tabs, ck, cv, prev_kv, batch=batch, t_len=t_len)
    x_new = _out_proj(x, o_a, o_b, y_c, p["gates"], mod_l, lp["ssm_norm_w"], lp["wa"], lp["wb"],
                      lp["wc"], lp["wo"], final_norm_w, t_len=t_len, mod_base=mod_base,
                      mod_per_batch=mod_per_batch, final=final)
    return x_new, (new_kv, new_sg, new_ss)


def kernel(x_prompt, x_sample, cache_k, cache_v, state_gdn, state_ssm, c, c_ctx, norm_w, w_mod, b_mod, w_in, gdn_conv_w, gdn_a_log, gdn_dt_bias, gdn_norm_w, attn_q_norm, attn_k_norm, ssm_conv_w, ssm_conv_b, ssm_a_log, ssm_dt_bias, ssm_d, ssm_norm_w, w_branch_a, w_branch_b, w_branch_c, w_out, final_norm_w):
    batch, seq, d_model = x_prompt.shape
    dec_batch, dec_seq, _ = x_sample.shape
    depth = w_in.shape[0]
    past = cache_k.shape[2]
    assert d_model == D_MODEL and seq % CHUNK == 0 and dec_seq % CHUNK == 0

    rows = -(-(1 + dec_batch) // 8) * 8
    cond = jnp.zeros((rows, D_MODEL), F32).at[0].set(c_ctx).at[1:1 + dec_batch].set(c)
    mod = _modulation(cond, w_mod, b_mod).reshape(depth, rows, 1, 3 * D_MODEL)

    seg = jnp.arange(ATTN_WIDTH) // HEAD_DIM
    bd = jnp.where(seg[:, None] == seg[None, :], 1.0 / HEAD_DIM, 0.0).astype(BF16)
    rope_tabs = _rope_tables(dec_seq)
    fnw = final_norm_w.reshape(1, D_MODEL)

    xp = x_prompt.reshape(batch * seq, D_MODEL)
    xs = x_sample.reshape(dec_batch * dec_seq, D_MODEL)
    new_cache = None
    w_perm, w_small = _permute_in_weights(w_in)
    for l in range(depth):
        lp = _layer_params(l, w_small, gdn_conv_w, gdn_a_log, gdn_dt_bias, gdn_norm_w, attn_q_norm,
                           attn_k_norm, ssm_conv_w, ssm_conv_b, ssm_a_log, ssm_dt_bias, ssm_d,
                           ssm_norm_w, w_branch_a, w_branch_b, w_branch_c, w_out)
        lp["w_main"] = (w_perm, l)
        nw = norm_w[l].reshape(1, D_MODEL)
        final = l == depth - 1
        xp, new_cache = _stream_layer(
            xp, lp, mod[l], nw, fnw, bd, None, None, new_cache,
            batch=batch, t_len=seq, mod_base=0, mod_per_batch=False, final=final)
        ctx = (cache_k[:, l].reshape(dec_batch, past, KV_WIDTH),
               cache_v[:, l].reshape(dec_batch, past, KV_WIDTH),
               (state_gdn, l), (state_ssm, l))
        xs, _ = _stream_layer(
            xs, lp, mod[l], nw, fnw, bd, rope_tabs, ctx, None,
            batch=dec_batch, t_len=dec_seq, mod_base=1, mod_per_batch=True, final=final)
    (new_k, new_v), new_gdn, new_ssm = new_cache
    return (xp.reshape(batch, seq, D_MODEL), xs.reshape(dec_batch, dec_seq, D_MODEL),
            new_k.reshape(batch, depth, seq, KV_HEADS, HEAD_DIM),
            new_v.reshape(batch, depth, seq, KV_HEADS, HEAD_DIM), new_gdn, new_ssm)
```
